```python
import math
import jax, jax.numpy as jnp
from jax import lax
import numpy as np

D_MODEL = 1024
BATCH = 16
SEQ = 256
DEPTH = 2
DEC_BATCH = 2
DEC_SEQ = 2048
PAST_LEN = 256

GRID_W = 64
D_MIX = D_MODEL
D_ATTN = D_MIX // 2
D_CONV = D_MIX - D_ATTN
N_HEADS = 4
HEAD_DIM = D_ATTN // (2 * N_HEADS)
V_DIM = 2 * HEAD_DIM
CONV_W = 3
D_FF = 2816
ROPE_BASE = 10000.0
EPS = 1e-6
Q_BLOCK = 128
N_MOD = 6
D_IN = 3 * D_ATTN + 3 * D_CONV
SPLITS = (D_ATTN, 2 * D_ATTN, 3 * D_ATTN, 3 * D_ATTN + D_CONV, 3 * D_ATTN + 2 * D_CONV)

kernel_name = 'hybrid_diffattn_shortconv_dit_step'


def _rmsnorm(x, g):
    xf = x.astype(jnp.float32)
    r = lax.rsqrt(jnp.mean(xf * xf, axis=-1, keepdims=True) + EPS)
    return (xf * r).astype(x.dtype) * g


def _dwconv3(u, w):
    up = jnp.pad(u, ((0, 0), (1, 1), (0, 0)))
    return up[:, :-2] * w[0] + up[:, 1:-1] * w[1] + up[:, 2:] * w[2]


def _grid_angles(n_tok):
    rows = n_tok // GRID_W
    row = jnp.repeat(jnp.arange(rows, dtype=jnp.float32), GRID_W)
    col = jnp.tile(jnp.arange(GRID_W, dtype=jnp.float32), rows)
    n_freq = HEAD_DIM // 4
    inv = ROPE_BASE ** (-jnp.arange(n_freq, dtype=jnp.float32) / n_freq)
    return row[:, None] * inv[None], col[:, None] * inv[None]


def _rot_axis(x, ang):
    half = x.shape[-1] // 2
    cos = jnp.cos(ang)[None, :, None, None, :].astype(x.dtype)
    sin = jnp.sin(ang)[None, :, None, None, :].astype(x.dtype)
    x1, x2 = x[..., :half], x[..., half:]
    return jnp.concatenate([x1 * cos - x2 * sin, x2 * cos + x1 * sin], axis=-1)


def _rope_2d(x, ang_row, ang_col):
    h = HEAD_DIM // 2
    return jnp.concatenate([_rot_axis(x[..., :h], ang_row), _rot_axis(x[..., h:], ang_col)], axis=-1)


def _diff_attention(q, k, v, lam):
    bsz, lq = q.shape[:2]
    nb = lq // Q_BLOCK
    qb = q.reshape(bsz, nb, Q_BLOCK, N_HEADS, 2, HEAD_DIM).transpose(1, 0, 2, 3, 4, 5)
    scale = HEAD_DIM ** -0.5

    def block(qblk):
        s = jnp.einsum('bqhsd,bkhsd->bhsqk', qblk, k).astype(jnp.float32) * scale
        p = jax.nn.softmax(s, axis=-1)
        a = p[:, :, 0] - lam * p[:, :, 1]
        return jnp.einsum('bhqk,bkhe->bqhe', a.astype(v.dtype), v)

    o = lax.map(block, qb)
    return o.transpose(1, 0, 2, 3, 4).reshape(bsz, lq, N_HEADS, V_DIM)


def setup_inputs(seed: int = 0) -> dict:
    key = jax.random.key(seed)
    ks = jax.random.split(key, 24)
    f32 = jnp.float32
    nrm = lambda k, shape, s: jax.random.normal(k, shape, f32) * s
    return {
        'x_prompt': nrm(ks[0], (BATCH, SEQ, D_MODEL), 1.0),
        'x_sample': nrm(ks[1], (DEC_BATCH, DEC_SEQ, D_MODEL), 1.0),
        'cache_k': nrm(ks[2], (DEC_BATCH, DEPTH, PAST_LEN, N_HEADS, 2, HEAD_DIM), 1.0),
        'cache_v': nrm(ks[3], (DEC_BATCH, DEPTH, PAST_LEN, N_HEADS, V_DIM), 1.0),
        'c': nrm(ks[4], (DEC_BATCH, D_MODEL), 1.0),
        'c_ctx': nrm(ks[5], (D_MODEL,), 1.0),
        'w_mod': nrm(ks[6], (DEPTH, D_MODEL, N_MOD * D_MODEL), D_MODEL ** -0.5),
        'b_mod': nrm(ks[7], (DEPTH, N_MOD * D_MODEL), 0.02),
        'norm1_g': 1.0 + nrm(ks[8], (DEPTH, D_MODEL), 0.02),
        'w_in': nrm(ks[9], (DEPTH, D_MODEL, D_IN), D_MODEL ** -0.5),
        'q_norm_g': 1.0 + nrm(ks[10], (DEPTH, HEAD_DIM), 0.02),
        'k_norm_g': 1.0 + nrm(ks[11], (DEPTH, HEAD_DIM), 0.02),
        'lambda_q1': nrm(ks[12], (DEPTH, HEAD_DIM), 0.1),
        'lambda_k1': nrm(ks[13], (DEPTH, HEAD_DIM), 0.1),
        'lambda_q2': nrm(ks[14], (DEPTH, HEAD_DIM), 0.1),
        'lambda_k2': nrm(ks[15], (DEPTH, HEAD_DIM), 0.1),
        'subln_g': 1.0 + nrm(ks[16], (DEPTH, V_DIM), 0.02),
        'conv_w': nrm(ks[17], (DEPTH, CONV_W, D_CONV), CONV_W ** -0.5),
        'conv_norm_g': 1.0 + nrm(ks[18], (DEPTH, D_CONV), 0.02),
        'w_out': nrm(ks[19], (DEPTH, D_MIX, D_MODEL), D_MIX ** -0.5),
        'norm2_g': 1.0 + nrm(ks[20], (DEPTH, D_MODEL), 0.02),
        'w_up': nrm(ks[21], (DEPTH, D_MODEL, 2 * D_FF), D_MODEL ** -0.5),
        'ffn_conv_w': nrm(ks[22], (DEPTH, CONV_W, 2 * D_FF), CONV_W ** -0.5),
        'w_down': nrm(ks[23], (DEPTH, D_FF, D_MODEL), D_FF ** -0.5),
    }


def reference(x_prompt, x_sample, cache_k, cache_v, c, c_ctx, w_mod, b_mod, norm1_g, w_in,
              q_norm_g, k_norm_g, lambda_q1, lambda_k1, lambda_q2, lambda_k2, subln_g,
              conv_w, conv_norm_g, w_out, norm2_g, w_up, ffn_conv_w, w_down):

    def layer(l, x, cvec, ctx_kv, ang):
        m = (jax.nn.silu(cvec) @ w_mod[l] + b_mod[l])[..., None, :]
        sh1, sc1, g1, sh2, sc2, g2 = jnp.split(m, N_MOD, axis=-1)
        h = _rmsnorm(x, norm1_g[l]) * (1 + sc1) + sh1
        bsz, n = x.shape[:2]
        q, k, v, bg, cg, xc = jnp.split(h @ w_in[l], SPLITS, axis=-1)
        q = _rmsnorm(q.reshape(bsz, n, N_HEADS, 2, HEAD_DIM), q_norm_g[l])
        k = _rmsnorm(k.reshape(bsz, n, N_HEADS, 2, HEAD_DIM), k_norm_g[l])
        v = v.reshape(bsz, n, N_HEADS, V_DIM)
        if ang is None:
            k_all, v_all = k, v
        else:
            q = _rope_2d(q, *ang)
            k = _rope_2d(k, *ang)
            k_all = jnp.concatenate([ctx_kv[0], k], axis=1)
            v_all = jnp.concatenate([ctx_kv[1], v], axis=1)
        lam_init = 0.8 - 0.6 * math.exp(-0.3 * l)
        f = lambda a: a.astype(jnp.float32)
        lam = (jnp.exp(jnp.sum(f(lambda_q1[l]) * f(lambda_k1[l])))
               - jnp.exp(jnp.sum(f(lambda_q2[l]) * f(lambda_k2[l]))) + lam_init)
        o = _diff_attention(q, k_all, v_all, lam)
        o = (_rmsnorm(o, subln_g[l]) * (1.0 - lam_init)).reshape(bsz, n, D_ATTN)
        y = _rmsnorm(bg * _dwconv3(cg * xc, conv_w[l]), conv_norm_g[l])
        x = x + g1 * (jnp.concatenate([o, y], axis=-1) @ w_out[l])
        h2 = _rmsnorm(x, norm2_g[l]) * (1 + sc2) + sh2
        a, b = jnp.split(_dwconv3(h2 @ w_up[l], ffn_conv_w[l]), 2, axis=-1)
        x = x + g2 * ((jax.nn.silu(a) * b) @ w_down[l])
        return x, k, v

    xp = x_prompt
    ks_new, vs_new = [], []
    for l in range(DEPTH):
        xp, k_l, v_l = layer(l, xp, c_ctx, None, None)
        ks_new.append(k_l)
        vs_new.append(v_l)
    y_prompt = xp
    new_k = jnp.stack(ks_new, axis=1)
    new_v = jnp.stack(vs_new, axis=1)

    ang = _grid_angles(x_sample.shape[1])
    xs = x_sample
    for l in range(DEPTH):
        xs, _, _ = layer(l, xs, c, (cache_k[:, l], cache_v[:, l]), ang)
    y_sample = xs

    return (y_prompt, y_sample, new_k, new_v)
```

```python
import functools
import math

import jax
import jax.numpy as jnp
from jax import lax
from jax.experimental import pallas as pl
from jax.experimental.pallas import tpu as pltpu

D_MODEL = 1024
DEPTH = 2
GRID_W = 64
D_ATTN = 512
D_CONV = 512
N_HEADS = 4
HEAD_DIM = 64
V_DIM = 128
D_FF = 2816
ROPE_BASE = 10000.0
EPS = 1e-6
N_MOD = 6
D_IN = 3 * D_ATTN + 3 * D_CONV

F32 = jnp.float32
BF16 = jnp.bfloat16

LANES = 128
BF16_ROWS = 16
MOD_ROWS = 8
VMEM_LIMIT = 56 * 1024 * 1024

FF_CHUNK = 256
N_FF_CHUNKS = D_FF // FF_CHUNK
HALO = BF16_ROWS


def _cparams(n_axes):
    return pltpu.CompilerParams(
        dimension_semantics=("arbitrary",) * n_axes, vmem_limit_bytes=VMEM_LIMIT)


def _const_spec(shape):
    nd = len(shape)
    return pl.BlockSpec(shape, lambda *_: (0,) * nd, pipeline_mode=pl.Buffered(1))


def _rms_rows(x):
    return lax.rsqrt(jnp.mean(x * x, axis=-1, keepdims=True) + EPS)


def _mod_kernel(cv_ref, w_ref, b_ref, o_ref):
    cv = cv_ref[...]
    s = (cv * jax.nn.sigmoid(cv)).astype(BF16)
    o_ref[...] = jnp.dot(s, w_ref[...].astype(BF16), preferred_element_type=F32) + b_ref[...]


def _modulation(cvecs, w_mod, b_mod):
    tn = 1536
    n_out = N_MOD * D_MODEL
    return pl.pallas_call(
        _mod_kernel,
        grid=(DEPTH, n_out // tn),
        in_specs=[
            pl.BlockSpec((MOD_ROWS, D_MODEL), lambda l, j: (0, 0)),
            pl.BlockSpec((None, D_MODEL, tn), lambda l, j: (l, 0, j)),
            pl.BlockSpec((None, 1, tn), lambda l, j: (l, 0, j)),
        ],
        out_specs=pl.BlockSpec((None, MOD_ROWS, tn), lambda l, j: (l, 0, j)),
        out_shape=jax.ShapeDtypeStruct((DEPTH, MOD_ROWS, n_out), F32),
        compiler_params=_cparams(2),
        name="adaln_mod",
    )(cvecs, w_mod, b_mod.reshape(DEPTH, 1, n_out))


def _group_norm_gain(z, gmat_ref, gain):
    ms = jnp.dot((z * z).astype(BF16), gmat_ref[...], preferred_element_type=F32)
    return z * lax.rsqrt(ms + EPS) * gain


def _rope(z, c_ref, sa_ref, sb_ref):
    c, sa, sb = c_ref[...], sa_ref[...], sb_ref[...]
    cols = []
    for j in range(z.shape[1] // LANES):
        zj = z[:, j * LANES:(j + 1) * LANES]
        hi = pltpu.roll(zj, LANES - HEAD_DIM // 4, axis=1)
        lo = pltpu.roll(zj, HEAD_DIM // 4, axis=1)
        cols.append(zj * c + hi * sa + lo * sb)
    return jnp.concatenate(cols, axis=1)


def _proj_kernel(*refs, rope, emit_f32):
    it = iter(refs)
    x_ref, m_ref, g1_ref, w_ref, qg_ref, kg_ref, gmat_ref = (next(it) for _ in range(7))
    if rope:
        c_ref, sa_ref, sb_ref = (next(it) for _ in range(3))
    q_ref, k_ref, v_ref, u_ref, bg_ref = (next(it) for _ in range(5))
    if emit_f32:
        kf_ref, vf_ref = next(it), next(it)

    x = x_ref[...]
    m = m_ref[...]
    sh1 = m[:, 0:D_MODEL]
    sc1 = m[:, D_MODEL:2 * D_MODEL]
    h = ((x * _rms_rows(x)) * g1_ref[...]) * (1.0 + sc1) + sh1
    hb = h.astype(BF16)

    def col(i0, n):
        return jnp.dot(hb, w_ref[:, i0:i0 + n], preferred_element_type=F32)

    q = _group_norm_gain(col(0, D_ATTN), gmat_ref, qg_ref[...])
    k = _group_norm_gain(col(D_ATTN, D_ATTN), gmat_ref, kg_ref[...])
    if rope:
        q = _rope(q, c_ref, sa_ref, sb_ref)
        k = _rope(k, c_ref, sa_ref, sb_ref)
    v = col(2 * D_ATTN, D_ATTN)
    q_ref[...] = (q * (HEAD_DIM ** -0.5)).astype(BF16)
    k_ref[...] = k.astype(BF16)
    v_ref[...] = v.astype(BF16)
    if emit_f32:
        kf_ref[...] = k
        vf_ref[...] = v
    bg_ref[...] = col(3 * D_ATTN, D_CONV).astype(BF16)
    cg = col(3 * D_ATTN + D_CONV, D_CONV)
    xc = col(3 * D_ATTN + 2 * D_CONV, D_CONV)
    u_ref[...] = (cg * xc).astype(BF16)


def _proj(x, mod, layer, row0, mod_span, seq_len, tile, norm_g, w_in, qg, kg, gmat, rope_tabs,
          emit_f32):
    n = x.shape[0]
    tiles_per_seq = max(seq_len // tile, 1)
    in_specs = [
        pl.BlockSpec((tile, D_MODEL), lambda i: (i, 0)),
        pl.BlockSpec((None, None, 1, N_MOD * D_MODEL),
                     lambda i: (layer, row0 + (i * tile) // mod_span, 0, 0)),
        _const_spec((1, D_MODEL)),
        _const_spec((D_MODEL, D_IN)),
        _const_spec((1, D_ATTN)),
        _const_spec((1, D_ATTN)),
        _const_spec((D_ATTN, D_ATTN)),
    ]
    args = [x, mod, norm_g, w_in, qg, kg, gmat]
    if rope_tabs is not None:
        for t in rope_tabs:
            in_specs.append(pl.BlockSpec((tile, LANES), lambda i: (i % tiles_per_seq, 0)))
            args.append(t)
    tok = lambda dt: jax.ShapeDtypeStruct((n, D_ATTN), dt)
    out_shape = [tok(BF16)] * 5 + ([tok(F32)] * 2 if emit_f32 else [])
    out_specs = [pl.BlockSpec((tile, D_ATTN), lambda i: (i, 0)) for _ in out_shape]
    return pl.pallas_call(
        functools.partial(_proj_kernel, rope=rope_tabs is not None, emit_f32=emit_f32),
        grid=(n // tile,),
        in_specs=in_specs,
        out_specs=out_specs,
        out_shape=out_shape,
        compiler_params=_cparams(1),
        name="proj",
    )(*args)


def _shift_rows(u, prev_row, next_row, row_in_seq, seq_len):
    t = u.shape[0]
    ridx = lax.broadcasted_iota(jnp.int32, (t, 1), 0)
    up = jnp.where(ridx == 0, prev_row, pltpu.roll(u, 1, axis=0))
    dn = jnp.where(ridx == t - 1, next_row, pltpu.roll(u, t - 1, axis=0))
    up = jnp.where(row_in_seq == 0, 0.0, up)
    dn = jnp.where(row_in_seq == seq_len - 1, 0.0, dn)
    return up, dn


def _attn_kernel(*refs, has_cache, tq, seq_len, lam_init):
    it = iter(refs)
    x_ref, m_ref, q_ref, k_ref, v_ref = (next(it) for _ in range(5))
    if has_cache:
        kc_ref, vc_ref = next(it), next(it)
    u_ref, up_ref, un_ref, bg_ref = (next(it) for _ in range(4))
    lam_ref, sg_ref, cw_ref, cg_ref, wo_ref = (next(it) for _ in range(5))
    o_ref = next(it)

    lp = lam_ref[...]
    lam = (jnp.exp(jnp.sum(lp[0:1] * lp[1:2], axis=-1, keepdims=True))
           - jnp.exp(jnp.sum(lp[2:3] * lp[3:4], axis=-1, keepdims=True)) + lam_init)

    lane = lax.broadcasted_iota(jnp.int32, (1, LANES), 1)
    sub0 = jnp.where(lane < HEAD_DIM, 1.0, 0.0).astype(BF16)
    sub1 = jnp.where(lane < HEAD_DIM, 0.0, 1.0).astype(BF16)
    nt = (((1,), (1,)), ((), ()))
    parts = []
    for h in range(N_HEADS):
        sl = slice(h * LANES, (h + 1) * LANES)
        qh = q_ref[:, sl]
        q2 = jnp.concatenate([qh * sub0, qh * sub1], axis=0)
        s = lax.dot_general(q2, k_ref[:, sl], nt, preferred_element_type=F32)
        mx = jnp.max(s, axis=-1, keepdims=True)
        if has_cache:
            sc = lax.dot_general(q2, kc_ref[:, sl].astype(BF16), nt, preferred_element_type=F32)
            mx = jnp.maximum(mx, jnp.max(sc, axis=-1, keepdims=True))
        e = jnp.exp(s - mx)
        den = jnp.sum(e, axis=-1, keepdims=True)
        if has_cache:
            ec = jnp.exp(sc - mx)
            den = den + jnp.sum(ec, axis=-1, keepdims=True)
        inv = 1.0 / den
        c0 = inv[:tq]
        c1 = inv[tq:] * lam
        a = (e[:tq] * c0 - e[tq:] * c1).astype(BF16)
        oh = jnp.dot(a, v_ref[:, sl], preferred_element_type=F32)
        if has_cache:
            ac = (ec[:tq] * c0 - ec[tq:] * c1).astype(BF16)
            oh = oh + jnp.dot(ac, vc_ref[:, sl].astype(BF16), preferred_element_type=F32)
        parts.append(((oh * _rms_rows(oh)) * sg_ref[...]) * (1.0 - lam_init))

    j = pl.program_id(1)
    row_in_seq = (j * tq + lax.broadcasted_iota(jnp.int32, (tq, 1), 0)) % seq_len
    u = u_ref[...].astype(F32)
    up, dn = _shift_rows(u, up_ref[BF16_ROWS - 1:BF16_ROWS, :].astype(F32),
                         un_ref[0:1, :].astype(F32), row_in_seq, seq_len)
    cw = cw_ref[...]
    t = bg_ref[...].astype(F32) * (up * cw[0:1] + u * cw[1:2] + dn * cw[2:3])
    parts.append((t * _rms_rows(t)) * cg_ref[...])

    cat = jnp.concatenate(parts, axis=1).astype(BF16)
    m = m_ref[...]
    g1 = m[:, 2 * D_MODEL:3 * D_MODEL]
    o_ref[...] = x_ref[...] + g1 * jnp.dot(cat, wo_ref[...], preferred_element_type=F32)


def _attn(x, mod, layer, row0, mod_span, batch, seq_len, tq, q, k, v, cache, u, bg, lam_p, subln_g,
          conv_w, conv_g, w_out, lam_init):
    qt = seq_len // tq
    hb = tq // BF16_ROWS
    n_hblocks = x.shape[0] // BF16_ROWS
    tokmap = lambda b, j: (b * qt + j, 0)
    in_specs = [
        pl.BlockSpec((tq, D_MODEL), tokmap),
        pl.BlockSpec((None, None, 1, N_MOD * D_MODEL),
                     lambda b, j: (layer, row0 + (b * seq_len) // mod_span, 0, 0)),
        pl.BlockSpec((tq, D_ATTN), tokmap),
        pl.BlockSpec((seq_len, D_ATTN), lambda b, j: (b, 0)),
        pl.BlockSpec((seq_len, D_ATTN), lambda b, j: (b, 0)),
    ]
    args = [x, mod, q, k, v]
    if cache is not None:
        ck, cv = cache
        p = ck.shape[2]
        for c in (ck, cv):
            in_specs.append(pl.BlockSpec((None, None, p, D_ATTN), lambda b, j: (b, layer, 0, 0)))
            args.append(c)
    in_specs += [
        pl.BlockSpec((tq, D_CONV), tokmap),
        pl.BlockSpec((BF16_ROWS, D_CONV), lambda b, j: (jnp.maximum((b * qt + j) * hb - 1, 0), 0)),
        pl.BlockSpec((BF16_ROWS, D_CONV),
                     lambda b, j: (jnp.minimum((b * qt + j + 1) * hb, n_hblocks - 1), 0)),
        pl.BlockSpec((tq, D_CONV), tokmap),
        _const_spec((4, HEAD_DIM)),
        _const_spec((1, V_DIM)),
        _const_spec((3, D_CONV)),
        _const_spec((1, D_CONV)),
        _const_spec((D_MODEL, D_MODEL)),
    ]
    args += [u, u, u, bg, lam_p, subln_g, conv_w, conv_g, w_out]
    return pl.pallas_call(
        functools.partial(_attn_kernel, has_cache=cache is not None, tq=tq, seq_len=seq_len,
                          lam_init=lam_init),
        grid=(batch, qt),
        in_specs=in_specs,
        out_specs=pl.BlockSpec((tq, D_MODEL), tokmap),
        out_shape=jax.ShapeDtypeStruct(x.shape, F32),
        compiler_params=_cparams(2),
        name="attn",
    )(*args)


def _ffn_kernel(*refs, tile, seq_len, halo):
    it = iter(refs)
    x_ref = next(it)
    if halo:
        xp_ref, xn_ref = next(it), next(it)
    m_ref, g2_ref, wup_ref, cw_ref, wdn_ref, o_ref, hs_ref, acc_ref = (next(it) for _ in range(8))

    m = m_ref[...]
    sh2 = m[:, 3 * D_MODEL:4 * D_MODEL]
    sc2 = m[:, 4 * D_MODEL:5 * D_MODEL]
    gate = m[:, 5 * D_MODEL:6 * D_MODEL]

    def pre(xv):
        return ((xv * _rms_rows(xv)) * g2_ref[...]) * (1.0 + sc2) + sh2

    x = x_ref[...]
    i = pl.program_id(0)
    base = HALO if halo else 0
    hs_ref[base:base + tile, :] = pre(x).astype(BF16)
    if halo:
        tiles_per_seq = seq_len // tile
        jt = i % tiles_per_seq
        hs_ref[0:HALO, :] = jnp.where(jt == 0, 0.0, pre(xp_ref[...])).astype(BF16)
        hs_ref[HALO + tile:, :] = jnp.where(jt == tiles_per_seq - 1, 0.0,
                                            pre(xn_ref[...])).astype(BF16)
    rows = tile + 2 * base
    if not halo:
        row_in_seq = lax.broadcasted_iota(jnp.int32, (tile, 1), 0) % seq_len

    def conv(uu, w3):
        up = pltpu.roll(uu, 1, axis=0)
        dn = pltpu.roll(uu, rows - 1, axis=0)
        if halo:
            up, mid, dn = up[base:base + tile], uu[base:base + tile], dn[base:base + tile]
        else:
            mid = uu
            up = jnp.where(row_in_seq == 0, 0.0, up)
            dn = jnp.where(row_in_seq == seq_len - 1, 0.0, dn)
        return up * w3[0:1] + mid * w3[1:2] + dn * w3[2:3]

    acc_ref[...] = jnp.zeros_like(acc_ref)

    def body(c, carry):
        hs = hs_ref[...]
        ua = jnp.dot(hs, wup_ref[c], preferred_element_type=F32)
        ub = jnp.dot(hs, wup_ref[c + N_FF_CHUNKS], preferred_element_type=F32)
        ca = conv(ua, cw_ref[c])
        cb = conv(ub, cw_ref[c + N_FF_CHUNKS])
        g = ((ca * jax.nn.sigmoid(ca)) * cb).astype(BF16)
        acc_ref[...] += jnp.dot(g, wdn_ref[c], preferred_element_type=F32)
        return carry

    lax.fori_loop(0, N_FF_CHUNKS, body, 0)
    o_ref[...] = x + gate * acc_ref[...]


def _ffn(x, mod, layer, row0, mod_span, seq_len, tile, halo, norm_g, w_up3, conv_w3, w_dn3):
    n = x.shape[0]
    hb = tile // BF16_ROWS
    n_hblocks = n // BF16_ROWS
    in_specs = [pl.BlockSpec((tile, D_MODEL), lambda i: (i, 0))]
    args = [x]
    if halo:
        in_specs += [
            pl.BlockSpec((HALO, D_MODEL), lambda i: (jnp.maximum(i * hb - 1, 0), 0)),
            pl.BlockSpec((HALO, D_MODEL), lambda i: (jnp.minimum((i + 1) * hb, n_hblocks - 1), 0)),
        ]
        args += [x, x]
    in_specs += [
        pl.BlockSpec((None, None, 1, N_MOD * D_MODEL),
                     lambda i: (layer, row0 + (i * tile) // mod_span, 0, 0)),
        _const_spec((1, D_MODEL)),
        _const_spec((2 * N_FF_CHUNKS, D_MODEL, FF_CHUNK)),
        _const_spec((2 * N_FF_CHUNKS, 3, FF_CHUNK)),
        _const_spec((N_FF_CHUNKS, FF_CHUNK, D_MODEL)),
    ]
    args += [mod, norm_g, w_up3, conv_w3, w_dn3]
    rows = tile + (2 * HALO if halo else 0)
    return pl.pallas_call(
        functools.partial(_ffn_kernel, tile=tile, seq_len=seq_len, halo=halo),
        grid=(n // tile,),
        in_specs=in_specs,
        out_specs=pl.BlockSpec((tile, D_MODEL), lambda i: (i, 0)),
        out_shape=jax.ShapeDtypeStruct(x.shape, F32),
        scratch_shapes=[pltpu.VMEM((rows, D_MODEL), BF16), pltpu.VMEM((tile, D_MODEL), F32)],
        compiler_params=_cparams(1),
        name="ffn",
    )(*args)


def _rope_tables(n_tok):
    rows = n_tok // GRID_W
    row = jnp.repeat(jnp.arange(rows, dtype=F32), GRID_W)
    col = jnp.tile(jnp.arange(GRID_W, dtype=F32), rows)
    n_freq = HEAD_DIM // 4
    inv = ROPE_BASE ** (-jnp.arange(n_freq, dtype=F32) / n_freq)
    ar, ac = row[:, None] * inv[None], col[:, None] * inv[None]
    cr, sr, cc, sc = jnp.cos(ar), jnp.sin(ar), jnp.cos(ac), jnp.sin(ac)
    z = jnp.zeros_like(sr)
    rep = lambda parts: jnp.tile(jnp.concatenate(parts, axis=-1), (1, LANES // HEAD_DIM))
    return rep([cr, cr, cc, cc]), rep([-sr, z, -sc, z]), rep([z, sr, z, sc])


def kernel(x_prompt, x_sample, cache_k, cache_v, c, c_ctx, w_mod, b_mod, norm1_g, w_in, q_norm_g, k_norm_g, lambda_q1, lambda_k1, lambda_q2, lambda_k2, subln_g, conv_w, conv_norm_g, w_out, norm2_g, w_up, ffn_conv_w, w_down):
    batch, seq, _ = x_prompt.shape
    dec_batch, dec_seq, _ = x_sample.shape
    past = cache_k.shape[2]

    cvecs = jnp.zeros((MOD_ROWS, D_MODEL), F32).at[0].set(c_ctx).at[1:1 + dec_batch].set(c)
    mod = _modulation(cvecs, w_mod, b_mod).reshape(DEPTH, MOD_ROWS, 1, N_MOD * D_MODEL)

    w_in_b = w_in.astype(BF16)
    w_out_b = w_out.astype(BF16)
    w_up3 = w_up.astype(BF16).reshape(DEPTH, D_MODEL, 2 * N_FF_CHUNKS, FF_CHUNK).transpose(0, 2, 1, 3)
    w_dn3 = w_down.astype(BF16).reshape(DEPTH, N_FF_CHUNKS, FF_CHUNK, D_MODEL)
    cw3 = ffn_conv_w.reshape(DEPTH, 3, 2 * N_FF_CHUNKS, FF_CHUNK).transpose(0, 2, 1, 3)
    grp = jnp.arange(D_ATTN) // HEAD_DIM
    gmat = jnp.where(grp[:, None] == grp[None, :], 1.0 / HEAD_DIM, 0.0).astype(BF16)
    tile_g = lambda g: jnp.tile(g, (1, D_ATTN // HEAD_DIM))
    rope_tabs = _rope_tables(dec_seq)
    ck = cache_k.reshape(dec_batch, DEPTH, past, D_ATTN)
    cv = cache_v.reshape(dec_batch, DEPTH, past, D_ATTN)

    def run(x, row0, mod_span, n_batch, seq_len, tile, tq, ffn_halo, rope, cache, emit):
        kv = []
        for l in range(DEPTH):
            lam_init = 0.8 - 0.6 * math.exp(-0.3 * l)
            lam_p = jnp.stack([lambda_q1[l], lambda_k1[l], lambda_q2[l], lambda_k2[l]])
            outs = _proj(x, mod, l, row0, mod_span, seq_len, tile, norm1_g[l][None], w_in_b[l],
                         tile_g(q_norm_g[l][None]), tile_g(k_norm_g[l][None]), gmat, rope, emit)
            q, k, v, u, bg = outs[:5]
            if emit:
                kv.append(outs[5:])
            x = _attn(x, mod, l, row0, mod_span, n_batch, seq_len, tq, q, k, v, cache, u, bg, lam_p,
                      subln_g[l][None], conv_w[l], conv_norm_g[l][None], w_out_b[l], lam_init)
            x = _ffn(x, mod, l, row0, mod_span, seq_len, tile, ffn_halo, norm2_g[l][None], w_up3[l],
                     cw3[l], w_dn3[l])
        return x, kv

    xp, kv = run(x_prompt.reshape(batch * seq, D_MODEL), 0, batch * seq, batch, seq, 512, seq,
                 False, None, None, True)
    xs, _ = run(x_sample.reshape(dec_batch * dec_seq, D_MODEL), 1, dec_seq, dec_batch, dec_seq,
                512, 256, True, rope_tabs, (ck, cv), False)

    new_k = jnp.stack([kv[l][0].reshape(batch, seq, N_HEADS, 2, HEAD_DIM) for l in range(DEPTH)], axis=1)
    new_v = jnp.stack([kv[l][1].reshape(batch, seq, N_HEADS, V_DIM) for l in range(DEPTH)], axis=1)
    return (xp.reshape(batch, seq, D_MODEL), xs.reshape(dec_batch, dec_seq, D_MODEL), new_k, new_v)
```

```python
import functools
import math

import jax
import jax.numpy as jnp
from jax import lax
from jax.experimental import pallas as pl
from jax.experimental.pallas import tpu as pltpu

D_MODEL = 1024
DEPTH = 2
GRID_W = 64
D_ATTN = 512
D_CONV = 512
N_HEADS = 4
HEAD_DIM = 64
V_DIM = 128
D_FF = 2816
ROPE_BASE = 10000.0
EPS = 1e-6
N_MOD = 6
D_IN = 3 * D_ATTN + 3 * D_CONV

F32 = jnp.float32
BF16 = jnp.bfloat16

LANES = 128
BF16_ROWS = 16
F32_ROWS = 8
MOD_ROWS = F32_ROWS
_U_PAD = F32_ROWS
VMEM_LIMIT = 60 * 1024 * 1024

FF_CHUNK = 256
N_FF_CHUNKS = D_FF // FF_CHUNK
HALO = BF16_ROWS
TOKEN_TILE = 512
Q_TILE = 256


def _cparams(n_axes):
    return pltpu.CompilerParams(
        dimension_semantics=("arbitrary",) * n_axes, vmem_limit_bytes=VMEM_LIMIT)


def _layer_spec(shape, layer):
    nd = len(shape)
    return pl.BlockSpec((None,) + tuple(shape), lambda *_: (layer,) + (0,) * nd,
                        pipeline_mode=pl.Buffered(1))


def _const_spec(shape):
    nd = len(shape)
    return pl.BlockSpec(shape, lambda *_: (0,) * nd, pipeline_mode=pl.Buffered(1))


def _mod_spec(layer, row_of):
    return pl.BlockSpec((None, None, 1, N_MOD * D_MODEL),
                        lambda *idx: (layer, row_of(*idx), 0, 0))


def _rms_rows(x):
    return lax.rsqrt(jnp.mean(x * x, axis=-1, keepdims=True) + EPS)


def _mod_kernel(cv_ref, w_ref, b_ref, o_ref):
    cv = cv_ref[...]
    s = (cv * jax.nn.sigmoid(cv)).astype(BF16)
    o_ref[...] = jnp.dot(s, w_ref[...].astype(BF16), preferred_element_type=F32) + b_ref[...]


def _modulation(cvecs, w_mod, b_mod):
    tn = 1536
    n_out = N_MOD * D_MODEL
    return pl.pallas_call(
        _mod_kernel,
        grid=(DEPTH, n_out // tn),
        in_specs=[
            pl.BlockSpec((MOD_ROWS, D_MODEL), lambda l, j: (0, 0)),
            pl.BlockSpec((None, D_MODEL, tn), lambda l, j: (l, 0, j)),
            pl.BlockSpec((None, 1, tn), lambda l, j: (l, 0, j)),
        ],
        out_specs=pl.BlockSpec((None, MOD_ROWS, tn), lambda l, j: (l, 0, j)),
        out_shape=jax.ShapeDtypeStruct((DEPTH, MOD_ROWS, n_out), F32),
        compiler_params=_cparams(2),
        name="adaln_mod",
    )(cvecs, w_mod, b_mod.reshape(DEPTH, 1, n_out))


def _group_norm_gain(z, gmat_ref, gain):
    ms = jnp.dot((z * z).astype(BF16), gmat_ref[...], preferred_element_type=F32)
    return z * lax.rsqrt(ms + EPS) * gain


def _rope(z, c_ref, sa_ref, sb_ref):
    c, sa, sb = c_ref[...], sa_ref[...], sb_ref[...]
    cols = []
    for j in range(z.shape[1] // LANES):
        zj = z[:, j * LANES:(j + 1) * LANES]
        hi = pltpu.roll(zj, LANES - HEAD_DIM // 4, axis=1)
        lo = pltpu.roll(zj, HEAD_DIM // 4, axis=1)
        cols.append(zj * c + hi * sa + lo * sb)
    return jnp.concatenate(cols, axis=1)


def _proj_kernel(*refs, rope, emit_f32, seq_len):
    it = iter(refs)
    x_ref, m_ref, g1_ref, w_ref, qg_ref, kg_ref, gmat_ref = (next(it) for _ in range(7))
    if rope:
        c_ref, sa_ref, sb_ref = (next(it) for _ in range(3))
    if emit_f32 == "alias":
        next(it), next(it)
    q_ref, k_ref, v_ref, u_ref, bg_ref = (next(it) for _ in range(5))
    if emit_f32:
        kf_ref, vf_ref = next(it), next(it)

    x = x_ref[...]
    m = m_ref[...]
    sh1 = m[:, 0:D_MODEL]
    sc1 = m[:, D_MODEL:2 * D_MODEL]
    h = ((x * _rms_rows(x)) * g1_ref[...]) * (1.0 + sc1) + sh1
    hb = h.astype(BF16)

    def col(i0, n):
        return jnp.dot(hb, w_ref[:, i0:i0 + n].astype(BF16), preferred_element_type=F32)

    q = _group_norm_gain(col(0, D_ATTN), gmat_ref, qg_ref[...])
    k = _group_norm_gain(col(D_ATTN, D_ATTN), gmat_ref, kg_ref[...])
    if rope:
        q = _rope(q, c_ref, sa_ref, sb_ref)
        k = _rope(k, c_ref, sa_ref, sb_ref)
    v = col(2 * D_ATTN, D_ATTN)
    q_ref[...] = (q * (HEAD_DIM ** -0.5)).astype(BF16)
    k_ref[...] = k.astype(BF16)
    v_ref[...] = v.astype(BF16)
    if emit_f32:
        n_seq = x.shape[0] // seq_len
        kf_ref[...] = k.reshape(n_seq, seq_len, D_ATTN)
        vf_ref[...] = v.reshape(n_seq, seq_len, D_ATTN)
    bg_ref[...] = col(3 * D_ATTN, D_CONV).astype(BF16)
    cg = col(3 * D_ATTN + D_CONV, D_CONV)
    xc = col(3 * D_ATTN + 2 * D_CONV, D_CONV)
    u_ref[...] = (cg * xc).astype(BF16)


def _proj(x, mod, layer, row0, mod_span, seq_len, norm1_g, w_in, qg, kg, gmat, rope_tabs, kv_out):
    n = x.shape[0]
    tile = TOKEN_TILE
    tiles_per_seq = max(seq_len // tile, 1)
    in_specs = [
        pl.BlockSpec((tile, D_MODEL), lambda i: (i, 0)),
        _mod_spec(layer, lambda i: row0 + (i * tile) // mod_span),
        _layer_spec((1, D_MODEL), layer),
        _layer_spec((D_MODEL, D_IN), layer),
        _layer_spec((1, D_ATTN), layer),
        _layer_spec((1, D_ATTN), layer),
        _const_spec((D_ATTN, D_ATTN)),
    ]
    args = [x, mod, norm1_g, w_in, qg, kg, gmat]
    if rope_tabs is not None:
        for t in rope_tabs:
            in_specs.append(pl.BlockSpec((tile, LANES), lambda i: (i % tiles_per_seq, 0)))
            args.append(t)
    aliases = {}
    emit = False
    if kv_out is not None and kv_out != "new":
        emit = "alias"
        for a in kv_out:
            aliases[len(args)] = 5 + len(aliases)
            in_specs.append(pl.BlockSpec(memory_space=pl.ANY))
            args.append(a)
    elif kv_out == "new":
        emit = True
    tok = lambda dt: jax.ShapeDtypeStruct((n, D_ATTN), dt)
    out_shape = [tok(BF16)] * 5
    out_specs = [pl.BlockSpec((tile, D_ATTN), lambda i: (i, 0)) for _ in out_shape]
    if emit:
        n_seq = tile // seq_len
        kv_shape = jax.ShapeDtypeStruct((n // seq_len, DEPTH, seq_len, D_ATTN), F32)
        out_shape += [kv_shape] * 2
        out_specs += [pl.BlockSpec((n_seq, None, seq_len, D_ATTN), lambda i: (i, layer, 0, 0))] * 2
    return pl.pallas_call(
        functools.partial(_proj_kernel, rope=rope_tabs is not None, emit_f32=emit, seq_len=seq_len),
        grid=(n // tile,),
        in_specs=in_specs,
        out_specs=out_specs,
        out_shape=out_shape,
        input_output_aliases=aliases,
        compiler_params=_cparams(1),
        name="proj",
    )(*args)


def _shift_rows(u, prev_row, next_row, row_in_seq, seq_len):
    t = u.shape[0]
    ridx = lax.broadcasted_iota(jnp.int32, (t, 1), 0)
    up = jnp.where(ridx == 0, prev_row, pltpu.roll(u, 1, axis=0))
    dn = jnp.where(ridx == t - 1, next_row, pltpu.roll(u, t - 1, axis=0))
    up = jnp.where(row_in_seq == 0, 0.0, up)
    dn = jnp.where(row_in_seq == seq_len - 1, 0.0, dn)
    return up, dn


def _attn_kernel(*refs, has_cache, tq, seq_len, layer, lam_init):
    it = iter(refs)
    x_ref, m_ref, q_ref, k_ref, v_ref = (next(it) for _ in range(5))
    if has_cache:
        kc_ref, vc_ref = next(it), next(it)
    u_ref, up_ref, un_ref, bg_ref = (next(it) for _ in range(4))
    lam_refs = [next(it) for _ in range(4)]
    sg_ref, cw_ref, cg_ref, wo_ref = (next(it) for _ in range(4))
    o_ref = next(it)

    lq1, lk1, lq2, lk2 = (r[layer:layer + 1, :] for r in lam_refs)
    lam = (jnp.exp(jnp.sum(lq1 * lk1, axis=-1, keepdims=True))
           - jnp.exp(jnp.sum(lq2 * lk2, axis=-1, keepdims=True)) + lam_init)

    lane = lax.broadcasted_iota(jnp.int32, (1, LANES), 1)
    sub0 = jnp.where(lane < HEAD_DIM, 1.0, 0.0).astype(BF16)
    sub1 = jnp.where(lane < HEAD_DIM, 0.0, 1.0).astype(BF16)
    nt = (((1,), (1,)), ((), ()))
    parts = []
    for h in range(N_HEADS):
        sl = slice(h * LANES, (h + 1) * LANES)
        qh = q_ref[:, sl]
        q2 = jnp.concatenate([qh * sub0, qh * sub1], axis=0)
        s = lax.dot_general(q2, k_ref[:, sl], nt, preferred_element_type=F32)
        mx = jnp.max(s, axis=-1, keepdims=True)
        if has_cache:
            sc = lax.dot_general(q2, kc_ref[:, sl].astype(BF16), nt, preferred_element_type=F32)
            mx = jnp.maximum(mx, jnp.max(sc, axis=-1, keepdims=True))
        e = jnp.exp(s - mx)
        den = jnp.sum(e, axis=-1, keepdims=True)
        if has_cache:
            ec = jnp.exp(sc - mx)
            den = den + jnp.sum(ec, axis=-1, keepdims=True)
        inv = 1.0 / den
        c0 = inv[:tq]
        c1 = inv[tq:] * lam
        a = (e[:tq] * c0 - e[tq:] * c1).astype(BF16)
        oh = jnp.dot(a, v_ref[:, sl], preferred_element_type=F32)
        if has_cache:
            ac = (ec[:tq] * c0 - ec[tq:] * c1).astype(BF16)
            oh = oh + jnp.dot(ac, vc_ref[:, sl].astype(BF16), preferred_element_type=F32)
        parts.append(((oh * _rms_rows(oh)) * sg_ref[...]) * (1.0 - lam_init))

    j = pl.program_id(1)
    row_in_seq = (j * tq + lax.broadcasted_iota(jnp.int32, (tq, 1), 0)) % seq_len
    u = u_ref[...].astype(F32)
    up, dn = _shift_rows(u, up_ref[BF16_ROWS - 1:BF16_ROWS, :].astype(F32),
                         un_ref[0:1, :].astype(F32), row_in_seq, seq_len)
    cw = cw_ref[...]
    t = bg_ref[...].astype(F32) * (up * cw[0:1] + u * cw[1:2] + dn * cw[2:3])
    parts.append((t * _rms_rows(t)) * cg_ref[...])

    cat = jnp.concatenate(parts, axis=1).astype(BF16)
    m = m_ref[...]
    g1 = m[:, 2 * D_MODEL:3 * D_MODEL]
    o_ref[...] = x_ref[...] + g1 * jnp.dot(cat, wo_ref[...].astype(BF16),
                                           preferred_element_type=F32)


def _attn(x, mod, layer, row0, mod_span, batch, seq_len, tq, q, k, v, cache, u, bg, lam_ps, subln_g,
          conv_w, conv_g, w_out, lam_init):
    qt = seq_len // tq
    hb = tq // BF16_ROWS
    n_hblocks = x.shape[0] // BF16_ROWS
    tokmap = lambda b, j: (b * qt + j, 0)
    in_specs = [
        pl.BlockSpec((tq, D_MODEL), tokmap),
        _mod_spec(layer, lambda b, j: row0 + (b * seq_len) // mod_span),
        pl.BlockSpec((tq, D_ATTN), tokmap),
        pl.BlockSpec((seq_len, D_ATTN), lambda b, j: (b, 0)),
        pl.BlockSpec((seq_len, D_ATTN), lambda b, j: (b, 0)),
    ]
    args = [x, mod, q, k, v]
    if cache is not None:
        for c in cache:
            p = c.shape[2]
            in_specs.append(pl.BlockSpec((None, None, p, D_ATTN), lambda b, j: (b, layer, 0, 0)))
            args.append(c)
    in_specs += [
        pl.BlockSpec((tq, D_CONV), tokmap),
        pl.BlockSpec((BF16_ROWS, D_CONV), lambda b, j: (jnp.maximum((b * qt + j) * hb - 1, 0), 0)),
        pl.BlockSpec((BF16_ROWS, D_CONV),
                     lambda b, j: (jnp.minimum((b * qt + j + 1) * hb, n_hblocks - 1), 0)),
        pl.BlockSpec((tq, D_CONV), tokmap),
    ]
    in_specs += [_const_spec((DEPTH, HEAD_DIM))] * 4
    in_specs += [
        _layer_spec((1, V_DIM), layer),
        _layer_spec((3, D_CONV), layer),
        _layer_spec((1, D_CONV), layer),
        _layer_spec((D_MODEL, D_MODEL), layer),
    ]
    args += [u, u, u, bg, *lam_ps, subln_g, conv_w, conv_g, w_out]
    return pl.pallas_call(
        functools.partial(_attn_kernel, has_cache=cache is not None, tq=tq, seq_len=seq_len,
                          layer=layer, lam_init=lam_init),
        grid=(batch, qt),
        in_specs=in_specs,
        out_specs=pl.BlockSpec((tq, D_MODEL), tokmap),
        out_shape=jax.ShapeDtypeStruct(x.shape, F32),
        compiler_params=_cparams(2),
        name="attn",
    )(*args)


def _ffn_kernel(*refs, tile, seq_len, halo):
    it = iter(refs)
    x_ref = next(it)
    if halo:
        xp_ref, xn_ref = next(it), next(it)
    m_ref, g2_ref, wup_ref, cw_ref, wdn_ref, o_ref, hs_ref, u_ref, g_ref = (
        next(it) for _ in range(9))

    m = m_ref[...]
    sh2 = m[:, 3 * D_MODEL:4 * D_MODEL]
    sc2 = m[:, 4 * D_MODEL:5 * D_MODEL]
    gate = m[:, 5 * D_MODEL:6 * D_MODEL]

    def pre(xv):
        return ((xv * _rms_rows(xv)) * g2_ref[...]) * (1.0 + sc2) + sh2

    x = x_ref[...]
    i = pl.program_id(0)
    base = HALO if halo else 0
    hs_ref[base:base + tile, :] = pre(x).astype(BF16)
    if halo:
        tiles_per_seq = seq_len // tile
        jt = i % tiles_per_seq
        hs_ref[0:HALO, :] = jnp.where(jt == 0, 0.0, pre(xp_ref[...])).astype(BF16)
        hs_ref[HALO + tile:, :] = jnp.where(jt == tiles_per_seq - 1, 0.0,
                                            pre(xn_ref[...])).astype(BF16)
    if halo:
        segs = [(HALO, tile, 0)]
    else:
        segs = [(_U_PAD + s * (seq_len + _U_PAD), seq_len, s * seq_len)
                for s in range(tile // seq_len)]

        @pl.when(i == 0)
        def _():
            u_ref[...] = jnp.zeros_like(u_ref)

    hs = hs_ref[...]
    for c in range(N_FF_CHUNKS):
        slot = c % 2
        cols = (c * FF_CHUNK, D_FF + c * FF_CHUNK)
        for half, c0 in enumerate(cols):
            uu = jnp.dot(hs, wup_ref[:, c0:c0 + FF_CHUNK].astype(BF16),
                         preferred_element_type=F32)
            lanes = slice(half * FF_CHUNK, (half + 1) * FF_CHUNK)
            if halo:
                u_ref[slot, :, lanes] = uu
            else:
                for b0, ln, r0 in segs:
                    u_ref[slot, b0:b0 + ln, lanes] = uu[r0:r0 + ln]
        for b0, ln, r0 in segs:
            conv = []
            for half, c0 in enumerate(cols):
                lanes = slice(half * FF_CHUNK, (half + 1) * FF_CHUNK)
                w3 = cw_ref[:, c0:c0 + FF_CHUNK]
                conv.append(u_ref[slot, b0 - 1:b0 - 1 + ln, lanes] * w3[0:1]
                            + u_ref[slot, b0:b0 + ln, lanes] * w3[1:2]
                            + u_ref[slot, b0 + 1:b0 + 1 + ln, lanes] * w3[2:3])
            ca, cb = conv
            g_ref[r0:r0 + ln, c * FF_CHUNK:(c + 1) * FF_CHUNK] = (
                (ca * jax.nn.sigmoid(ca)) * cb).astype(BF16)

    gs = g_ref[...]
    for n0 in range(0, D_MODEL, FF_CHUNK):
        d = jnp.dot(gs, wdn_ref[:, n0:n0 + FF_CHUNK].astype(BF16), preferred_element_type=F32)
        o_ref[:, n0:n0 + FF_CHUNK] = x[:, n0:n0 + FF_CHUNK] + gate[:, n0:n0 + FF_CHUNK] * d


def _ffn(x, mod, layer, row0, mod_span, seq_len, halo, norm2_g, w_up, conv_w, w_down):
    n = x.shape[0]
    tile = TOKEN_TILE
    hb = tile // BF16_ROWS
    n_hblocks = n // BF16_ROWS
    in_specs = [pl.BlockSpec((tile, D_MODEL), lambda i: (i, 0))]
    args = [x]
    if halo:
        in_specs += [
            pl.BlockSpec((HALO, D_MODEL), lambda i: (jnp.maximum(i * hb - 1, 0), 0)),
            pl.BlockSpec((HALO, D_MODEL), lambda i: (jnp.minimum((i + 1) * hb, n_hblocks - 1), 0)),
        ]
        args += [x, x]
    in_specs += [
        _mod_spec(layer, lambda i: row0 + (i * tile) // mod_span),
        _layer_spec((1, D_MODEL), layer),
        _layer_spec((D_MODEL, 2 * D_FF), layer),
        _layer_spec((3, 2 * D_FF), layer),
        _layer_spec((D_FF, D_MODEL), layer),
    ]
    args += [mod, norm2_g, w_up, conv_w, w_down]
    if halo:
        rows_h = rows_u = tile + 2 * HALO
    else:
        rows_h = tile
        rows_u = _U_PAD + (tile // seq_len) * (seq_len + _U_PAD)
    return pl.pallas_call(
        functools.partial(_ffn_kernel, tile=tile, seq_len=seq_len, halo=halo),
        grid=(n // tile,),
        in_specs=in_specs,
        out_specs=pl.BlockSpec((tile, D_MODEL), lambda i: (i, 0)),
        out_shape=jax.ShapeDtypeStruct(x.shape, F32),
        scratch_shapes=[pltpu.VMEM((rows_h, D_MODEL), BF16),
                        pltpu.VMEM((2, rows_u, 2 * FF_CHUNK), F32),
                        pltpu.VMEM((tile, D_FF), BF16)],
        compiler_params=_cparams(1),
        name="ffn",
    )(*args)


def _rope_tables(n_tok):
    rows = n_tok // GRID_W
    row = jnp.repeat(jnp.arange(rows, dtype=F32), GRID_W)
    col = jnp.tile(jnp.arange(GRID_W, dtype=F32), rows)
    n_freq = HEAD_DIM // 4
    inv = ROPE_BASE ** (-jnp.arange(n_freq, dtype=F32) / n_freq)
    ar, ac = row[:, None] * inv[None], col[:, None] * inv[None]
    cr, sr, cc, sc = jnp.cos(ar), jnp.sin(ar), jnp.cos(ac), jnp.sin(ac)
    z = jnp.zeros_like(sr)
    rep = lambda parts: jnp.tile(jnp.concatenate(parts, axis=-1), (1, LANES // HEAD_DIM))
    return rep([cr, cr, cc, cc]), rep([-sr, z, -sc, z]), rep([z, sr, z, sc])


def kernel(x_prompt, x_sample, cache_k, cache_v, c, c_ctx, w_mod, b_mod, norm1_g, w_in, q_norm_g, k_norm_g, lambda_q1, lambda_k1, lambda_q2, lambda_k2, subln_g, conv_w, conv_norm_g, w_out, norm2_g, w_up, ffn_conv_w, w_down):
    batch, seq, _ = x_prompt.shape
    dec_batch, dec_seq, _ = x_sample.shape
    past = cache_k.shape[2]

    cvecs = jnp.zeros((MOD_ROWS, D_MODEL), F32).at[0].set(c_ctx).at[1:1 + dec_batch].set(c)
    mod = _modulation(cvecs, w_mod, b_mod).reshape(DEPTH, MOD_ROWS, 1, N_MOD * D_MODEL)

    grp = jnp.arange(D_ATTN) // HEAD_DIM
    gmat = jnp.where(grp[:, None] == grp[None, :], 1.0 / HEAD_DIM, 0.0).astype(BF16)
    row3 = lambda g: g.reshape(DEPTH, 1, -1)
    qg = jnp.tile(row3(q_norm_g), (1, 1, D_ATTN // HEAD_DIM))
    kg = jnp.tile(row3(k_norm_g), (1, 1, D_ATTN // HEAD_DIM))
    n1, n2, sg, cng = row3(norm1_g), row3(norm2_g), row3(subln_g), row3(conv_norm_g)
    lam_ps = (lambda_q1, lambda_k1, lambda_q2, lambda_k2)
    rope_tabs = _rope_tables(dec_seq)
    cache = (cache_k.reshape(dec_batch, DEPTH, past, D_ATTN),
             cache_v.reshape(dec_batch, DEPTH, past, D_ATTN))

    def run(x, row0, mod_span, n_batch, seq_len, tq, ffn_halo, rope, cache, emit):
        kv = "new" if emit else None
        for l in range(DEPTH):
            lam_init = 0.8 - 0.6 * math.exp(-0.3 * l)
            outs = _proj(x, mod, l, row0, mod_span, seq_len, n1, w_in, qg, kg, gmat, rope, kv)
            q, k, v, u, bg = outs[:5]
            if emit:
                kv = tuple(outs[5:])
            x = _attn(x, mod, l, row0, mod_span, n_batch, seq_len, tq, q, k, v, cache, u, bg, lam_ps,
                      sg, conv_w, cng, w_out, lam_init)
            x = _ffn(x, mod, l, row0, mod_span, seq_len, ffn_halo, n2, w_up, ffn_conv_w, w_down)
        return x, kv

    xp, (new_k, new_v) = run(x_prompt.reshape(batch * seq, D_MODEL), 0, batch * seq, batch, seq,
                             seq, False, None, None, True)
    xs, _ = run(x_sample.reshape(dec_batch * dec_seq, D_MODEL), 1, dec_seq, dec_batch, dec_seq,
                Q_TILE, True, rope_tabs, cache, False)

    return (xp.reshape(batch, seq, D_MODEL), xs.reshape(dec_batch, dec_seq, D_MODEL),
            new_k.reshape(batch, DEPTH, seq, N_HEADS, 2, HEAD_DIM),
            new_v.reshape(batch, DEPTH, seq, N_HEADS, V_DIM))
```

```python
import functools
import math

import jax
import jax.numpy as jnp
from jax import lax
from jax.experimental import pallas as pl
from jax.experimental.pallas import tpu as pltpu

D_MODEL = 1024
DEPTH = 2
GRID_W = 64
D_ATTN = 512
D_CONV = 512
N_HEADS = 4
HEAD_DIM = 64
V_DIM = 128
D_FF = 2816
ROPE_BASE = 10000.0
EPS = 1e-6
N_MOD = 6
D_IN = 3 * D_ATTN + 3 * D_CONV

F32 = jnp.float32
BF16 = jnp.bfloat16

LANES = 128
BF16_ROWS = 16
F32_ROWS = 8
MOD_ROWS = F32_ROWS
_U_PAD = F32_ROWS
VMEM_LIMIT = 60 * 1024 * 1024

FF_CHUNK = 256
N_FF_CHUNKS = D_FF // FF_CHUNK
HALO = BF16_ROWS
TOKEN_TILE = 512
Q_TILE = 256


def _cparams(n_axes):
    return pltpu.CompilerParams(
        dimension_semantics=("arbitrary",) * n_axes, vmem_limit_bytes=VMEM_LIMIT)


def _layer_spec(shape, layer):
    nd = len(shape)
    return pl.BlockSpec((None,) + tuple(shape), lambda *_: (layer,) + (0,) * nd,
                        pipeline_mode=pl.Buffered(1))


def _const_spec(shape):
    nd = len(shape)
    return pl.BlockSpec(shape, lambda *_: (0,) * nd, pipeline_mode=pl.Buffered(1))


def _mod_spec(layer, row_of):
    return pl.BlockSpec((None, None, 1, N_MOD * D_MODEL),
                        lambda *idx: (layer, row_of(*idx), 0, 0))


def _rms_rows(x):
    return lax.rsqrt(jnp.mean(x * x, axis=-1, keepdims=True) + EPS)


def _mod_kernel(cv_ref, w_ref, b_ref, o_ref):
    cv = cv_ref[...]
    s = (cv * jax.nn.sigmoid(cv)).astype(BF16)
    o_ref[...] = jnp.dot(s, w_ref[...].astype(BF16), preferred_element_type=F32) + b_ref[...]


def _modulation(cvecs, w_mod, b_mod):
    tn = 1536
    n_out = N_MOD * D_MODEL
    return pl.pallas_call(
        _mod_kernel,
        grid=(DEPTH, n_out // tn),
        in_specs=[
            pl.BlockSpec((MOD_ROWS, D_MODEL), lambda l, j: (0, 0)),
            pl.BlockSpec((None, D_MODEL, tn), lambda l, j: (l, 0, j)),
            pl.BlockSpec((None, 1, tn), lambda l, j: (l, 0, j)),
        ],
        out_specs=pl.BlockSpec((None, MOD_ROWS, tn), lambda l, j: (l, 0, j)),
        out_shape=jax.ShapeDtypeStruct((DEPTH, MOD_ROWS, n_out), F32),
        compiler_params=_cparams(2),
        name="adaln_mod",
    )(cvecs, w_mod, b_mod.reshape(DEPTH, 1, n_out))


def _group_norm_gain(z, gmat_ref, gain):
    ms = jnp.dot((z * z).astype(BF16), gmat_ref[...], preferred_element_type=F32)
    return z * lax.rsqrt(ms + EPS) * gain


def _rope(z, c_ref, sa_ref, sb_ref):
    c, sa, sb = c_ref[...], sa_ref[...], sb_ref[...]
    cols = []
    for j in range(z.shape[1] // LANES):
        zj = z[:, j * LANES:(j + 1) * LANES]
        hi = pltpu.roll(zj, LANES - HEAD_DIM // 4, axis=1)
        lo = pltpu.roll(zj, HEAD_DIM // 4, axis=1)
        cols.append(zj * c + hi * sa + lo * sb)
    return jnp.concatenate(cols, axis=1)


def _proj_kernel(*refs, rope, emit_f32, seq_len):
    it = iter(refs)
    x_ref, m_ref, g1_ref, w_ref, qg_ref, kg_ref, gmat_ref = (next(it) for _ in range(7))
    if rope:
        c_ref, sa_ref, sb_ref = (next(it) for _ in range(3))
    if emit_f32 == "alias":
        next(it), next(it)
    q_ref, k_ref, v_ref, u_ref, bg_ref = (next(it) for _ in range(5))
    if emit_f32:
        kf_ref, vf_ref = next(it), next(it)

    x = x_ref[...]
    m = m_ref[...]
    sh1 = m[:, 0:D_MODEL]
    sc1 = m[:, D_MODEL:2 * D_MODEL]
    h = ((x * _rms_rows(x)) * g1_ref[...]) * (1.0 + sc1) + sh1
    hb = h.astype(BF16)

    def col(i0, n):
        return jnp.dot(hb, w_ref[:, i0:i0 + n].astype(BF16), preferred_element_type=F32)

    q = _group_norm_gain(col(0, D_ATTN), gmat_ref, qg_ref[...])
    k = _group_norm_gain(col(D_ATTN, D_ATTN), gmat_ref, kg_ref[...])
    if rope:
        q = _rope(q, c_ref, sa_ref, sb_ref)
        k = _rope(k, c_ref, sa_ref, sb_ref)
    v = col(2 * D_ATTN, D_ATTN)
    q_ref[...] = (q * (HEAD_DIM ** -0.5 * math.log2(math.e))).astype(BF16)
    k_ref[...] = k.astype(BF16)
    v_ref[...] = v.T.astype(BF16)
    if emit_f32:
        n_seq = x.shape[0] // seq_len
        kf_ref[...] = k.reshape(n_seq, seq_len, D_ATTN)
        vf_ref[...] = v.reshape(n_seq, seq_len, D_ATTN)
    bg_ref[...] = col(3 * D_ATTN, D_CONV).astype(BF16)
    cg = col(3 * D_ATTN + D_CONV, D_CONV)
    xc = col(3 * D_ATTN + 2 * D_CONV, D_CONV)
    u_ref[...] = (cg * xc).astype(BF16)


def _proj(x, mod, layer, row0, mod_span, seq_len, norm1_g, w_in, qg, kg, gmat, rope_tabs, kv_out):
    n = x.shape[0]
    tile = TOKEN_TILE
    tiles_per_seq = max(seq_len // tile, 1)
    in_specs = [
        pl.BlockSpec((tile, D_MODEL), lambda i: (i, 0)),
        _mod_spec(layer, lambda i: row0 + (i * tile) // mod_span),
        _layer_spec((1, D_MODEL), layer),
        _layer_spec((D_MODEL, D_IN), layer),
        _layer_spec((1, D_ATTN), layer),
        _layer_spec((1, D_ATTN), layer),
        _const_spec((D_ATTN, D_ATTN)),
    ]
    args = [x, mod, norm1_g, w_in, qg, kg, gmat]
    if rope_tabs is not None:
        for t in rope_tabs:
            in_specs.append(pl.BlockSpec((tile, LANES), lambda i: (i % tiles_per_seq, 0)))
            args.append(t)
    aliases = {}
    emit = False
    if kv_out is not None and kv_out != "new":
        emit = "alias"
        for a in kv_out:
            aliases[len(args)] = 5 + len(aliases)
            in_specs.append(pl.BlockSpec(memory_space=pl.ANY))
            args.append(a)
    elif kv_out == "new":
        emit = True
    tok = jax.ShapeDtypeStruct((n, D_ATTN), BF16)
    tok_spec = pl.BlockSpec((tile, D_ATTN), lambda i: (i, 0))
    out_shape = [tok, tok, jax.ShapeDtypeStruct((D_ATTN, n), BF16), tok, tok]
    out_specs = [tok_spec, tok_spec, pl.BlockSpec((D_ATTN, tile), lambda i: (0, i)), tok_spec,
                 tok_spec]
    if emit:
        n_seq = tile // seq_len
        kv_shape = jax.ShapeDtypeStruct((n // seq_len, DEPTH, seq_len, D_ATTN), F32)
        out_shape += [kv_shape] * 2
        out_specs += [pl.BlockSpec((n_seq, None, seq_len, D_ATTN), lambda i: (i, layer, 0, 0))] * 2
    return pl.pallas_call(
        functools.partial(_proj_kernel, rope=rope_tabs is not None, emit_f32=emit, seq_len=seq_len),
        grid=(n // tile,),
        in_specs=in_specs,
        out_specs=out_specs,
        out_shape=out_shape,
        input_output_aliases=aliases,
        compiler_params=_cparams(1),
        name="proj",
    )(*args)


def _shift_rows(u, prev_row, next_row, row_in_seq, seq_len):
    t = u.shape[0]
    ridx = lax.broadcasted_iota(jnp.int32, (t, 1), 0)
    up = jnp.where(ridx == 0, prev_row, pltpu.roll(u, 1, axis=0))
    dn = jnp.where(ridx == t - 1, next_row, pltpu.roll(u, t - 1, axis=0))
    up = jnp.where(row_in_seq == 0, 0.0, up)
    dn = jnp.where(row_in_seq == seq_len - 1, 0.0, dn)
    return up, dn


def _attn_kernel(*refs, has_cache, tq, seq_len, layer, lam_init):
    it = iter(refs)
    x_ref, m_ref, q_ref, k_ref, vt_ref = (next(it) for _ in range(5))
    if has_cache:
        kc_ref, vc_ref = next(it), next(it)
    u_ref, up_ref, un_ref, bg_ref = (next(it) for _ in range(4))
    lam_refs = [next(it) for _ in range(4)]
    sg_ref, cw_ref, cg_ref, wo_ref = (next(it) for _ in range(4))
    o_ref = next(it)

    lq1, lk1, lq2, lk2 = (r[layer:layer + 1, :] for r in lam_refs)
    lam = (jnp.exp(jnp.sum(lq1 * lk1, axis=-1, keepdims=True))
           - jnp.exp(jnp.sum(lq2 * lk2, axis=-1, keepdims=True)) + lam_init)

    lane = lax.broadcasted_iota(jnp.int32, (1, LANES), 1)
    sub0 = jnp.where(lane < HEAD_DIM, 1.0, 0.0).astype(BF16)
    sub1 = jnp.where(lane < HEAD_DIM, 0.0, 1.0).astype(BF16)
    nt = (((1,), (1,)), ((), ()))
    nk = k_ref.shape[0]
    ones = jnp.ones((BF16_ROWS, nk), BF16)
    sgain = jnp.concatenate([sg_ref[...]] * (tq // LANES), axis=1)
    if has_cache:
        nc = kc_ref.shape[0]
        ones_c = jnp.ones((BF16_ROWS, nc), BF16)
        vct = vc_ref[...].T.astype(BF16)
    def scores(h):
        sl = slice(h * LANES, (h + 1) * LANES)
        qh = q_ref[:, sl]
        q2 = jnp.concatenate([qh * sub0, qh * sub1], axis=0)
        st = lax.dot_general(k_ref[:, sl], q2, nt, preferred_element_type=F32)
        if not has_cache:
            return st, None
        return st, lax.dot_general(kc_ref[:, sl].astype(BF16), q2, nt,
                                   preferred_element_type=F32)

    heads = []
    nxt = scores(0)

    j = pl.program_id(1)
    row_in_seq = (j * tq + lax.broadcasted_iota(jnp.int32, (tq, 1), 0)) % seq_len
    u = u_ref[...].astype(F32)
    up, dn = _shift_rows(u, up_ref[BF16_ROWS - 1:BF16_ROWS, :].astype(F32),
                         un_ref[0:1, :].astype(F32), row_in_seq, seq_len)
    cw = cw_ref[...]
    t = bg_ref[...].astype(F32) * (up * cw[0:1] + u * cw[1:2] + dn * cw[2:3])
    y = ((t * _rms_rows(t)) * cg_ref[...]).astype(BF16)
    y_proj = jnp.dot(y, wo_ref[D_ATTN:, :].astype(BF16), preferred_element_type=F32)

    for h in range(N_HEADS):
        sl = slice(h * LANES, (h + 1) * LANES)
        st, sct = nxt
        if h + 1 < N_HEADS:
            nxt = scores(h + 1)
        mx = jnp.max(st, axis=0, keepdims=True)
        if has_cache:
            mx = jnp.maximum(mx, jnp.max(sct, axis=0, keepdims=True))
        e = jnp.exp2(st - mx).astype(BF16)
        oe = jnp.dot(jnp.concatenate([vt_ref[sl, :], ones], axis=0), e,
                     preferred_element_type=F32)
        if has_cache:
            ec = jnp.exp2(sct - mx).astype(BF16)
            oe = oe + jnp.dot(jnp.concatenate([vct[sl, :], ones_c], axis=0), ec,
                              preferred_element_type=F32)
        den = oe[V_DIM:V_DIM + 1, :]
        c0 = 1.0 / den[:, :tq]
        c1 = lam / den[:, tq:]
        oh = oe[:V_DIM, :tq] * c0 - oe[:V_DIM, tq:] * c1
        r = lax.rsqrt(jnp.mean(oh * oh, axis=0, keepdims=True) + EPS)
        heads.append(((oh * r) * sgain) * (1.0 - lam_init))
    o = jnp.concatenate(heads, axis=0).T.astype(BF16)
    mix = y_proj + jnp.dot(o, wo_ref[0:D_ATTN, :].astype(BF16), preferred_element_type=F32)
    g1 = m_ref[...][:, 2 * D_MODEL:3 * D_MODEL]
    o_ref[...] = x_ref[...] + g1 * mix


def _attn(x, mod, layer, row0, mod_span, batch, seq_len, tq, q, k, vt, cache, u, bg, lam_ps, subln_g,
          conv_w, conv_g, w_out, lam_init):
    qt = seq_len // tq
    hb = tq // BF16_ROWS
    n_hblocks = x.shape[0] // BF16_ROWS
    tokmap = lambda b, j: (b * qt + j, 0)
    in_specs = [
        pl.BlockSpec((tq, D_MODEL), tokmap),
        _mod_spec(layer, lambda b, j: row0 + (b * seq_len) // mod_span),
        pl.BlockSpec((tq, D_ATTN), tokmap),
        pl.BlockSpec((seq_len, D_ATTN), lambda b, j: (b, 0)),
        pl.BlockSpec((D_ATTN, seq_len), lambda b, j: (0, b)),
    ]
    args = [x, mod, q, k, vt]
    if cache is not None:
        for c in cache:
            p = c.shape[2]
            in_specs.append(pl.BlockSpec((None, None, p, D_ATTN), lambda b, j: (b, layer, 0, 0)))
            args.append(c)
    in_specs += [
        pl.BlockSpec((tq, D_CONV), tokmap),
        pl.BlockSpec((BF16_ROWS, D_CONV), lambda b, j: (jnp.maximum((b * qt + j) * hb - 1, 0), 0)),
        pl.BlockSpec((BF16_ROWS, D_CONV),
                     lambda b, j: (jnp.minimum((b * qt + j + 1) * hb, n_hblocks - 1), 0)),
        pl.BlockSpec((tq, D_CONV), tokmap),
    ]
    in_specs += [_const_spec((DEPTH, HEAD_DIM))] * 4
    in_specs += [
        _layer_spec((V_DIM, LANES), layer),
        _layer_spec((3, D_CONV), layer),
        _layer_spec((1, D_CONV), layer),
        _layer_spec((D_MODEL, D_MODEL), layer),
    ]
    args += [u, u, u, bg, *lam_ps, subln_g, conv_w, conv_g, w_out]
    return pl.pallas_call(
        functools.partial(_attn_kernel, has_cache=cache is not None, tq=tq, seq_len=seq_len,
                          layer=layer, lam_init=lam_init),
        grid=(batch, qt),
        in_specs=in_specs,
        out_specs=pl.BlockSpec((tq, D_MODEL), tokmap),
        out_shape=jax.ShapeDtypeStruct(x.shape, F32),
        compiler_params=_cparams(2),
        name="attn",
    )(*args)


def _ffn_kernel(*refs, tile, seq_len, halo):
    it = iter(refs)
    x_ref = next(it)
    if halo:
        xp_ref, xn_ref = next(it), next(it)
    m_ref, g2_ref, wup_ref, cw_ref, wdn_ref, o_ref, hs_ref, u_ref, g_ref = (
        next(it) for _ in range(9))

    m = m_ref[...]
    sh2 = m[:, 3 * D_MODEL:4 * D_MODEL]
    sc2 = m[:, 4 * D_MODEL:5 * D_MODEL]
    gate = m[:, 5 * D_MODEL:6 * D_MODEL]

    def pre(xv):
        return ((xv * _rms_rows(xv)) * g2_ref[...]) * (1.0 + sc2) + sh2

    x = x_ref[...]
    i = pl.program_id(0)
    base = HALO if halo else 0
    hs_ref[base:base + tile, :] = pre(x).astype(BF16)
    if halo:
        tiles_per_seq = seq_len // tile
        jt = i % tiles_per_seq
        hs_ref[0:HALO, :] = jnp.where(jt == 0, 0.0, pre(xp_ref[...])).astype(BF16)
        hs_ref[HALO + tile:, :] = jnp.where(jt == tiles_per_seq - 1, 0.0,
                                            pre(xn_ref[...])).astype(BF16)
    if halo:
        segs = [(HALO, tile, 0)]
    else:
        segs = [(_U_PAD + s * (seq_len + _U_PAD), seq_len, s * seq_len)
                for s in range(tile // seq_len)]

        @pl.when(i == 0)
        def _():
            u_ref[...] = jnp.zeros_like(u_ref)

    hs = hs_ref[...]
    for c in range(N_FF_CHUNKS):
        slot = c % 2
        cols = (c * FF_CHUNK, D_FF + c * FF_CHUNK)
        for half, c0 in enumerate(cols):
            uu = jnp.dot(hs, wup_ref[:, c0:c0 + FF_CHUNK].astype(BF16),
                         preferred_element_type=F32)
            lanes = slice(half * FF_CHUNK, (half + 1) * FF_CHUNK)
            if halo:
                u_ref[slot, :, lanes] = uu
            else:
                for b0, ln, r0 in segs:
                    u_ref[slot, b0:b0 + ln, lanes] = uu[r0:r0 + ln]
        for b0, ln, r0 in segs:
            conv = []
            for half, c0 in enumerate(cols):
                lanes = slice(half * FF_CHUNK, (half + 1) * FF_CHUNK)
                w3 = cw_ref[:, c0:c0 + FF_CHUNK]
                conv.append(u_ref[slot, b0 - 1:b0 - 1 + ln, lanes] * w3[0:1]
                            + u_ref[slot, b0:b0 + ln, lanes] * w3[1:2]
                            + u_ref[slot, b0 + 1:b0 + 1 + ln, lanes] * w3[2:3])
            ca, cb = conv
            g_ref[r0:r0 + ln, c * FF_CHUNK:(c + 1) * FF_CHUNK] = (
                (ca * jax.nn.sigmoid(ca)) * cb).astype(BF16)

    gs = g_ref[...]
    for n0 in range(0, D_MODEL, FF_CHUNK):
        d = jnp.dot(gs, wdn_ref[:, n0:n0 + FF_CHUNK].astype(BF16), preferred_element_type=F32)
        o_ref[:, n0:n0 + FF_CHUNK] = x[:, n0:n0 + FF_CHUNK] + gate[:, n0:n0 + FF_CHUNK] * d


def _ffn(x, mod, layer, row0, mod_span, seq_len, halo, norm2_g, w_up, conv_w, w_down):
    n = x.shape[0]
    tile = TOKEN_TILE
    hb = tile // BF16_ROWS
    n_hblocks = n // BF16_ROWS
    in_specs = [pl.BlockSpec((tile, D_MODEL), lambda i: (i, 0))]
    args = [x]
    if halo:
        in_specs += [
            pl.BlockSpec((HALO, D_MODEL), lambda i: (jnp.maximum(i * hb - 1, 0), 0)),
            pl.BlockSpec((HALO, D_MODEL), lambda i: (jnp.minimum((i + 1) * hb, n_hblocks - 1), 0)),
        ]
        args += [x, x]
    in_specs += [
        _mod_spec(layer, lambda i: row0 + (i * tile) // mod_span),
        _layer_spec((1, D_MODEL), layer),
        _layer_spec((D_MODEL, 2 * D_FF), layer),
        _layer_spec((3, 2 * D_FF), layer),
        _layer_spec((D_FF, D_MODEL), layer),
    ]
    args += [mod, norm2_g, w_up, conv_w, w_down]
    if halo:
        rows_h = rows_u = tile + 2 * HALO
    else:
        rows_h = tile
        rows_u = _U_PAD + (tile // seq_len) * (seq_len + _U_PAD)
    return pl.pallas_call(
        functools.partial(_ffn_kernel, tile=tile, seq_len=seq_len, halo=halo),
        grid=(n // tile,),
        in_specs=in_specs,
        out_specs=pl.BlockSpec((tile, D_MODEL), lambda i: (i, 0)),
        out_shape=jax.ShapeDtypeStruct(x.shape, F32),
        scratch_shapes=[pltpu.VMEM((rows_h, D_MODEL), BF16),
                        pltpu.VMEM((2, rows_u, 2 * FF_CHUNK), F32),
                        pltpu.VMEM((tile, D_FF), BF16)],
        compiler_params=_cparams(1),
        name="ffn",
    )(*args)


def _rope_tables(n_tok):
    rows = n_tok // GRID_W
    row = jnp.repeat(jnp.arange(rows, dtype=F32), GRID_W)
    col = jnp.tile(jnp.arange(GRID_W, dtype=F32), rows)
    n_freq = HEAD_DIM // 4
    inv = ROPE_BASE ** (-jnp.arange(n_freq, dtype=F32) / n_freq)
    ar, ac = row[:, None] * inv[None], col[:, None] * inv[None]
    cr, sr, cc, sc = jnp.cos(ar), jnp.sin(ar), jnp.cos(ac), jnp.sin(ac)
    z = jnp.zeros_like(sr)
    rep = lambda parts: jnp.tile(jnp.concatenate(parts, axis=-1), (1, LANES // HEAD_DIM))
    return rep([cr, cr, cc, cc]), rep([-sr, z, -sc, z]), rep([z, sr, z, sc])


def kernel(x_prompt, x_sample, cache_k, cache_v, c, c_ctx, w_mod, b_mod, norm1_g, w_in, q_norm_g, k_norm_g, lambda_q1, lambda_k1, lambda_q2, lambda_k2, subln_g, conv_w, conv_norm_g, w_out, norm2_g, w_up, ffn_conv_w, w_down):
    batch, seq, _ = x_prompt.shape
    dec_batch, dec_seq, _ = x_sample.shape
    past = cache_k.shape[2]

    cvecs = jnp.zeros((MOD_ROWS, D_MODEL), F32).at[0].set(c_ctx).at[1:1 + dec_batch].set(c)
    mod = _modulation(cvecs, w_mod, b_mod).reshape(DEPTH, MOD_ROWS, 1, N_MOD * D_MODEL)

    grp = jnp.arange(D_ATTN) // HEAD_DIM
    gmat = jnp.where(grp[:, None] == grp[None, :], 1.0 / HEAD_DIM, 0.0).astype(BF16)
    row3 = lambda g: g.reshape(DEPTH, 1, -1)
    qg = jnp.tile(row3(q_norm_g), (1, 1, D_ATTN // HEAD_DIM))
    kg = jnp.tile(row3(k_norm_g), (1, 1, D_ATTN // HEAD_DIM))
    n1, n2, cng = row3(norm1_g), row3(norm2_g), row3(conv_norm_g)
    sg = jnp.broadcast_to(subln_g[:, :, None], (DEPTH, V_DIM, LANES))
    lam_ps = (lambda_q1, lambda_k1, lambda_q2, lambda_k2)
    rope_tabs = _rope_tables(dec_seq)
    cache = (cache_k.reshape(dec_batch, DEPTH, past, D_ATTN),
             cache_v.reshape(dec_batch, DEPTH, past, D_ATTN))

    def run(x, row0, mod_span, n_batch, seq_len, tq, ffn_halo, rope, cache, emit):
        kv = "new" if emit else None
        for l in range(DEPTH):
            lam_init = 0.8 - 0.6 * math.exp(-0.3 * l)
            outs = _proj(x, mod, l, row0, mod_span, seq_len, n1, w_in, qg, kg, gmat, rope, kv)
            q, k, v, u, bg = outs[:5]
            if emit:
                kv = tuple(outs[5:])
            x = _attn(x, mod, l, row0, mod_span, n_batch, seq_len, tq, q, k, v, cache, u, bg, lam_ps,
                      sg, conv_w, cng, w_out, lam_init)
            x = _ffn(x, mod, l, row0, mod_span, seq_len, ffn_halo, n2, w_up, ffn_conv_w, w_down)
        return x, kv

    xp, (new_k, new_v) = run(x_prompt.reshape(batch * seq, D_MODEL), 0, batch * seq, batch, seq,
                             seq, False, None, None, True)
    xs, _ = run(x_sample.reshape(dec_batch * dec_seq, D_MODEL), 1, dec_seq, dec_batch, dec_seq,
                Q_TILE, True, rope_tabs, cache, False)

    return (xp.reshape(batch, seq, D_MODEL), xs.reshape(dec_batch, dec_seq, D_MODEL),
            new_k.reshape(batch, DEPTH, seq, N_HEADS, 2, HEAD_DIM),
            new_v.reshape(batch, DEPTH, seq, N_HEADS, V_DIM))
```

```python
import functools
import math

import jax
import jax.numpy as jnp
from jax import lax
from jax.experimental import pallas as pl
from jax.experimental.pallas import tpu as pltpu

D_MODEL = 1024
DEPTH = 2
GRID_W = 64
D_ATTN = 512
D_CONV = 512
N_HEADS = 4
HEAD_DIM = 64
V_DIM = 128
D_FF = 2816
ROPE_BASE = 10000.0
EPS = 1e-6
N_MOD = 6
D_IN = 3 * D_ATTN + 3 * D_CONV

F32 = jnp.float32
BF16 = jnp.bfloat16

LANES = 128
BF16_ROWS = 16
F32_ROWS = 8
MOD_ROWS = F32_ROWS
_U_PAD = F32_ROWS
VMEM_LIMIT = 60 * 1024 * 1024

FF_CHUNK = 256
N_FF_CHUNKS = D_FF // FF_CHUNK
HALO = BF16_ROWS
TOKEN_TILE = 512
Q_TILE = 512
CTX_Q_TILE = 1024


def _cparams(n_axes):
    return pltpu.CompilerParams(
        dimension_semantics=("arbitrary",) * n_axes, vmem_limit_bytes=VMEM_LIMIT)


def _layer_spec(shape, layer):
    nd = len(shape)
    return pl.BlockSpec((None,) + tuple(shape), lambda *_: (layer,) + (0,) * nd,
                        pipeline_mode=pl.Buffered(1))


def _const_spec(shape):
    nd = len(shape)
    return pl.BlockSpec(shape, lambda *_: (0,) * nd, pipeline_mode=pl.Buffered(1))


def _mod_spec(layer, row_of):
    return pl.BlockSpec((None, None, 1, N_MOD * D_MODEL),
                        lambda *idx: (layer, row_of(*idx), 0, 0))


def _rms_rows(x):
    return lax.rsqrt(jnp.mean(x * x, axis=-1, keepdims=True) + EPS)


def _mod_kernel(cv_ref, w_ref, b_ref, o_ref):
    cv = cv_ref[...]
    s = (cv * jax.nn.sigmoid(cv)).astype(BF16)
    o_ref[...] = jnp.dot(s, w_ref[...].astype(BF16), preferred_element_type=F32) + b_ref[...]


def _modulation(cvecs, w_mod, b_mod):
    tn = 1536
    n_out = N_MOD * D_MODEL
    return pl.pallas_call(
        _mod_kernel,
        grid=(DEPTH, n_out // tn),
        in_specs=[
            pl.BlockSpec((MOD_ROWS, D_MODEL), lambda l, j: (0, 0)),
            pl.BlockSpec((None, D_MODEL, tn), lambda l, j: (l, 0, j)),
            pl.BlockSpec((None, 1, tn), lambda l, j: (l, 0, j)),
        ],
        out_specs=pl.BlockSpec((None, MOD_ROWS, tn), lambda l, j: (l, 0, j)),
        out_shape=jax.ShapeDtypeStruct((DEPTH, MOD_ROWS, n_out), F32),
        compiler_params=_cparams(2),
        name="adaln_mod",
    )(cvecs, w_mod, b_mod.reshape(DEPTH, 1, n_out))


def _group_norm_gain(z, gmat_ref, gain):
    ms = jnp.dot((z * z).astype(BF16), gmat_ref[...], preferred_element_type=F32)
    return z * lax.rsqrt(ms + EPS) * gain


def _rope(z, c_ref, sa_ref, sb_ref):
    c, sa, sb = c_ref[...], sa_ref[...], sb_ref[...]
    cols = []
    for j in range(z.shape[1] // LANES):
        zj = z[:, j * LANES:(j + 1) * LANES]
        hi = pltpu.roll(zj, LANES - HEAD_DIM // 4, axis=1)
        lo = pltpu.roll(zj, HEAD_DIM // 4, axis=1)
        cols.append(zj * c + hi * sa + lo * sb)
    return jnp.concatenate(cols, axis=1)


def _proj_kernel(*refs, rope, emit_f32, seq_len, layer):
    it = iter(refs)
    x_ref, m_ref, g1_ref, w_ref, qg_ref, kg_ref, gmat_ref = (next(it) for _ in range(7))
    if rope:
        c_ref, sa_ref, sb_ref = (next(it) for _ in range(3))
    if emit_f32 == "alias":
        next(it), next(it)
    q_ref, k_ref, v_ref, u_ref, bg_ref = (next(it) for _ in range(5))
    if emit_f32:
        kf_ref, vf_ref = next(it), next(it)

    x = x_ref[...]
    m = m_ref[...]
    sh1 = m[:, 0:D_MODEL]
    sc1 = m[:, D_MODEL:2 * D_MODEL]
    h = ((x * _rms_rows(x)) * g1_ref[...]) * (1.0 + sc1) + sh1
    hb = h.astype(BF16)

    def col(i0, n):
        return jnp.dot(hb, w_ref[:, i0:i0 + n].astype(BF16), preferred_element_type=F32)

    q = _group_norm_gain(col(0, D_ATTN), gmat_ref, qg_ref[...])
    k = _group_norm_gain(col(D_ATTN, D_ATTN), gmat_ref, kg_ref[...])
    if rope:
        q = _rope(q, c_ref, sa_ref, sb_ref)
        k = _rope(k, c_ref, sa_ref, sb_ref)
    v = col(2 * D_ATTN, D_ATTN)
    q_ref[...] = (q * (HEAD_DIM ** -0.5 * math.log2(math.e))).astype(BF16)
    k_ref[...] = k.astype(BF16)
    v_ref[...] = v.T.astype(BF16)
    if emit_f32 == "alias":
        n_seq = x.shape[0] // seq_len
        kf_ref[...] = k.reshape(n_seq, seq_len, D_ATTN)
        vf_ref[...] = v.reshape(n_seq, seq_len, D_ATTN)
    elif emit_f32:
        n_seq = x.shape[0] // seq_len
        zero = jnp.zeros((n_seq, seq_len, D_ATTN), F32)
        for l in range(DEPTH):
            kf_ref[:, l] = k.reshape(n_seq, seq_len, D_ATTN) if l == layer else zero
            vf_ref[:, l] = v.reshape(n_seq, seq_len, D_ATTN) if l == layer else zero
    bg_ref[...] = col(3 * D_ATTN, D_CONV).astype(BF16)
    cg = col(3 * D_ATTN + D_CONV, D_CONV)
    xc = col(3 * D_ATTN + 2 * D_CONV, D_CONV)
    u_ref[...] = (cg * xc).astype(BF16)


def _proj(x, mod, layer, row0, mod_span, seq_len, norm1_g, w_in, qg, kg, gmat, rope_tabs, kv_out):
    n = x.shape[0]
    tile = TOKEN_TILE
    tiles_per_seq = max(seq_len // tile, 1)
    in_specs = [
        pl.BlockSpec((tile, D_MODEL), lambda i: (i, 0)),
        _mod_spec(layer, lambda i: row0 + (i * tile) // mod_span),
        _layer_spec((1, D_MODEL), layer),
        _layer_spec((D_MODEL, D_IN), layer),
        _layer_spec((1, D_ATTN), layer),
        _layer_spec((1, D_ATTN), layer),
        _const_spec((D_ATTN, D_ATTN)),
    ]
    args = [x, mod, norm1_g, w_in, qg, kg, gmat]
    if rope_tabs is not None:
        for t in rope_tabs:
            in_specs.append(pl.BlockSpec((tile, LANES), lambda i: (i % tiles_per_seq, 0)))
            args.append(t)
    aliases = {}
    emit = False
    if kv_out is not None and kv_out != "new":
        emit = "alias"
        for a in kv_out:
            aliases[len(args)] = 5 + len(aliases)
            in_specs.append(pl.BlockSpec(memory_space=pl.ANY))
            args.append(a)
    elif kv_out == "new":
        emit = True
    tok = jax.ShapeDtypeStruct((n, D_ATTN), BF16)
    tok_spec = pl.BlockSpec((tile, D_ATTN), lambda i: (i, 0))
    out_shape = [tok, tok, jax.ShapeDtypeStruct((D_ATTN, n), BF16), tok, tok]
    out_specs = [tok_spec, tok_spec, pl.BlockSpec((D_ATTN, tile), lambda i: (0, i)), tok_spec,
                 tok_spec]
    if emit:
        n_seq = tile // seq_len
        kv_shape = jax.ShapeDtypeStruct((n // seq_len, DEPTH, seq_len, D_ATTN), F32)
        out_shape += [kv_shape] * 2
        if emit == "alias":
            kv_spec = pl.BlockSpec((n_seq, None, seq_len, D_ATTN), lambda i: (i, layer, 0, 0))
        else:
            kv_spec = pl.BlockSpec((n_seq, DEPTH, seq_len, D_ATTN), lambda i: (i, 0, 0, 0))
        out_specs += [kv_spec] * 2
    return pl.pallas_call(
        functools.partial(_proj_kernel, rope=rope_tabs is not None, emit_f32=emit, seq_len=seq_len,
                          layer=layer),
        grid=(n // tile,),
        in_specs=in_specs,
        out_specs=out_specs,
        out_shape=out_shape,
        input_output_aliases=aliases,
        compiler_params=_cparams(1),
        name="proj",
    )(*args)


def _shift_rows(u, prev_row, next_row, row_in_seq, seq_len):
    t = u.shape[0]
    ridx = lax.broadcasted_iota(jnp.int32, (t, 1), 0)
    up = jnp.where(ridx == 0, prev_row, pltpu.roll(u, 1, axis=0))
    dn = jnp.where(ridx == t - 1, next_row, pltpu.roll(u, t - 1, axis=0))
    up = jnp.where(row_in_seq == 0, 0.0, up)
    dn = jnp.where(row_in_seq == seq_len - 1, 0.0, dn)
    return up, dn


def _attn_kernel(*refs, has_cache, tq, seq_len, layer, lam_init):
    it = iter(refs)
    x_ref, m_ref, q_ref, k_ref, vt_ref = (next(it) for _ in range(5))
    if has_cache:
        kc_ref, vc_ref = next(it), next(it)
    u_ref, up_ref, un_ref, bg_ref = (next(it) for _ in range(4))
    lam_refs = [next(it) for _ in range(4)]
    sg_ref, cw_ref, cg_ref, wo_ref = (next(it) for _ in range(4))
    o_ref = next(it)

    lq1, lk1, lq2, lk2 = (r[layer:layer + 1, :] for r in lam_refs)
    lam = (jnp.exp(jnp.sum(lq1 * lk1, axis=-1, keepdims=True))
           - jnp.exp(jnp.sum(lq2 * lk2, axis=-1, keepdims=True)) + lam_init)

    nq = min(tq, seq_len)
    if seq_len >= tq:
        groups = [(0, 0, k_ref.shape[0])]
    else:
        groups = [(g * seq_len, g * seq_len, seq_len) for g in range(tq // seq_len)]

    lane = lax.broadcasted_iota(jnp.int32, (1, LANES), 1)
    sub0 = jnp.where(lane < HEAD_DIM, 1.0, 0.0).astype(BF16)
    sub1 = jnp.where(lane < HEAD_DIM, 0.0, 1.0).astype(BF16)
    nt = (((1,), (1,)), ((), ()))
    ones = jnp.ones((BF16_ROWS, groups[0][2]), BF16)
    sgain = jnp.concatenate([sg_ref[...]] * (nq // LANES), axis=1)
    if has_cache:
        nc = kc_ref.shape[0]
        ones_c = jnp.ones((BF16_ROWS, nc), BF16)
        vct = vc_ref[...].T.astype(BF16)

    def scores(job):
        (q0, k0, nk), h = job
        sl = slice(h * LANES, (h + 1) * LANES)
        qh = q_ref[q0:q0 + nq, sl]
        q2 = jnp.concatenate([qh * sub0, qh * sub1], axis=0)
        st = lax.dot_general(k_ref[k0:k0 + nk, sl], q2, nt, preferred_element_type=F32)
        if not has_cache:
            return st, None
        return st, lax.dot_general(kc_ref[:, sl].astype(BF16), q2, nt,
                                   preferred_element_type=F32)

    jobs = [(grp, h) for grp in groups for h in range(N_HEADS)]
    nxt = scores(jobs[0])

    j = pl.program_id(1)
    row_in_seq = (j * tq + lax.broadcasted_iota(jnp.int32, (tq, 1), 0)) % seq_len
    u = u_ref[...].astype(F32)
    up, dn = _shift_rows(u, up_ref[BF16_ROWS - 1:BF16_ROWS, :].astype(F32),
                         un_ref[0:1, :].astype(F32), row_in_seq, seq_len)
    cw = cw_ref[...]
    t = bg_ref[...].astype(F32) * (up * cw[0:1] + u * cw[1:2] + dn * cw[2:3])
    y = ((t * _rms_rows(t)) * cg_ref[...]).astype(BF16)
    y_proj = jnp.dot(y, wo_ref[D_ATTN:, :].astype(BF16), preferred_element_type=F32)

    outs = []
    for n, ((q0, k0, nk), h) in enumerate(jobs):
        sl = slice(h * LANES, (h + 1) * LANES)
        st, sct = nxt
        if n + 1 < len(jobs):
            nxt = scores(jobs[n + 1])
        mx = jnp.max(st, axis=0, keepdims=True)
        if has_cache:
            mx = jnp.maximum(mx, jnp.max(sct, axis=0, keepdims=True))
        e = jnp.exp2(st - mx).astype(BF16)
        oe = jnp.dot(jnp.concatenate([vt_ref[sl, k0:k0 + nk], ones], axis=0), e,
                     preferred_element_type=F32)
        if has_cache:
            ec = jnp.exp2(sct - mx).astype(BF16)
            oe = oe + jnp.dot(jnp.concatenate([vct[sl, :], ones_c], axis=0), ec,
                              preferred_element_type=F32)
        den = oe[V_DIM:V_DIM + 1, :]
        c0 = 1.0 / den[:, :nq]
        c1 = lam / den[:, nq:]
        oh = oe[:V_DIM, :nq] * c0 - oe[:V_DIM, nq:] * c1
        r = lax.rsqrt(jnp.mean(oh * oh, axis=0, keepdims=True) + EPS)
        outs.append(((oh * r) * sgain) * (1.0 - lam_init))
    ot = jnp.concatenate(
        [jnp.concatenate(outs[g * N_HEADS:(g + 1) * N_HEADS], axis=0) for g in range(len(groups))],
        axis=1)
    o = ot.T.astype(BF16)
    mix = y_proj + jnp.dot(o, wo_ref[0:D_ATTN, :].astype(BF16), preferred_element_type=F32)
    g1 = m_ref[...][:, 2 * D_MODEL:3 * D_MODEL]
    o_ref[...] = x_ref[...] + g1 * mix


def _attn(x, mod, layer, row0, mod_span, seq_len, tq, q, k, vt, cache, u, bg, lam_ps, subln_g,
          conv_w, conv_g, w_out, lam_init):
    n = x.shape[0]
    grp_rows = max(seq_len, tq)
    qt = grp_rows // tq
    hb = tq // BF16_ROWS
    n_hblocks = n // BF16_ROWS
    tokmap = lambda b, j: (b * qt + j, 0)
    in_specs = [
        pl.BlockSpec((tq, D_MODEL), tokmap),
        _mod_spec(layer, lambda b, j: row0 + (b * grp_rows) // mod_span),
        pl.BlockSpec((tq, D_ATTN), tokmap),
        pl.BlockSpec((grp_rows, D_ATTN), lambda b, j: (b, 0)),
        pl.BlockSpec((D_ATTN, grp_rows), lambda b, j: (0, b)),
    ]
    args = [x, mod, q, k, vt]
    if cache is not None:
        for c in cache:
            p = c.shape[2]
            in_specs.append(pl.BlockSpec((None, None, p, D_ATTN), lambda b, j: (b, layer, 0, 0)))
            args.append(c)
    in_specs += [
        pl.BlockSpec((tq, D_CONV), tokmap),
        pl.BlockSpec((BF16_ROWS, D_CONV), lambda b, j: (jnp.maximum((b * qt + j) * hb - 1, 0), 0)),
        pl.BlockSpec((BF16_ROWS, D_CONV),
                     lambda b, j: (jnp.minimum((b * qt + j + 1) * hb, n_hblocks - 1), 0)),
        pl.BlockSpec((tq, D_CONV), tokmap),
    ]
    in_specs += [_const_spec((DEPTH, HEAD_DIM))] * 4
    in_specs += [
        _layer_spec((V_DIM, LANES), layer),
        _layer_spec((3, D_CONV), layer),
        _layer_spec((1, D_CONV), layer),
        _layer_spec((D_MODEL, D_MODEL), layer),
    ]
    args += [u, u, u, bg, *lam_ps, subln_g, conv_w, conv_g, w_out]
    return pl.pallas_call(
        functools.partial(_attn_kernel, has_cache=cache is not None, tq=tq, seq_len=seq_len,
                          layer=layer, lam_init=lam_init),
        grid=(n // grp_rows, qt),
        in_specs=in_specs,
        out_specs=pl.BlockSpec((tq, D_MODEL), tokmap),
        out_shape=jax.ShapeDtypeStruct(x.shape, F32),
        compiler_params=_cparams(2),
        name="attn",
    )(*args)


def _ffn_kernel(*refs, tile, seq_len, halo):
    it = iter(refs)
    x_ref = next(it)
    if halo:
        xp_ref, xn_ref = next(it), next(it)
    m_ref, g2_ref, wup_ref, cw_ref, wdn_ref, o_ref, hs_ref, u_ref, g_ref = (
        next(it) for _ in range(9))

    m = m_ref[...]
    sh2 = m[:, 3 * D_MODEL:4 * D_MODEL]
    sc2 = m[:, 4 * D_MODEL:5 * D_MODEL]
    gate = m[:, 5 * D_MODEL:6 * D_MODEL]

    def pre(xv):
        return ((xv * _rms_rows(xv)) * g2_ref[...]) * (1.0 + sc2) + sh2

    x = x_ref[...]
    i = pl.program_id(0)
    base = HALO if halo else 0
    hs_ref[base:base + tile, :] = pre(x).astype(BF16)
    if halo:
        tiles_per_seq = seq_len // tile
        jt = i % tiles_per_seq
        hs_ref[0:HALO, :] = jnp.where(jt == 0, 0.0, pre(xp_ref[...])).astype(BF16)
        hs_ref[HALO + tile:, :] = jnp.where(jt == tiles_per_seq - 1, 0.0,
                                            pre(xn_ref[...])).astype(BF16)
    if halo:
        segs = [(HALO, tile, 0)]
    else:
        segs = [(_U_PAD + s * (seq_len + _U_PAD), seq_len, s * seq_len)
                for s in range(tile // seq_len)]

        @pl.when(i == 0)
        def _():
            u_ref[...] = jnp.zeros_like(u_ref)

    hs = hs_ref[...]
    for c in range(N_FF_CHUNKS):
        slot = c % 2
        cols = (c * FF_CHUNK, D_FF + c * FF_CHUNK)
        for half, c0 in enumerate(cols):
            uu = jnp.dot(hs, wup_ref[:, c0:c0 + FF_CHUNK].astype(BF16),
                         preferred_element_type=F32)
            lanes = slice(half * FF_CHUNK, (half + 1) * FF_CHUNK)
            if halo:
                u_ref[slot, :, lanes] = uu
            else:
                for b0, ln, r0 in segs:
                    u_ref[slot, b0:b0 + ln, lanes] = uu[r0:r0 + ln]
        for b0, ln, r0 in segs:
            conv = []
            for half, c0 in enumerate(cols):
                lanes = slice(half * FF_CHUNK, (half + 1) * FF_CHUNK)
                w3 = cw_ref[:, c0:c0 + FF_CHUNK]
                conv.append(u_ref[slot, b0 - 1:b0 - 1 + ln, lanes] * w3[0:1]
                            + u_ref[slot, b0:b0 + ln, lanes] * w3[1:2]
                            + u_ref[slot, b0 + 1:b0 + 1 + ln, lanes] * w3[2:3])
            ca, cb = conv
            g_ref[r0:r0 + ln, c * FF_CHUNK:(c + 1) * FF_CHUNK] = (
                (ca * jax.nn.sigmoid(ca)) * cb).astype(BF16)

    gs = g_ref[...]
    for n0 in range(0, D_MODEL, FF_CHUNK):
        d = jnp.dot(gs, wdn_ref[:, n0:n0 + FF_CHUNK].astype(BF16), preferred_element_type=F32)
        o_ref[:, n0:n0 + FF_CHUNK] = x[:, n0:n0 + FF_CHUNK] + gate[:, n0:n0 + FF_CHUNK] * d


def _ffn(x, mod, layer, row0, mod_span, seq_len, halo, norm2_g, w_up, conv_w, w_down):
    n = x.shape[0]
    tile = TOKEN_TILE
    hb = tile // BF16_ROWS
    n_hblocks = n // BF16_ROWS
    in_specs = [pl.BlockSpec((tile, D_MODEL), lambda i: (i, 0))]
    args = [x]
    if halo:
        in_specs += [
            pl.BlockSpec((HALO, D_MODEL), lambda i: (jnp.maximum(i * hb - 1, 0), 0)),
            pl.BlockSpec((HALO, D_MODEL), lambda i: (jnp.minimum((i + 1) * hb, n_hblocks - 1), 0)),
        ]
        args += [x, x]
    in_specs += [
        _mod_spec(layer, lambda i: row0 + (i * tile) // mod_span),
        _layer_spec((1, D_MODEL), layer),
        _layer_spec((D_MODEL, 2 * D_FF), layer),
        _layer_spec((3, 2 * D_FF), layer),
        _layer_spec((D_FF, D_MODEL), layer),
    ]
    args += [mod, norm2_g, w_up, conv_w, w_down]
    if halo:
        rows_h = rows_u = tile + 2 * HALO
    else:
        rows_h = tile
        rows_u = _U_PAD + (tile // seq_len) * (seq_len + _U_PAD)
    return pl.pallas_call(
        functools.partial(_ffn_kernel, tile=tile, seq_len=seq_len, halo=halo),
        grid=(n // tile,),
        in_specs=in_specs,
        out_specs=pl.BlockSpec((tile, D_MODEL), lambda i: (i, 0)),
        out_shape=jax.ShapeDtypeStruct(x.shape, F32),
        scratch_shapes=[pltpu.VMEM((rows_h, D_MODEL), BF16),
                        pltpu.VMEM((2, rows_u, 2 * FF_CHUNK), F32),
                        pltpu.VMEM((tile, D_FF), BF16)],
        compiler_params=_cparams(1),
        name="ffn",
    )(*args)


def _rope_tables(n_tok):
    rows = n_tok // GRID_W
    row = jnp.repeat(jnp.arange(rows, dtype=F32), GRID_W)
    col = jnp.tile(jnp.arange(GRID_W, dtype=F32), rows)
    n_freq = HEAD_DIM // 4
    inv = ROPE_BASE ** (-jnp.arange(n_freq, dtype=F32) / n_freq)
    ar, ac = row[:, None] * inv[None], col[:, None] * inv[None]
    cr, sr, cc, sc = jnp.cos(ar), jnp.sin(ar), jnp.cos(ac), jnp.sin(ac)
    z = jnp.zeros_like(sr)
    rep = lambda parts: jnp.tile(jnp.concatenate(parts, axis=-1), (1, LANES // HEAD_DIM))
    return rep([cr, cr, cc, cc]), rep([-sr, z, -sc, z]), rep([z, sr, z, sc])


def kernel(x_prompt, x_sample, cache_k, cache_v, c, c_ctx, w_mod, b_mod, norm1_g, w_in, q_norm_g, k_norm_g, lambda_q1, lambda_k1, lambda_q2, lambda_k2, subln_g, conv_w, conv_norm_g, w_out, norm2_g, w_up, ffn_conv_w, w_down):
    batch, seq, _ = x_prompt.shape
    dec_batch, dec_seq, _ = x_sample.shape
    past = cache_k.shape[2]

    cvecs = jnp.zeros((MOD_ROWS, D_MODEL), F32).at[0].set(c_ctx).at[1:1 + dec_batch].set(c)
    mod = _modulation(cvecs, w_mod, b_mod).reshape(DEPTH, MOD_ROWS, 1, N_MOD * D_MODEL)

    grp = jnp.arange(D_ATTN) // HEAD_DIM
    gmat = jnp.where(grp[:, None] == grp[None, :], 1.0 / HEAD_DIM, 0.0).astype(BF16)
    row3 = lambda g: g.reshape(DEPTH, 1, -1)
    qg = jnp.tile(row3(q_norm_g), (1, 1, D_ATTN // HEAD_DIM))
    kg = jnp.tile(row3(k_norm_g), (1, 1, D_ATTN // HEAD_DIM))
    n1, n2, cng = row3(norm1_g), row3(norm2_g), row3(conv_norm_g)
    sg = jnp.broadcast_to(subln_g[:, :, None], (DEPTH, V_DIM, LANES))
    lam_ps = (lambda_q1, lambda_k1, lambda_q2, lambda_k2)
    rope_tabs = _rope_tables(dec_seq)
    cache = (cache_k.reshape(dec_batch, DEPTH, past, D_ATTN),
             cache_v.reshape(dec_batch, DEPTH, past, D_ATTN))

    def run(x, row0, mod_span, seq_len, tq, ffn_halo, rope, cache, emit):
        kv = "new" if emit else None
        for l in range(DEPTH):
            lam_init = 0.8 - 0.6 * math.exp(-0.3 * l)
            outs = _proj(x, mod, l, row0, mod_span, seq_len, n1, w_in, qg, kg, gmat, rope, kv)
            q, k, v, u, bg = outs[:5]
            if emit:
                kv = tuple(outs[5:])
            x = _attn(x, mod, l, row0, mod_span, seq_len, tq, q, k, v, cache, u, bg, lam_ps, sg,
                      conv_w, cng, w_out, lam_init)
            x = _ffn(x, mod, l, row0, mod_span, seq_len, ffn_halo, n2, w_up, ffn_conv_w, w_down)
        return x, kv

    xp, (new_k, new_v) = run(x_prompt.reshape(batch * seq, D_MODEL), 0, batch * seq, seq,
                             CTX_Q_TILE, False, None, None, True)
    xs, _ = run(x_sample.reshape(dec_batch * dec_seq, D_MODEL), 1, dec_seq, dec_seq, Q_TILE, True,
                rope_tabs, cache, False)

    return (xp.reshape(batch, seq, D_MODEL), xs.reshape(dec_batch, dec_seq, D_MODEL),
            new_k.reshape(batch, DEPTH, seq, N_HEADS, 2, HEAD_DIM),
            new_v.reshape(batch, DEPTH, seq, N_HEADS, V_DIM))
```

```python
import functools
import math

import jax
import jax.numpy as jnp
from jax import lax
from jax.experimental import pallas as pl
from jax.experimental.pallas import tpu as pltpu

D_MODEL = 1024
DEPTH = 2
GRID_W = 64
D_ATTN = 512
D_CONV = 512
N_HEADS = 4
HEAD_DIM = 64
V_DIM = 128
D_FF = 2816
ROPE_BASE = 10000.0
EPS = 1e-6
N_MOD = 6
D_IN = 3 * D_ATTN + 3 * D_CONV

F32 = jnp.float32
BF16 = jnp.bfloat16

LANES = 128
BF16_ROWS = 16
F32_ROWS = 8
MOD_ROWS = F32_ROWS
_U_PAD = F32_ROWS
VMEM_LIMIT = 60 * 1024 * 1024

FF_CHUNK = 256
N_FF_CHUNKS = D_FF // FF_CHUNK
HALO = BF16_ROWS
TOKEN_TILE = 512
FFN_TILE = 512
Q_TILE = 512
CTX_Q_TILE = 512


def _cparams(n_axes):
    return pltpu.CompilerParams(
        dimension_semantics=("arbitrary",) * n_axes, vmem_limit_bytes=VMEM_LIMIT)


def _layer_spec(shape, layer):
    nd = len(shape)
    return pl.BlockSpec((None,) + tuple(shape), lambda *_: (layer,) + (0,) * nd,
                        pipeline_mode=pl.Buffered(1))


def _const_spec(shape):
    nd = len(shape)
    return pl.BlockSpec(shape, lambda *_: (0,) * nd, pipeline_mode=pl.Buffered(1))


def _mod_spec(layer, row_of):
    return pl.BlockSpec((None, None, 1, N_MOD * D_MODEL),
                        lambda *idx: (layer, row_of(*idx), 0, 0))


def _rms_rows(x):
    return lax.rsqrt(jnp.mean(x * x, axis=-1, keepdims=True) + EPS)


def _mod_kernel(cv_ref, w_ref, b_ref, o_ref):
    cv = cv_ref[...]
    s = (cv * jax.nn.sigmoid(cv)).astype(BF16)
    o_ref[...] = jnp.dot(s, w_ref[...].astype(BF16), preferred_element_type=F32) + b_ref[...]


def _modulation(cvecs, w_mod, b_mod):
    tn = 1536
    n_out = N_MOD * D_MODEL
    return pl.pallas_call(
        _mod_kernel,
        grid=(DEPTH, n_out // tn),
        in_specs=[
            pl.BlockSpec((MOD_ROWS, D_MODEL), lambda l, j: (0, 0)),
            pl.BlockSpec((None, D_MODEL, tn), lambda l, j: (l, 0, j)),
            pl.BlockSpec((None, 1, tn), lambda l, j: (l, 0, j)),
        ],
        out_specs=pl.BlockSpec((None, MOD_ROWS, tn), lambda l, j: (l, 0, j)),
        out_shape=jax.ShapeDtypeStruct((DEPTH, MOD_ROWS, n_out), F32),
        compiler_params=_cparams(2),
        name="adaln_mod",
    )(cvecs, w_mod, b_mod.reshape(DEPTH, 1, n_out))


def _group_norm_gain(z, gmat_ref, gain):
    ms = jnp.dot((z * z).astype(BF16), gmat_ref[...], preferred_element_type=F32)
    return z * lax.rsqrt(ms + EPS) * gain


def _rope(z, c_ref, sa_ref, sb_ref):
    c, sa, sb = c_ref[...], sa_ref[...], sb_ref[...]
    cols = []
    for j in range(z.shape[1] // LANES):
        zj = z[:, j * LANES:(j + 1) * LANES]
        hi = pltpu.roll(zj, LANES - HEAD_DIM // 4, axis=1)
        lo = pltpu.roll(zj, HEAD_DIM // 4, axis=1)
        cols.append(zj * c + hi * sa + lo * sb)
    return jnp.concatenate(cols, axis=1)


def _proj_kernel(*refs, rope, emit_f32, seq_len, layer):
    it = iter(refs)
    x_ref, m_ref, g1_ref, w_ref, qg_ref, kg_ref, gmat_ref = (next(it) for _ in range(7))
    if rope:
        c_ref, sa_ref, sb_ref = (next(it) for _ in range(3))
    if emit_f32 == "alias":
        next(it), next(it)
    q_ref, k_ref, v_ref, u_ref, bg_ref = (next(it) for _ in range(5))
    if emit_f32:
        kf_ref, vf_ref = next(it), next(it)

    x = x_ref[...]
    m = m_ref[...]
    sh1 = m[:, 0:D_MODEL]
    sc1 = m[:, D_MODEL:2 * D_MODEL]
    h = ((x * _rms_rows(x)) * g1_ref[...]) * (1.0 + sc1) + sh1
    hb = h.astype(BF16)

    def col(i0, n):
        return jnp.dot(hb, w_ref[:, i0:i0 + n].astype(BF16), preferred_element_type=F32)

    q = _group_norm_gain(col(0, D_ATTN), gmat_ref, qg_ref[...])
    k = _group_norm_gain(col(D_ATTN, D_ATTN), gmat_ref, kg_ref[...])
    if rope:
        q = _rope(q, c_ref, sa_ref, sb_ref)
        k = _rope(k, c_ref, sa_ref, sb_ref)
    v = col(2 * D_ATTN, D_ATTN)
    q_ref[...] = (q * (HEAD_DIM ** -0.5 * math.log2(math.e))).astype(BF16)
    k_ref[...] = k.astype(BF16)
    v_ref[...] = v.T.astype(BF16)
    if emit_f32 == "alias":
        n_seq = x.shape[0] // seq_len
        kf_ref[...] = k.reshape(n_seq, seq_len, D_ATTN)
        vf_ref[...] = v.reshape(n_seq, seq_len, D_ATTN)
    elif emit_f32:
        n_seq = x.shape[0] // seq_len
        zero = jnp.zeros((n_seq, seq_len, D_ATTN), F32)
        for l in range(DEPTH):
            kf_ref[:, l] = k.reshape(n_seq, seq_len, D_ATTN) if l == layer else zero
            vf_ref[:, l] = v.reshape(n_seq, seq_len, D_ATTN) if l == layer else zero
    bg_ref[...] = col(3 * D_ATTN, D_CONV).astype(BF16)
    cg = col(3 * D_ATTN + D_CONV, D_CONV)
    xc = col(3 * D_ATTN + 2 * D_CONV, D_CONV)
    u_ref[...] = (cg * xc).astype(BF16)


def _proj(x, mod, layer, row0, mod_span, seq_len, norm1_g, w_in, qg, kg, gmat, rope_tabs, kv_out):
    n = x.shape[0]
    tile = TOKEN_TILE
    tiles_per_seq = max(seq_len // tile, 1)
    in_specs = [
        pl.BlockSpec((tile, D_MODEL), lambda i: (i, 0)),
        _mod_spec(layer, lambda i: row0 + (i * tile) // mod_span),
        _layer_spec((1, D_MODEL), layer),
        _layer_spec((D_MODEL, D_IN), layer),
        _layer_spec((1, D_ATTN), layer),
        _layer_spec((1, D_ATTN), layer),
        _const_spec((D_ATTN, D_ATTN)),
    ]
    args = [x, mod, norm1_g, w_in, qg, kg, gmat]
    if rope_tabs is not None:
        for t in rope_tabs:
            in_specs.append(pl.BlockSpec((tile, LANES), lambda i: (i % tiles_per_seq, 0)))
            args.append(t)
    aliases = {}
    emit = False
    if kv_out is not None and kv_out != "new":
        emit = "alias"
        for a in kv_out:
            aliases[len(args)] = 5 + len(aliases)
            in_specs.append(pl.BlockSpec(memory_space=pl.ANY))
            args.append(a)
    elif kv_out == "new":
        emit = True
    tok = jax.ShapeDtypeStruct((n, D_ATTN), BF16)
    tok_spec = pl.BlockSpec((tile, D_ATTN), lambda i: (i, 0))
    out_shape = [tok, tok, jax.ShapeDtypeStruct((D_ATTN, n), BF16), tok, tok]
    out_specs = [tok_spec, tok_spec, pl.BlockSpec((D_ATTN, tile), lambda i: (0, i)), tok_spec,
                 tok_spec]
    if emit:
        n_seq = tile // seq_len
        kv_shape = jax.ShapeDtypeStruct((n // seq_len, DEPTH, seq_len, D_ATTN), F32)
        out_shape += [kv_shape] * 2
        if emit == "alias":
            kv_spec = pl.BlockSpec((n_seq, None, seq_len, D_ATTN), lambda i: (i, layer, 0, 0))
        else:
            kv_spec = pl.BlockSpec((n_seq, DEPTH, seq_len, D_ATTN), lambda i: (i, 0, 0, 0))
        out_specs += [kv_spec] * 2
    return pl.pallas_call(
        functools.partial(_proj_kernel, rope=rope_tabs is not None, emit_f32=emit, seq_len=seq_len,
                          layer=layer),
        grid=(n // tile,),
        in_specs=in_specs,
        out_specs=out_specs,
        out_shape=out_shape,
        input_output_aliases=aliases,
        compiler_params=_cparams(1),
        name="proj",
    )(*args)


def _shift_rows(u, prev_row, next_row, row_in_seq, seq_len):
    t = u.shape[0]
    ridx = lax.broadcasted_iota(jnp.int32, (t, 1), 0)
    up = jnp.where(ridx == 0, prev_row, pltpu.roll(u, 1, axis=0))
    dn = jnp.where(ridx == t - 1, next_row, pltpu.roll(u, t - 1, axis=0))
    up = jnp.where(row_in_seq == 0, 0.0, up)
    dn = jnp.where(row_in_seq == seq_len - 1, 0.0, dn)
    return up, dn


def _attn_kernel(*refs, has_cache, cast_ffn, tq, seq_len, layer, lam_init):
    it = iter(refs)
    x_ref, m_ref, q_ref, k_ref, vt_ref = (next(it) for _ in range(5))
    if has_cache:
        kc_ref, vc_ref = next(it), next(it)
    u_ref, up_ref, un_ref, bg_ref = (next(it) for _ in range(4))
    lam_refs = [next(it) for _ in range(4)]
    sg_ref, cw_ref, cg_ref, wo_ref = (next(it) for _ in range(4))
    if cast_ffn:
        wu_ref, wd_ref = next(it), next(it)
    o_ref = next(it)
    if cast_ffn:
        wub_ref, wdb_ref = next(it), next(it)
        wub_ref[...] = wu_ref[...].astype(BF16)
        wdb_ref[...] = wd_ref[...].astype(BF16)

    lq1, lk1, lq2, lk2 = (r[layer:layer + 1, :] for r in lam_refs)
    lam = (jnp.exp(jnp.sum(lq1 * lk1, axis=-1, keepdims=True))
           - jnp.exp(jnp.sum(lq2 * lk2, axis=-1, keepdims=True)) + lam_init)

    nq = min(tq, seq_len)
    if seq_len >= tq:
        groups = [(0, 0, k_ref.shape[0])]
    else:
        groups = [(g * seq_len, g * seq_len, seq_len) for g in range(tq // seq_len)]

    lane = lax.broadcasted_iota(jnp.int32, (1, LANES), 1)
    sub0 = jnp.where(lane < HEAD_DIM, 1.0, 0.0).astype(BF16)
    sub1 = jnp.where(lane < HEAD_DIM, 0.0, 1.0).astype(BF16)
    nt = (((1,), (1,)), ((), ()))
    ones = jnp.ones((BF16_ROWS, groups[0][2]), BF16)
    sgain = jnp.concatenate([sg_ref[...]] * (nq // LANES), axis=1)
    if has_cache:
        nc = kc_ref.shape[0]
        ones_c = jnp.ones((BF16_ROWS, nc), BF16)
        vct = vc_ref[...].T.astype(BF16)

    def scores(job):
        (q0, k0, nk), h = job
        sl = slice(h * LANES, (h + 1) * LANES)
        qh = q_ref[q0:q0 + nq, sl]
        q2 = jnp.concatenate([qh * sub0, qh * sub1], axis=0)
        st = lax.dot_general(k_ref[k0:k0 + nk, sl], q2, nt, preferred_element_type=F32)
        if not has_cache:
            return st, None
        return st, lax.dot_general(kc_ref[:, sl].astype(BF16), q2, nt,
                                   preferred_element_type=F32)

    jobs = [(grp, h) for grp in groups for h in range(N_HEADS)]
    nxt = scores(jobs[0])

    j = pl.program_id(1)
    row_in_seq = (j * tq + lax.broadcasted_iota(jnp.int32, (tq, 1), 0)) % seq_len
    u = u_ref[...].astype(F32)
    up, dn = _shift_rows(u, up_ref[BF16_ROWS - 1:BF16_ROWS, :].astype(F32),
                         un_ref[0:1, :].astype(F32), row_in_seq, seq_len)
    cw = cw_ref[...]
    t = bg_ref[...].astype(F32) * (up * cw[0:1] + u * cw[1:2] + dn * cw[2:3])
    y = ((t * _rms_rows(t)) * cg_ref[...]).astype(BF16)
    y_proj = jnp.dot(y, wo_ref[D_ATTN:, :].astype(BF16), preferred_element_type=F32)

    outs = []
    for n, ((q0, k0, nk), h) in enumerate(jobs):
        sl = slice(h * LANES, (h + 1) * LANES)
        st, sct = nxt
        if n + 1 < len(jobs):
            nxt = scores(jobs[n + 1])
        mx = jnp.max(st, axis=0, keepdims=True)
        if has_cache:
            mx = jnp.maximum(mx, jnp.max(sct, axis=0, keepdims=True))
        e = jnp.exp2(st - mx).astype(BF16)
        oe = jnp.dot(jnp.concatenate([vt_ref[sl, k0:k0 + nk], ones], axis=0), e,
                     preferred_element_type=F32)
        if has_cache:
            ec = jnp.exp2(sct - mx).astype(BF16)
            oe = oe + jnp.dot(jnp.concatenate([vct[sl, :], ones_c], axis=0), ec,
                              preferred_element_type=F32)
        den = oe[V_DIM:V_DIM + 1, :]
        c0 = 1.0 / den[:, :nq]
        c1 = lam / den[:, nq:]
        oh = oe[:V_DIM, :nq] * c0 - oe[:V_DIM, nq:] * c1
        r = lax.rsqrt(jnp.mean(oh * oh, axis=0, keepdims=True) + EPS)
        outs.append(((oh * r) * sgain) * (1.0 - lam_init))
    ot = jnp.concatenate(
        [jnp.concatenate(outs[g * N_HEADS:(g + 1) * N_HEADS], axis=0) for g in range(len(groups))],
        axis=1)
    o = ot.T.astype(BF16)
    mix = y_proj + jnp.dot(o, wo_ref[0:D_ATTN, :].astype(BF16), preferred_element_type=F32)
    g1 = m_ref[...][:, 2 * D_MODEL:3 * D_MODEL]
    o_ref[...] = x_ref[...] + g1 * mix


def _attn(x, mod, layer, row0, mod_span, seq_len, tq, q, k, vt, cache, u, bg, lam_ps, subln_g,
          conv_w, conv_g, w_out, lam_init, ffn_weights=None):
    n = x.shape[0]
    grp_rows = max(seq_len, tq)
    qt = grp_rows // tq
    hb = tq // BF16_ROWS
    n_hblocks = n // BF16_ROWS
    tokmap = lambda b, j: (b * qt + j, 0)
    in_specs = [
        pl.BlockSpec((tq, D_MODEL), tokmap),
        _mod_spec(layer, lambda b, j: row0 + (b * grp_rows) // mod_span),
        pl.BlockSpec((tq, D_ATTN), tokmap),
        pl.BlockSpec((grp_rows, D_ATTN), lambda b, j: (b, 0)),
        pl.BlockSpec((D_ATTN, grp_rows), lambda b, j: (0, b)),
    ]
    args = [x, mod, q, k, vt]
    if cache is not None:
        for c in cache:
            p = c.shape[2]
            in_specs.append(pl.BlockSpec((None, None, p, D_ATTN), lambda b, j: (b, layer, 0, 0)))
            args.append(c)
    in_specs += [
        pl.BlockSpec((tq, D_CONV), tokmap),
        pl.BlockSpec((BF16_ROWS, D_CONV), lambda b, j: (jnp.maximum((b * qt + j) * hb - 1, 0), 0)),
        pl.BlockSpec((BF16_ROWS, D_CONV),
                     lambda b, j: (jnp.minimum((b * qt + j + 1) * hb, n_hblocks - 1), 0)),
        pl.BlockSpec((tq, D_CONV), tokmap),
    ]
    in_specs += [_const_spec((DEPTH, HEAD_DIM))] * 4
    in_specs += [
        _layer_spec((V_DIM, LANES), layer),
        _layer_spec((3, D_CONV), layer),
        _layer_spec((1, D_CONV), layer),
        _layer_spec((D_MODEL, D_MODEL), layer),
    ]
    args += [u, u, u, bg, *lam_ps, subln_g, conv_w, conv_g, w_out]
    out_specs = [pl.BlockSpec((tq, D_MODEL), tokmap)]
    out_shape = [jax.ShapeDtypeStruct(x.shape, F32)]
    if ffn_weights is not None:
        n_steps = n // tq
        for w in ffn_weights:
            rows, cols = w.shape[1:]
            slab = rows // n_steps
            assert slab * n_steps == rows and slab % BF16_ROWS == 0, (rows, n_steps)
            in_specs.append(pl.BlockSpec((None, slab, cols), lambda b, j: (layer, b * qt + j, 0)))
            out_specs.append(pl.BlockSpec((slab, cols), lambda b, j: (b * qt + j, 0)))
            out_shape.append(jax.ShapeDtypeStruct((rows, cols), BF16))
            args.append(w)
    outs = pl.pallas_call(
        functools.partial(_attn_kernel, has_cache=cache is not None,
                          cast_ffn=ffn_weights is not None, tq=tq, seq_len=seq_len, layer=layer,
                          lam_init=lam_init),
        grid=(n // grp_rows, qt),
        in_specs=in_specs,
        out_specs=out_specs,
        out_shape=out_shape,
        compiler_params=_cparams(2),
        name="attn",
    )(*args)
    return outs[0], tuple(outs[1:])


def _ffn_kernel(*refs, tile, seq_len, halo):
    it = iter(refs)
    x_ref = next(it)
    if halo:
        xp_ref, xn_ref = next(it), next(it)
    m_ref, g2_ref, wup_ref, cw_ref, wdn_ref, o_ref, hs_ref, u_ref, g_ref = (
        next(it) for _ in range(9))

    m = m_ref[...]
    sh2 = m[:, 3 * D_MODEL:4 * D_MODEL]
    sc2 = m[:, 4 * D_MODEL:5 * D_MODEL]
    gate = m[:, 5 * D_MODEL:6 * D_MODEL]

    def pre(xv):
        return ((xv * _rms_rows(xv)) * g2_ref[...]) * (1.0 + sc2) + sh2

    x = x_ref[...]
    i = pl.program_id(0)
    base = HALO if halo else 0
    hs_ref[base:base + tile, :] = pre(x).astype(BF16)
    if halo:
        tiles_per_seq = seq_len // tile
        jt = i % tiles_per_seq
        hs_ref[0:HALO, :] = jnp.where(jt == 0, 0.0, pre(xp_ref[...])).astype(BF16)
        hs_ref[HALO + tile:, :] = jnp.where(jt == tiles_per_seq - 1, 0.0,
                                            pre(xn_ref[...])).astype(BF16)
    if halo:
        segs = [(HALO, tile, 0)]
    else:
        segs = [(_U_PAD + s * (seq_len + _U_PAD), seq_len, s * seq_len)
                for s in range(tile // seq_len)]

        @pl.when(i == 0)
        def _():
            u_ref[...] = jnp.zeros_like(u_ref)

    hs = hs_ref[...]
    for c in range(N_FF_CHUNKS):
        slot = c % 2
        cols = (c * FF_CHUNK, D_FF + c * FF_CHUNK)
        for half, c0 in enumerate(cols):
            uu = jnp.dot(hs, wup_ref[:, c0:c0 + FF_CHUNK], preferred_element_type=F32)
            lanes = slice(half * FF_CHUNK, (half + 1) * FF_CHUNK)
            if halo:
                u_ref[slot, :, lanes] = uu
            else:
                for b0, ln, r0 in segs:
                    u_ref[slot, b0:b0 + ln, lanes] = uu[r0:r0 + ln]
        for b0, ln, r0 in segs:
            conv = []
            for half, c0 in enumerate(cols):
                lanes = slice(half * FF_CHUNK, (half + 1) * FF_CHUNK)
                w3 = cw_ref[:, c0:c0 + FF_CHUNK]
                conv.append(u_ref[slot, b0 - 1:b0 - 1 + ln, lanes] * w3[0:1]
                            + u_ref[slot, b0:b0 + ln, lanes] * w3[1:2]
                            + u_ref[slot, b0 + 1:b0 + 1 + ln, lanes] * w3[2:3])
            ca, cb = conv
            g_ref[r0:r0 + ln, c * FF_CHUNK:(c + 1) * FF_CHUNK] = (
                (ca * jax.nn.sigmoid(ca)) * cb).astype(BF16)

    gs = g_ref[...]
    for n0 in range(0, D_MODEL, FF_CHUNK):
        d = jnp.dot(gs, wdn_ref[:, n0:n0 + FF_CHUNK], preferred_element_type=F32)
        o_ref[:, n0:n0 + FF_CHUNK] = x[:, n0:n0 + FF_CHUNK] + gate[:, n0:n0 + FF_CHUNK] * d


def _ffn(x, mod, layer, row0, mod_span, seq_len, halo, norm2_g, w_up_b, conv_w, w_down_b):
    n = x.shape[0]
    tile = FFN_TILE
    hb = tile // BF16_ROWS
    n_hblocks = n // BF16_ROWS
    in_specs = [pl.BlockSpec((tile, D_MODEL), lambda i: (i, 0))]
    args = [x]
    if halo:
        in_specs += [
            pl.BlockSpec((HALO, D_MODEL), lambda i: (jnp.maximum(i * hb - 1, 0), 0)),
            pl.BlockSpec((HALO, D_MODEL), lambda i: (jnp.minimum((i + 1) * hb, n_hblocks - 1), 0)),
        ]
        args += [x, x]
    in_specs += [
        _mod_spec(layer, lambda i: row0 + (i * tile) // mod_span),
        _layer_spec((1, D_MODEL), layer),
        _const_spec((D_MODEL, 2 * D_FF)),
        _layer_spec((3, 2 * D_FF), layer),
        _const_spec((D_FF, D_MODEL)),
    ]
    args += [mod, norm2_g, w_up_b, conv_w, w_down_b]
    if halo:
        rows_h = rows_u = tile + 2 * HALO
    else:
        rows_h = tile
        rows_u = _U_PAD + (tile // seq_len) * (seq_len + _U_PAD)
    return pl.pallas_call(
        functools.partial(_ffn_kernel, tile=tile, seq_len=seq_len, halo=halo),
        grid=(n // tile,),
        in_specs=in_specs,
        out_specs=pl.BlockSpec((tile, D_MODEL), lambda i: (i, 0)),
        out_shape=jax.ShapeDtypeStruct(x.shape, F32),
        scratch_shapes=[pltpu.VMEM((rows_h, D_MODEL), BF16),
                        pltpu.VMEM((2, rows_u, 2 * FF_CHUNK), F32),
                        pltpu.VMEM((tile, D_FF), BF16)],
        compiler_params=_cparams(1),
        name="ffn",
    )(*args)


def _rope_tables(n_tok):
    rows = n_tok // GRID_W
    row = jnp.repeat(jnp.arange(rows, dtype=F32), GRID_W)
    col = jnp.tile(jnp.arange(GRID_W, dtype=F32), rows)
    n_freq = HEAD_DIM // 4
    inv = ROPE_BASE ** (-jnp.arange(n_freq, dtype=F32) / n_freq)
    ar, ac = row[:, None] * inv[None], col[:, None] * inv[None]
    cr, sr, cc, sc = jnp.cos(ar), jnp.sin(ar), jnp.cos(ac), jnp.sin(ac)
    z = jnp.zeros_like(sr)
    rep = lambda parts: jnp.tile(jnp.concatenate(parts, axis=-1), (1, LANES // HEAD_DIM))
    return rep([cr, cr, cc, cc]), rep([-sr, z, -sc, z]), rep([z, sr, z, sc])


def kernel(x_prompt, x_sample, cache_k, cache_v, c, c_ctx, w_mod, b_mod, norm1_g, w_in, q_norm_g, k_norm_g, lambda_q1, lambda_k1, lambda_q2, lambda_k2, subln_g, conv_w, conv_norm_g, w_out, norm2_g, w_up, ffn_conv_w, w_down):
    batch, seq, _ = x_prompt.shape
    dec_batch, dec_seq, _ = x_sample.shape
    past = cache_k.shape[2]

    cvecs = jnp.zeros((MOD_ROWS, D_MODEL), F32).at[0].set(c_ctx).at[1:1 + dec_batch].set(c)
    mod = _modulation(cvecs, w_mod, b_mod).reshape(DEPTH, MOD_ROWS, 1, N_MOD * D_MODEL)

    grp = jnp.arange(D_ATTN) // HEAD_DIM
    gmat = jnp.where(grp[:, None] == grp[None, :], 1.0 / HEAD_DIM, 0.0).astype(BF16)
    row3 = lambda g: g.reshape(DEPTH, 1, -1)
    qg = jnp.tile(row3(q_norm_g), (1, 1, D_ATTN // HEAD_DIM))
    kg = jnp.tile(row3(k_norm_g), (1, 1, D_ATTN // HEAD_DIM))
    n1, n2, cng = row3(norm1_g), row3(norm2_g), row3(conv_norm_g)
    sg = jnp.broadcast_to(subln_g[:, :, None], (DEPTH, V_DIM, LANES))
    lam_ps = (lambda_q1, lambda_k1, lambda_q2, lambda_k2)
    rope_tabs = _rope_tables(dec_seq)
    cache = (cache_k.reshape(dec_batch, DEPTH, past, D_ATTN),
             cache_v.reshape(dec_batch, DEPTH, past, D_ATTN))

    ffn_w = {}

    def run(x, row0, mod_span, seq_len, tq, ffn_halo, rope, cache, first_stream):
        kv = "new" if first_stream else None
        for l in range(DEPTH):
            lam_init = 0.8 - 0.6 * math.exp(-0.3 * l)
            outs = _proj(x, mod, l, row0, mod_span, seq_len, n1, w_in, qg, kg, gmat, rope, kv)
            q, k, v, u, bg = outs[:5]
            if first_stream:
                kv = tuple(outs[5:])
            x, cast = _attn(x, mod, l, row0, mod_span, seq_len, tq, q, k, v, cache, u, bg, lam_ps,
                            sg, conv_w, cng, w_out, lam_init,
                            ffn_weights=(w_up, w_down) if first_stream else None)
            if first_stream:
                ffn_w[l] = cast
            x = _ffn(x, mod, l, row0, mod_span, seq_len, ffn_halo, n2, ffn_w[l][0], ffn_conv_w,
                     ffn_w[l][1])
        return x, kv

    xp, (new_k, new_v) = run(x_prompt.reshape(batch * seq, D_MODEL), 0, batch * seq, seq,
                             CTX_Q_TILE, False, None, None, True)
    xs, _ = run(x_sample.reshape(dec_batch * dec_seq, D_MODEL), 1, dec_seq, dec_seq, Q_TILE, True,
                rope_tabs, cache, False)

    return (xp.reshape(batch, seq, D_MODEL), xs.reshape(dec_batch, dec_seq, D_MODEL),
            new_k.reshape(batch, DEPTH, seq, N_HEADS, 2, HEAD_DIM),
            new_v.reshape(batch, DEPTH, seq, N_HEADS, V_DIM))
```

```python
import functools
import math

import jax
import jax.numpy as jnp
from jax import lax
from jax.experimental import pallas as pl
from jax.experimental.pallas import tpu as pltpu

D_MODEL = 1024
DEPTH = 2
GRID_W = 64
D_ATTN = 512
D_CONV = 512
N_HEADS = 4
HEAD_DIM = 64
V_DIM = 128
D_FF = 2816
ROPE_BASE = 10000.0
EPS = 1e-6
N_MOD = 6
D_IN = 3 * D_ATTN + 3 * D_CONV

F32 = jnp.float32
BF16 = jnp.bfloat16

LANES = 128
BF16_ROWS = 16
F32_ROWS = 8
MOD_ROWS = F32_ROWS
_U_PAD = F32_ROWS
VMEM_LIMIT = 60 * 1024 * 1024

FF_CHUNK = 256
N_FF_CHUNKS = D_FF // FF_CHUNK
_U_COLS = FF_CHUNK // LANES
HALO = BF16_ROWS
TOKEN_TILE = 512
FFN_TILE = 512
Q_TILE = 512
CTX_Q_TILE = 512


def _cparams(n_axes, flags=None):
    return pltpu.CompilerParams(
        dimension_semantics=("arbitrary",) * n_axes, vmem_limit_bytes=VMEM_LIMIT, flags=flags)


def _layer_spec(shape, layer):
    nd = len(shape)
    return pl.BlockSpec((None,) + tuple(shape), lambda *_: (layer,) + (0,) * nd,
                        pipeline_mode=pl.Buffered(1))


def _const_spec(shape):
    nd = len(shape)
    return pl.BlockSpec(shape, lambda *_: (0,) * nd, pipeline_mode=pl.Buffered(1))


def _mod_spec(layer, row_of):
    return pl.BlockSpec((None, None, 1, N_MOD * D_MODEL),
                        lambda *idx: (layer, row_of(*idx), 0, 0))


def _rms_rows(x):
    return lax.rsqrt(jnp.mean(x * x, axis=-1, keepdims=True) + EPS)


def _mod_kernel(cv_ref, w_ref, b_ref, o_ref):
    cv = cv_ref[...]
    s = (cv * jax.nn.sigmoid(cv)).astype(BF16)
    o_ref[...] = jnp.dot(s, w_ref[...].astype(BF16), preferred_element_type=F32) + b_ref[...]


def _modulation(cvecs, w_mod, b_mod):
    tn = 1536
    n_out = N_MOD * D_MODEL
    return pl.pallas_call(
        _mod_kernel,
        grid=(DEPTH, n_out // tn),
        in_specs=[
            pl.BlockSpec((MOD_ROWS, D_MODEL), lambda l, j: (0, 0)),
            pl.BlockSpec((None, D_MODEL, tn), lambda l, j: (l, 0, j)),
            pl.BlockSpec((None, 1, tn), lambda l, j: (l, 0, j)),
        ],
        out_specs=pl.BlockSpec((None, MOD_ROWS, tn), lambda l, j: (l, 0, j)),
        out_shape=jax.ShapeDtypeStruct((DEPTH, MOD_ROWS, n_out), F32),
        compiler_params=_cparams(2),
        name="adaln_mod",
    )(cvecs, w_mod, b_mod.reshape(DEPTH, 1, n_out))


def _group_norm_gain(z, gmat_ref, gain):
    ms = jnp.dot((z * z).astype(BF16), gmat_ref[...], preferred_element_type=F32)
    return z * lax.rsqrt(ms + EPS) * gain


def _rope(z, c_ref, sa_ref, sb_ref):
    c, sa, sb = c_ref[...], sa_ref[...], sb_ref[...]
    cols = []
    for j in range(z.shape[1] // LANES):
        zj = z[:, j * LANES:(j + 1) * LANES]
        hi = pltpu.roll(zj, LANES - HEAD_DIM // 4, axis=1)
        lo = pltpu.roll(zj, HEAD_DIM // 4, axis=1)
        cols.append(zj * c + hi * sa + lo * sb)
    return jnp.concatenate(cols, axis=1)


def _proj_kernel(*refs, rope, emit_f32, seq_len, layer):
    it = iter(refs)
    x_ref, m_ref, g1_ref, w_ref, qg_ref, kg_ref, gmat_ref = (next(it) for _ in range(7))
    if rope:
        c_ref, sa_ref, sb_ref = (next(it) for _ in range(3))
    if emit_f32 == "alias":
        next(it), next(it)
    q_ref, k_ref, v_ref, u_ref, bg_ref = (next(it) for _ in range(5))
    if emit_f32:
        kf_ref, vf_ref = next(it), next(it)

    x = x_ref[...]
    m = m_ref[...]
    sh1 = m[:, 0:D_MODEL]
    sc1 = m[:, D_MODEL:2 * D_MODEL]
    h = ((x * _rms_rows(x)) * g1_ref[...]) * (1.0 + sc1) + sh1
    hb = h.astype(BF16)

    def col(i0, n):
        return jnp.dot(hb, w_ref[:, i0:i0 + n].astype(BF16), preferred_element_type=F32)

    q = _group_norm_gain(col(0, D_ATTN), gmat_ref, qg_ref[...])
    k = _group_norm_gain(col(D_ATTN, D_ATTN), gmat_ref, kg_ref[...])
    if rope:
        q = _rope(q, c_ref, sa_ref, sb_ref)
        k = _rope(k, c_ref, sa_ref, sb_ref)
    v = col(2 * D_ATTN, D_ATTN)
    q_ref[...] = (q * (HEAD_DIM ** -0.5 * math.log2(math.e))).astype(BF16)
    k_ref[...] = k.astype(BF16)
    v_ref[...] = v.T.astype(BF16)
    if emit_f32 == "alias":
        n_seq = x.shape[0] // seq_len
        kf_ref[...] = k.reshape(n_seq, seq_len, D_ATTN)
        vf_ref[...] = v.reshape(n_seq, seq_len, D_ATTN)
    elif emit_f32:
        n_seq = x.shape[0] // seq_len
        zero = jnp.zeros((n_seq, seq_len, D_ATTN), F32)
        for l in range(DEPTH):
            kf_ref[:, l] = k.reshape(n_seq, seq_len, D_ATTN) if l == layer else zero
            vf_ref[:, l] = v.reshape(n_seq, seq_len, D_ATTN) if l == layer else zero
    bg_ref[...] = col(3 * D_ATTN, D_CONV).astype(BF16)
    cg = col(3 * D_ATTN + D_CONV, D_CONV)
    xc = col(3 * D_ATTN + 2 * D_CONV, D_CONV)
    u_ref[...] = (cg * xc).astype(BF16)


def _proj(x, mod, layer, row0, mod_span, seq_len, norm1_g, w_in, qg, kg, gmat, rope_tabs, kv_out):
    n = x.shape[0]
    tile = TOKEN_TILE
    tiles_per_seq = max(seq_len // tile, 1)
    in_specs = [
        pl.BlockSpec((tile, D_MODEL), lambda i: (i, 0)),
        _mod_spec(layer, lambda i: row0 + (i * tile) // mod_span),
        _layer_spec((1, D_MODEL), layer),
        _layer_spec((D_MODEL, D_IN), layer),
        _layer_spec((1, D_ATTN), layer),
        _layer_spec((1, D_ATTN), layer),
        _const_spec((D_ATTN, D_ATTN)),
    ]
    args = [x, mod, norm1_g, w_in, qg, kg, gmat]
    if rope_tabs is not None:
        for t in rope_tabs:
            in_specs.append(pl.BlockSpec((tile, LANES), lambda i: (i % tiles_per_seq, 0)))
            args.append(t)
    aliases = {}
    emit = False
    if kv_out is not None and kv_out != "new":
        emit = "alias"
        for a in kv_out:
            aliases[len(args)] = 5 + len(aliases)
            in_specs.append(pl.BlockSpec(memory_space=pl.ANY))
            args.append(a)
    elif kv_out == "new":
        emit = True
    tok = jax.ShapeDtypeStruct((n, D_ATTN), BF16)
    tok_spec = pl.BlockSpec((tile, D_ATTN), lambda i: (i, 0))
    out_shape = [tok, tok, jax.ShapeDtypeStruct((D_ATTN, n), BF16), tok, tok]
    out_specs = [tok_spec, tok_spec, pl.BlockSpec((D_ATTN, tile), lambda i: (0, i)), tok_spec,
                 tok_spec]
    if emit:
        n_seq = tile // seq_len
        kv_shape = jax.ShapeDtypeStruct((n // seq_len, DEPTH, seq_len, D_ATTN), F32)
        out_shape += [kv_shape] * 2
        if emit == "alias":
            kv_spec = pl.BlockSpec((n_seq, None, seq_len, D_ATTN), lambda i: (i, layer, 0, 0))
        else:
            kv_spec = pl.BlockSpec((n_seq, DEPTH, seq_len, D_ATTN), lambda i: (i, 0, 0, 0))
        out_specs += [kv_spec] * 2
    return pl.pallas_call(
        functools.partial(_proj_kernel, rope=rope_tabs is not None, emit_f32=emit, seq_len=seq_len,
                          layer=layer),
        grid=(n // tile,),
        in_specs=in_specs,
        out_specs=out_specs,
        out_shape=out_shape,
        input_output_aliases=aliases,
        compiler_params=_cparams(1),
        name="proj",
    )(*args)


def _shift_rows(u, prev_row, next_row, row_in_seq, seq_len):
    t = u.shape[0]
    ridx = lax.broadcasted_iota(jnp.int32, (t, 1), 0)
    up = jnp.where(ridx == 0, prev_row, pltpu.roll(u, 1, axis=0))
    dn = jnp.where(ridx == t - 1, next_row, pltpu.roll(u, t - 1, axis=0))
    up = jnp.where(row_in_seq == 0, 0.0, up)
    dn = jnp.where(row_in_seq == seq_len - 1, 0.0, dn)
    return up, dn


def _attn_kernel(*refs, has_cache, cast_ffn, tq, seq_len, layer, lam_init):
    it = iter(refs)
    x_ref, m_ref, q_ref, k_ref, vt_ref = (next(it) for _ in range(5))
    if has_cache:
        kc_ref, vc_ref = next(it), next(it)
    u_ref, up_ref, un_ref, bg_ref = (next(it) for _ in range(4))
    lam_refs = [next(it) for _ in range(4)]
    sg_ref, cw_ref, cg_ref, wo_ref = (next(it) for _ in range(4))
    if cast_ffn:
        wu_ref, wd_ref = next(it), next(it)
    o_ref = next(it)
    if cast_ffn:
        wub_ref, wdb_ref = next(it), next(it)
        wub_ref[...] = wu_ref[...].astype(BF16)
        wdb_ref[...] = wd_ref[...].astype(BF16)

    lq1, lk1, lq2, lk2 = (r[layer:layer + 1, :] for r in lam_refs)
    lam = (jnp.exp(jnp.sum(lq1 * lk1, axis=-1, keepdims=True))
           - jnp.exp(jnp.sum(lq2 * lk2, axis=-1, keepdims=True)) + lam_init)

    nq = min(tq, seq_len)
    if seq_len >= tq:
        groups = [(0, 0, k_ref.shape[0])]
    else:
        groups = [(g * seq_len, g * seq_len, seq_len) for g in range(tq // seq_len)]

    lane = lax.broadcasted_iota(jnp.int32, (1, LANES), 1)
    sub0 = jnp.where(lane < HEAD_DIM, 1.0, 0.0).astype(BF16)
    sub1 = jnp.where(lane < HEAD_DIM, 0.0, 1.0).astype(BF16)
    nt = (((1,), (1,)), ((), ()))
    ones = jnp.ones((BF16_ROWS, groups[0][2]), BF16)
    sgain = jnp.concatenate([sg_ref[...]] * (nq // LANES), axis=1)
    if has_cache:
        nc = kc_ref.shape[0]
        ones_c = jnp.ones((BF16_ROWS, nc), BF16)
        vct = vc_ref[...].T.astype(BF16)

    def scores(job):
        (q0, k0, nk), h = job
        sl = slice(h * LANES, (h + 1) * LANES)
        qh = q_ref[q0:q0 + nq, sl]
        q2 = jnp.concatenate([qh * sub0, qh * sub1], axis=0)
        st = lax.dot_general(k_ref[k0:k0 + nk, sl], q2, nt, preferred_element_type=F32)
        if not has_cache:
            return st, None
        return st, lax.dot_general(kc_ref[:, sl].astype(BF16), q2, nt,
                                   preferred_element_type=F32)

    jobs = [(grp, h) for grp in groups for h in range(N_HEADS)]
    nxt = scores(jobs[0])

    j = pl.program_id(1)
    row_in_seq = (j * tq + lax.broadcasted_iota(jnp.int32, (tq, 1), 0)) % seq_len
    u = u_ref[...].astype(F32)
    up, dn = _shift_rows(u, up_ref[BF16_ROWS - 1:BF16_ROWS, :].astype(F32),
                         un_ref[0:1, :].astype(F32), row_in_seq, seq_len)
    cw = cw_ref[...]
    t = bg_ref[...].astype(F32) * (up * cw[0:1] + u * cw[1:2] + dn * cw[2:3])
    y = ((t * _rms_rows(t)) * cg_ref[...]).astype(BF16)
    y_proj = jnp.dot(y, wo_ref[D_ATTN:, :].astype(BF16), preferred_element_type=F32)

    outs = []
    for n, ((q0, k0, nk), h) in enumerate(jobs):
        sl = slice(h * LANES, (h + 1) * LANES)
        st, sct = nxt
        if n + 1 < len(jobs):
            nxt = scores(jobs[n + 1])
        mx = jnp.max(st, axis=0, keepdims=True)
        if has_cache:
            mx = jnp.maximum(mx, jnp.max(sct, axis=0, keepdims=True))
        e = jnp.exp2(st - mx).astype(BF16)
        oe = jnp.dot(jnp.concatenate([vt_ref[sl, k0:k0 + nk], ones], axis=0), e,
                     preferred_element_type=F32)
        if has_cache:
            ec = jnp.exp2(sct - mx).astype(BF16)
            oe = oe + jnp.dot(jnp.concatenate([vct[sl, :], ones_c], axis=0), ec,
                              preferred_element_type=F32)
        den = oe[V_DIM:V_DIM + 1, :]
        c0 = 1.0 / den[:, :nq]
        c1 = lam / den[:, nq:]
        oh = oe[:V_DIM, :nq] * c0 - oe[:V_DIM, nq:] * c1
        r = lax.rsqrt(jnp.mean(oh * oh, axis=0, keepdims=True) + EPS)
        outs.append(((oh * r) * sgain) * (1.0 - lam_init))
    ot = jnp.concatenate(
        [jnp.concatenate(outs[g * N_HEADS:(g + 1) * N_HEADS], axis=0) for g in range(len(groups))],
        axis=1)
    o = ot.T.astype(BF16)
    mix = y_proj + jnp.dot(o, wo_ref[0:D_ATTN, :].astype(BF16), preferred_element_type=F32)
    g1 = m_ref[...][:, 2 * D_MODEL:3 * D_MODEL]
    o_ref[...] = x_ref[...] + g1 * mix


def _attn(x, mod, layer, row0, mod_span, seq_len, tq, q, k, vt, cache, u, bg, lam_ps, subln_g,
          conv_w, conv_g, w_out, lam_init, ffn_weights=None):
    n = x.shape[0]
    grp_rows = max(seq_len, tq)
    qt = grp_rows // tq
    hb = tq // BF16_ROWS
    n_hblocks = n // BF16_ROWS
    tokmap = lambda b, j: (b * qt + j, 0)
    in_specs = [
        pl.BlockSpec((tq, D_MODEL), tokmap),
        _mod_spec(layer, lambda b, j: row0 + (b * grp_rows) // mod_span),
        pl.BlockSpec((tq, D_ATTN), tokmap),
        pl.BlockSpec((grp_rows, D_ATTN), lambda b, j: (b, 0)),
        pl.BlockSpec((D_ATTN, grp_rows), lambda b, j: (0, b)),
    ]
    args = [x, mod, q, k, vt]
    if cache is not None:
        for c in cache:
            p = c.shape[2]
            in_specs.append(pl.BlockSpec((None, None, p, D_ATTN), lambda b, j: (b, layer, 0, 0)))
            args.append(c)
    in_specs += [
        pl.BlockSpec((tq, D_CONV), tokmap),
        pl.BlockSpec((BF16_ROWS, D_CONV), lambda b, j: (jnp.maximum((b * qt + j) * hb - 1, 0), 0)),
        pl.BlockSpec((BF16_ROWS, D_CONV),
                     lambda b, j: (jnp.minimum((b * qt + j + 1) * hb, n_hblocks - 1), 0)),
        pl.BlockSpec((tq, D_CONV), tokmap),
    ]
    in_specs += [_const_spec((DEPTH, HEAD_DIM))] * 4
    in_specs += [
        _layer_spec((V_DIM, LANES), layer),
        _layer_spec((3, D_CONV), layer),
        _layer_spec((1, D_CONV), layer),
        _layer_spec((D_MODEL, D_MODEL), layer),
    ]
    args += [u, u, u, bg, *lam_ps, subln_g, conv_w, conv_g, w_out]
    out_specs = [pl.BlockSpec((tq, D_MODEL), tokmap)]
    out_shape = [jax.ShapeDtypeStruct(x.shape, F32)]
    if ffn_weights is not None:
        n_steps = n // tq
        for w in ffn_weights:
            rows, cols = w.shape[1:]
            slab = rows // n_steps
            assert slab * n_steps == rows and slab % BF16_ROWS == 0, (rows, n_steps)
            in_specs.append(pl.BlockSpec((None, slab, cols), lambda b, j: (layer, b * qt + j, 0)))
            out_specs.append(pl.BlockSpec((slab, cols), lambda b, j: (b * qt + j, 0)))
            out_shape.append(jax.ShapeDtypeStruct((rows, cols), BF16))
            args.append(w)
    outs = pl.pallas_call(
        functools.partial(_attn_kernel, has_cache=cache is not None,
                          cast_ffn=ffn_weights is not None, tq=tq, seq_len=seq_len, layer=layer,
                          lam_init=lam_init),
        grid=(n // grp_rows, qt),
        in_specs=in_specs,
        out_specs=out_specs,
        out_shape=out_shape,
        compiler_params=_cparams(2),
        name="attn",
    )(*args)
    return outs[0], tuple(outs[1:])


def _ffn_kernel(*refs, tile, seq_len, halo):
    it = iter(refs)
    x_ref = next(it)
    if halo:
        xp_ref, xn_ref = next(it), next(it)
    m_ref, g2_ref, wup_ref, cw_ref, wdn_ref, o_ref, hs_ref, u_ref, g_ref = (
        next(it) for _ in range(9))

    m = m_ref[...]
    sh2 = m[:, 3 * D_MODEL:4 * D_MODEL]
    sc2 = m[:, 4 * D_MODEL:5 * D_MODEL]
    gate = m[:, 5 * D_MODEL:6 * D_MODEL]

    def pre(xv):
        return ((xv * _rms_rows(xv)) * g2_ref[...]) * (1.0 + sc2) + sh2

    x = x_ref[...]
    i = pl.program_id(0)
    base = HALO if halo else 0
    hs_ref[base:base + tile, :] = pre(x).astype(BF16)
    if halo:
        tiles_per_seq = seq_len // tile
        jt = i % tiles_per_seq
        hs_ref[0:HALO, :] = jnp.where(jt == 0, 0.0, pre(xp_ref[...])).astype(BF16)
        hs_ref[HALO + tile:, :] = jnp.where(jt == tiles_per_seq - 1, 0.0,
                                            pre(xn_ref[...])).astype(BF16)
    if halo:
        segs = [(HALO, tile, 0)]
    else:
        segs = [(_U_PAD + s * (seq_len + _U_PAD), seq_len, s * seq_len)
                for s in range(tile // seq_len)]

        @pl.when(i == 0)
        def _():
            u_ref[...] = jnp.zeros_like(u_ref)

    hs = hs_ref[...]
    for c in range(N_FF_CHUNKS):
        slot = c % 2
        cols = (c * FF_CHUNK, D_FF + c * FF_CHUNK)
        for half, c0 in enumerate(cols):
            uu = jnp.dot(hs, wup_ref[:, c0:c0 + FF_CHUNK], preferred_element_type=F32)
            for j in range(_U_COLS):
                col = uu[:, j * LANES:(j + 1) * LANES]
                if halo:
                    u_ref[slot, half * _U_COLS + j] = col
                else:
                    for b0, ln, r0 in segs:
                        u_ref[slot, half * _U_COLS + j, b0:b0 + ln, :] = col[r0:r0 + ln]
        for b0, ln, r0 in segs:
            conv = []
            for half, c0 in enumerate(cols):
                w3 = cw_ref[:, c0:c0 + FF_CHUNK]
                taps = []
                for j in range(_U_COLS):
                    uc = u_ref.at[slot, half * _U_COLS + j]
                    w3j = w3[:, j * LANES:(j + 1) * LANES]
                    taps.append(uc[b0 - 1:b0 - 1 + ln, :] * w3j[0:1]
                                + uc[b0:b0 + ln, :] * w3j[1:2]
                                + uc[b0 + 1:b0 + 1 + ln, :] * w3j[2:3])
                conv.append(jnp.concatenate(taps, axis=1))
            ca, cb = conv
            g_ref[r0:r0 + ln, c * FF_CHUNK:(c + 1) * FF_CHUNK] = (
                (ca * jax.nn.sigmoid(ca)) * cb).astype(BF16)

    gs = g_ref[...]
    for n0 in range(0, D_MODEL, FF_CHUNK):
        d = jnp.dot(gs, wdn_ref[:, n0:n0 + FF_CHUNK], preferred_element_type=F32)
        o_ref[:, n0:n0 + FF_CHUNK] = x[:, n0:n0 + FF_CHUNK] + gate[:, n0:n0 + FF_CHUNK] * d


def _ffn(x, mod, layer, row0, mod_span, seq_len, halo, norm2_g, w_up_b, conv_w, w_down_b):
    n = x.shape[0]
    tile = FFN_TILE
    hb = tile // BF16_ROWS
    n_hblocks = n // BF16_ROWS
    in_specs = [pl.BlockSpec((tile, D_MODEL), lambda i: (i, 0))]
    args = [x]
    if halo:
        in_specs += [
            pl.BlockSpec((HALO, D_MODEL), lambda i: (jnp.maximum(i * hb - 1, 0), 0)),
            pl.BlockSpec((HALO, D_MODEL), lambda i: (jnp.minimum((i + 1) * hb, n_hblocks - 1), 0)),
        ]
        args += [x, x]
    in_specs += [
        _mod_spec(layer, lambda i: row0 + (i * tile) // mod_span),
        _layer_spec((1, D_MODEL), layer),
        _const_spec((D_MODEL, 2 * D_FF)),
        _layer_spec((3, 2 * D_FF), layer),
        _const_spec((D_FF, D_MODEL)),
    ]
    args += [mod, norm2_g, w_up_b, conv_w, w_down_b]
    if halo:
        rows_h = rows_u = tile + 2 * HALO
    else:
        rows_h = tile
        rows_u = _U_PAD + (tile // seq_len) * (seq_len + _U_PAD)
    return pl.pallas_call(
        functools.partial(_ffn_kernel, tile=tile, seq_len=seq_len, halo=halo),
        grid=(n // tile,),
        in_specs=in_specs,
        out_specs=pl.BlockSpec((tile, D_MODEL), lambda i: (i, 0)),
        out_shape=jax.ShapeDtypeStruct(x.shape, F32),
        scratch_shapes=[pltpu.VMEM((rows_h, D_MODEL), BF16),
                        pltpu.VMEM((2, 2 * _U_COLS, rows_u, LANES), F32),
                        pltpu.VMEM((tile, D_FF), BF16)],
        compiler_params=_cparams(1),
        name="ffn",
    )(*args)


def _rope_tables(n_tok):
    rows = n_tok // GRID_W
    row = jnp.repeat(jnp.arange(rows, dtype=F32), GRID_W)
    col = jnp.tile(jnp.arange(GRID_W, dtype=F32), rows)
    n_freq = HEAD_DIM // 4
    inv = ROPE_BASE ** (-jnp.arange(n_freq, dtype=F32) / n_freq)
    ar, ac = row[:, None] * inv[None], col[:, None] * inv[None]
    cr, sr, cc, sc = jnp.cos(ar), jnp.sin(ar), jnp.cos(ac), jnp.sin(ac)
    z = jnp.zeros_like(sr)
    rep = lambda parts: jnp.tile(jnp.concatenate(parts, axis=-1), (1, LANES // HEAD_DIM))
    return rep([cr, cr, cc, cc]), rep([-sr, z, -sc, z]), rep([z, sr, z, sc])


def kernel(x_prompt, x_sample, cache_k, cache_v, c, c_ctx, w_mod, b_mod, norm1_g, w_in, q_norm_g, k_norm_g, lambda_q1, lambda_k1, lambda_q2, lambda_k2, subln_g, conv_w, conv_norm_g, w_out, norm2_g, w_up, ffn_conv_w, w_down):
    batch, seq, _ = x_prompt.shape
    dec_batch, dec_seq, _ = x_sample.shape
    past = cache_k.shape[2]

    cvecs = jnp.zeros((MOD_ROWS, D_MODEL), F32).at[0].set(c_ctx).at[1:1 + dec_batch].set(c)
    mod = _modulation(cvecs, w_mod, b_mod).reshape(DEPTH, MOD_ROWS, 1, N_MOD * D_MODEL)

    grp = jnp.arange(D_ATTN) // HEAD_DIM
    gmat = jnp.where(grp[:, None] == grp[None, :], 1.0 / HEAD_DIM, 0.0).astype(BF16)
    row3 = lambda g: g.reshape(DEPTH, 1, -1)
    qg = jnp.tile(row3(q_norm_g), (1, 1, D_ATTN // HEAD_DIM))
    kg = jnp.tile(row3(k_norm_g), (1, 1, D_ATTN // HEAD_DIM))
    n1, n2, cng = row3(norm1_g), row3(norm2_g), row3(conv_norm_g)
    sg = jnp.broadcast_to(subln_g[:, :, None], (DEPTH, V_DIM, LANES))
    lam_ps = (lambda_q1, lambda_k1, lambda_q2, lambda_k2)
    rope_tabs = _rope_tables(dec_seq)
    cache = (cache_k.reshape(dec_batch, DEPTH, past, D_ATTN),
             cache_v.reshape(dec_batch, DEPTH, past, D_ATTN))

    ffn_w = {}

    def run(x, row0, mod_span, seq_len, tq, ffn_halo, rope, cache, first_stream):
        kv = "new" if first_stream else None
        for l in range(DEPTH):
            lam_init = 0.8 - 0.6 * math.exp(-0.3 * l)
            outs = _proj(x, mod, l, row0, mod_span, seq_len, n1, w_in, qg, kg, gmat, rope, kv)
            q, k, v, u, bg = outs[:5]
            if first_stream:
                kv = tuple(outs[5:])
            x, cast = _attn(x, mod, l, row0, mod_span, seq_len, tq, q, k, v, cache, u, bg, lam_ps,
                            sg, conv_w, cng, w_out, lam_init,
                            ffn_weights=(w_up, w_down) if first_stream else None)
            if first_stream:
                ffn_w[l] = cast
            x = _ffn(x, mod, l, row0, mod_span, seq_len, ffn_halo, n2, ffn_w[l][0], ffn_conv_w,
                     ffn_w[l][1])
        return x, kv

    xp, (new_k, new_v) = run(x_prompt.reshape(batch * seq, D_MODEL), 0, batch * seq, seq,
                             CTX_Q_TILE, False, None, None, True)
    xs, _ = run(x_sample.reshape(dec_batch * dec_seq, D_MODEL), 1, dec_seq, dec_seq, Q_TILE, True,
                rope_tabs, cache, False)

    return (xp.reshape(batch, seq, D_MODEL), xs.reshape(dec_batch, dec_seq, D_MODEL),
            new_k.reshape(batch, DEPTH, seq, N_HEADS, 2, HEAD_DIM),
            new_v.reshape(batch, DEPTH, seq, N_HEADS, V_DIM))
```

```python
import functools
import math

import jax
import jax.numpy as jnp
from jax import lax
from jax.experimental import pallas as pl
from jax.experimental.pallas import tpu as pltpu

D_MODEL = 1024
DEPTH = 2
GRID_W = 64
D_ATTN = 512
D_CONV = 512
N_HEADS = 4
HEAD_DIM = 64
V_DIM = 128
D_FF = 2816
ROPE_BASE = 10000.0
EPS = 1e-6
N_MOD = 6
D_IN = 3 * D_ATTN + 3 * D_CONV

F32 = jnp.float32
BF16 = jnp.bfloat16

LANES = 128
BF16_ROWS = 16
F32_ROWS = 8
MOD_ROWS = F32_ROWS
_U_PAD = F32_ROWS
VMEM_LIMIT = 60 * 1024 * 1024

FF_CHUNK = 256
N_FF_CHUNKS = D_FF // FF_CHUNK
_U_COLS = FF_CHUNK // LANES
HALO = BF16_ROWS
TOKEN_TILE = 512
FFN_TILE = 512
Q_TILE = 512
CTX_Q_TILE = 1024


def _cparams(n_axes, flags=None):
    return pltpu.CompilerParams(
        dimension_semantics=("arbitrary",) * n_axes, vmem_limit_bytes=VMEM_LIMIT, flags=flags)


def _layer_spec(shape, layer):
    nd = len(shape)
    return pl.BlockSpec((None,) + tuple(shape), lambda *_: (layer,) + (0,) * nd,
                        pipeline_mode=pl.Buffered(1))


def _const_spec(shape):
    nd = len(shape)
    return pl.BlockSpec(shape, lambda *_: (0,) * nd, pipeline_mode=pl.Buffered(1))


def _mod_spec(layer, row_of):
    return pl.BlockSpec((None, None, 1, N_MOD * D_MODEL),
                        lambda *idx: (layer, row_of(*idx), 0, 0))


def _rms_rows(x):
    return lax.rsqrt(jnp.mean(x * x, axis=-1, keepdims=True) + EPS)


def _mod_kernel(cv_ref, w_ref, b_ref, o_ref):
    cv = cv_ref[...]
    s = (cv * jax.nn.sigmoid(cv)).astype(BF16)
    o_ref[...] = jnp.dot(s, w_ref[...].astype(BF16), preferred_element_type=F32) + b_ref[...]


def _modulation(cvecs, w_mod, b_mod):
    tn = 1536
    n_out = N_MOD * D_MODEL
    return pl.pallas_call(
        _mod_kernel,
        grid=(DEPTH, n_out // tn),
        in_specs=[
            pl.BlockSpec((MOD_ROWS, D_MODEL), lambda l, j: (0, 0)),
            pl.BlockSpec((None, D_MODEL, tn), lambda l, j: (l, 0, j)),
            pl.BlockSpec((None, 1, tn), lambda l, j: (l, 0, j)),
        ],
        out_specs=pl.BlockSpec((None, MOD_ROWS, tn), lambda l, j: (l, 0, j)),
        out_shape=jax.ShapeDtypeStruct((DEPTH, MOD_ROWS, n_out), F32),
        compiler_params=_cparams(2),
        name="adaln_mod",
    )(cvecs, w_mod, b_mod.reshape(DEPTH, 1, n_out))


def _group_norm_gain(z, gmat_ref, gain):
    ms = jnp.dot((z * z).astype(BF16), gmat_ref[...], preferred_element_type=F32)
    return z * lax.rsqrt(ms + EPS) * gain


def _rope(z, c_ref, sa_ref, sb_ref):
    c, sa, sb = c_ref[...], sa_ref[...], sb_ref[...]
    cols = []
    for j in range(z.shape[1] // LANES):
        zj = z[:, j * LANES:(j + 1) * LANES]
        hi = pltpu.roll(zj, LANES - HEAD_DIM // 4, axis=1)
        lo = pltpu.roll(zj, HEAD_DIM // 4, axis=1)
        cols.append(zj * c + hi * sa + lo * sb)
    return jnp.concatenate(cols, axis=1)


def _proj_kernel(*refs, rope, emit_f32, cast_ffn, seq_len, layer):
    it = iter(refs)
    x_ref, m_ref, g1_ref, w_ref, qg_ref, kg_ref, gmat_ref = (next(it) for _ in range(7))
    if rope:
        c_ref, sa_ref, sb_ref = (next(it) for _ in range(3))
    if emit_f32 == "alias":
        next(it), next(it)
    if cast_ffn:
        wu_ref, wd_ref = next(it), next(it)
    q_ref, k_ref, v_ref, u_ref, bg_ref = (next(it) for _ in range(5))
    if emit_f32:
        kf_ref, vf_ref = next(it), next(it)
    if cast_ffn:
        wub_ref, wdb_ref = next(it), next(it)
        wub_ref[...] = wu_ref[...].astype(BF16)
        wdb_ref[...] = wd_ref[...].astype(BF16)

    x = x_ref[...]
    m = m_ref[...]
    sh1 = m[:, 0:D_MODEL]
    sc1 = m[:, D_MODEL:2 * D_MODEL]
    h = ((x * _rms_rows(x)) * g1_ref[...]) * (1.0 + sc1) + sh1
    hb = h.astype(BF16)

    def col(i0, n):
        return jnp.dot(hb, w_ref[:, i0:i0 + n].astype(BF16), preferred_element_type=F32)

    q = _group_norm_gain(col(0, D_ATTN), gmat_ref, qg_ref[...])
    k = _group_norm_gain(col(D_ATTN, D_ATTN), gmat_ref, kg_ref[...])
    if rope:
        q = _rope(q, c_ref, sa_ref, sb_ref)
        k = _rope(k, c_ref, sa_ref, sb_ref)
    v = col(2 * D_ATTN, D_ATTN)
    q_ref[...] = (q * (HEAD_DIM ** -0.5 * math.log2(math.e))).astype(BF16)
    k_ref[...] = k.astype(BF16)
    v_ref[...] = v.T.astype(BF16)
    if emit_f32 == "alias":
        n_seq = x.shape[0] // seq_len
        kf_ref[...] = k.reshape(n_seq, seq_len, D_ATTN)
        vf_ref[...] = v.reshape(n_seq, seq_len, D_ATTN)
    elif emit_f32:
        n_seq = x.shape[0] // seq_len
        zero = jnp.zeros((n_seq, seq_len, D_ATTN), F32)
        for l in range(DEPTH):
            kf_ref[:, l] = k.reshape(n_seq, seq_len, D_ATTN) if l == layer else zero
            vf_ref[:, l] = v.reshape(n_seq, seq_len, D_ATTN) if l == layer else zero
    bg_ref[...] = col(3 * D_ATTN, D_CONV).astype(BF16)
    cg = col(3 * D_ATTN + D_CONV, D_CONV)
    xc = col(3 * D_ATTN + 2 * D_CONV, D_CONV)
    u_ref[...] = (cg * xc).astype(BF16)


def _proj(x, mod, layer, row0, mod_span, seq_len, norm1_g, w_in, qg, kg, gmat, rope_tabs, kv_out,
          ffn_weights=None):
    n = x.shape[0]
    tile = TOKEN_TILE
    tiles_per_seq = max(seq_len // tile, 1)
    in_specs = [
        pl.BlockSpec((tile, D_MODEL), lambda i: (i, 0)),
        _mod_spec(layer, lambda i: row0 + (i * tile) // mod_span),
        _layer_spec((1, D_MODEL), layer),
        _layer_spec((D_MODEL, D_IN), layer),
        _layer_spec((1, D_ATTN), layer),
        _layer_spec((1, D_ATTN), layer),
        _const_spec((D_ATTN, D_ATTN)),
    ]
    args = [x, mod, norm1_g, w_in, qg, kg, gmat]
    if rope_tabs is not None:
        for t in rope_tabs:
            in_specs.append(pl.BlockSpec((tile, LANES), lambda i: (i % tiles_per_seq, 0)))
            args.append(t)
    aliases = {}
    emit = False
    if kv_out is not None and kv_out != "new":
        emit = "alias"
        for a in kv_out:
            aliases[len(args)] = 5 + len(aliases)
            in_specs.append(pl.BlockSpec(memory_space=pl.ANY))
            args.append(a)
    elif kv_out == "new":
        emit = True
    tok = jax.ShapeDtypeStruct((n, D_ATTN), BF16)
    tok_spec = pl.BlockSpec((tile, D_ATTN), lambda i: (i, 0))
    out_shape = [tok, tok, jax.ShapeDtypeStruct((D_ATTN, n), BF16), tok, tok]
    out_specs = [tok_spec, tok_spec, pl.BlockSpec((D_ATTN, tile), lambda i: (0, i)), tok_spec,
                 tok_spec]
    if emit:
        n_seq = tile // seq_len
        kv_shape = jax.ShapeDtypeStruct((n // seq_len, DEPTH, seq_len, D_ATTN), F32)
        out_shape += [kv_shape] * 2
        if emit == "alias":
            kv_spec = pl.BlockSpec((n_seq, None, seq_len, D_ATTN), lambda i: (i, layer, 0, 0))
        else:
            kv_spec = pl.BlockSpec((n_seq, DEPTH, seq_len, D_ATTN), lambda i: (i, 0, 0, 0))
        out_specs += [kv_spec] * 2
    if ffn_weights is not None:
        n_steps = n // tile
        for w in ffn_weights:
            rows, cols = w.shape[1:]
            slab = rows // n_steps
            assert slab * n_steps == rows and slab % BF16_ROWS == 0, (rows, n_steps)
            in_specs.append(pl.BlockSpec((None, slab, cols), lambda i: (layer, i, 0)))
            out_specs.append(pl.BlockSpec((slab, cols), lambda i: (i, 0)))
            out_shape.append(jax.ShapeDtypeStruct((rows, cols), BF16))
            args.append(w)
    return pl.pallas_call(
        functools.partial(_proj_kernel, rope=rope_tabs is not None, emit_f32=emit,
                          cast_ffn=ffn_weights is not None, seq_len=seq_len, layer=layer),
        grid=(n // tile,),
        in_specs=in_specs,
        out_specs=out_specs,
        out_shape=out_shape,
        input_output_aliases=aliases,
        compiler_params=_cparams(1),
        name="proj",
    )(*args)


def _shift_rows(u, prev_row, next_row, row_in_seq, seq_len):
    t = u.shape[0]
    ridx = lax.broadcasted_iota(jnp.int32, (t, 1), 0)
    up = jnp.where(ridx == 0, prev_row, pltpu.roll(u, 1, axis=0))
    dn = jnp.where(ridx == t - 1, next_row, pltpu.roll(u, t - 1, axis=0))
    up = jnp.where(row_in_seq == 0, 0.0, up)
    dn = jnp.where(row_in_seq == seq_len - 1, 0.0, dn)
    return up, dn


def _attn_kernel(*refs, has_cache, tq, seq_len, layer, lam_init):
    it = iter(refs)
    x_ref, m_ref, q_ref, k_ref, vt_ref = (next(it) for _ in range(5))
    if has_cache:
        kc_ref, vc_ref = next(it), next(it)
    u_ref, up_ref, un_ref, bg_ref = (next(it) for _ in range(4))
    lam_refs = [next(it) for _ in range(4)]
    sg_ref, cw_ref, cg_ref, wo_ref = (next(it) for _ in range(4))
    o_ref = next(it)

    lq1, lk1, lq2, lk2 = (r[layer:layer + 1, :] for r in lam_refs)
    lam = (jnp.exp(jnp.sum(lq1 * lk1, axis=-1, keepdims=True))
           - jnp.exp(jnp.sum(lq2 * lk2, axis=-1, keepdims=True)) + lam_init)

    nq = min(tq, seq_len)
    if seq_len >= tq:
        groups = [(0, 0, k_ref.shape[0])]
    else:
        groups = [(g * seq_len, g * seq_len, seq_len) for g in range(tq // seq_len)]

    lane = lax.broadcasted_iota(jnp.int32, (1, LANES), 1)
    sub0 = jnp.where(lane < HEAD_DIM, 1.0, 0.0).astype(BF16)
    sub1 = jnp.where(lane < HEAD_DIM, 0.0, 1.0).astype(BF16)
    nt = (((1,), (1,)), ((), ()))
    ones = jnp.ones((BF16_ROWS, groups[0][2]), BF16)
    sgain = jnp.concatenate([sg_ref[...]] * (nq // LANES), axis=1)
    if has_cache:
        nc = kc_ref.shape[0]
        ones_c = jnp.ones((BF16_ROWS, nc), BF16)
        vct = vc_ref[...].T.astype(BF16)

    def scores(job):
        (q0, k0, nk), h = job
        sl = slice(h * LANES, (h + 1) * LANES)
        qh = q_ref[q0:q0 + nq, sl]
        q2 = jnp.concatenate([qh * sub0, qh * sub1], axis=0)
        st = lax.dot_general(k_ref[k0:k0 + nk, sl], q2, nt, preferred_element_type=F32)
        if not has_cache:
            return st, None
        return st, lax.dot_general(kc_ref[:, sl].astype(BF16), q2, nt,
                                   preferred_element_type=F32)

    jobs = [(grp, h) for grp in groups for h in range(N_HEADS)]
    nxt = scores(jobs[0])

    j = pl.program_id(1)
    row_in_seq = (j * tq + lax.broadcasted_iota(jnp.int32, (tq, 1), 0)) % seq_len
    u = u_ref[...].astype(F32)
    up, dn = _shift_rows(u, up_ref[BF16_ROWS - 1:BF16_ROWS, :].astype(F32),
                         un_ref[0:1, :].astype(F32), row_in_seq, seq_len)
    cw = cw_ref[...]
    t = bg_ref[...].astype(F32) * (up * cw[0:1] + u * cw[1:2] + dn * cw[2:3])
    y = ((t * _rms_rows(t)) * cg_ref[...]).astype(BF16)
    y_proj = jnp.dot(y, wo_ref[D_ATTN:, :].astype(BF16), preferred_element_type=F32)

    outs = []
    for n, ((q0, k0, nk), h) in enumerate(jobs):
        sl = slice(h * LANES, (h + 1) * LANES)
        st, sct = nxt
        if n + 1 < len(jobs):
            nxt = scores(jobs[n + 1])
        mx = jnp.max(st, axis=0, keepdims=True)
        if has_cache:
            mx = jnp.maximum(mx, jnp.max(sct, axis=0, keepdims=True))
        e = jnp.exp2(st - mx).astype(BF16)
        oe = jnp.dot(jnp.concatenate([vt_ref[sl, k0:k0 + nk], ones], axis=0), e,
                     preferred_element_type=F32)
        if has_cache:
            ec = jnp.exp2(sct - mx).astype(BF16)
            oe = oe + jnp.dot(jnp.concatenate([vct[sl, :], ones_c], axis=0), ec,
                              preferred_element_type=F32)
        den = oe[V_DIM:V_DIM + 1, :]
        c0 = 1.0 / den[:, :nq]
        c1 = lam / den[:, nq:]
        oh = oe[:V_DIM, :nq] * c0 - oe[:V_DIM, nq:] * c1
        r = lax.rsqrt(jnp.mean(oh * oh, axis=0, keepdims=True) + EPS)
        outs.append(((oh * r) * sgain) * (1.0 - lam_init))
    ot = jnp.concatenate(
        [jnp.concatenate(outs[g * N_HEADS:(g + 1) * N_HEADS], axis=0) for g in range(len(groups))],
        axis=1)
    o = ot.T.astype(BF16)
    mix = y_proj + jnp.dot(o, wo_ref[0:D_ATTN, :].astype(BF16), preferred_element_type=F32)
    g1 = m_ref[...][:, 2 * D_MODEL:3 * D_MODEL]
    o_ref[...] = x_ref[...] + g1 * mix


def _attn(x, mod, layer, row0, mod_span, seq_len, tq, q, k, vt, cache, u, bg, lam_ps, subln_g,
          conv_w, conv_g, w_out, lam_init):
    n = x.shape[0]
    grp_rows = max(seq_len, tq)
    qt = grp_rows // tq
    hb = tq // BF16_ROWS
    n_hblocks = n // BF16_ROWS
    tokmap = lambda b, j: (b * qt + j, 0)
    in_specs = [
        pl.BlockSpec((tq, D_MODEL), tokmap),
        _mod_spec(layer, lambda b, j: row0 + (b * grp_rows) // mod_span),
        pl.BlockSpec((tq, D_ATTN), tokmap),
        pl.BlockSpec((grp_rows, D_ATTN), lambda b, j: (b, 0)),
        pl.BlockSpec((D_ATTN, grp_rows), lambda b, j: (0, b)),
    ]
    args = [x, mod, q, k, vt]
    if cache is not None:
        for c in cache:
            p = c.shape[2]
            in_specs.append(pl.BlockSpec((None, None, p, D_ATTN), lambda b, j: (b, layer, 0, 0)))
            args.append(c)
    in_specs += [
        pl.BlockSpec((tq, D_CONV), tokmap),
        pl.BlockSpec((BF16_ROWS, D_CONV), lambda b, j: (jnp.maximum((b * qt + j) * hb - 1, 0), 0)),
        pl.BlockSpec((BF16_ROWS, D_CONV),
                     lambda b, j: (jnp.minimum((b * qt + j + 1) * hb, n_hblocks - 1), 0)),
        pl.BlockSpec((tq, D_CONV), tokmap),
    ]
    in_specs += [_const_spec((DEPTH, HEAD_DIM))] * 4
    in_specs += [
        _layer_spec((V_DIM, LANES), layer),
        _layer_spec((3, D_CONV), layer),
        _layer_spec((1, D_CONV), layer),
        _layer_spec((D_MODEL, D_MODEL), layer),
    ]
    args += [u, u, u, bg, *lam_ps, subln_g, conv_w, conv_g, w_out]
    return pl.pallas_call(
        functools.partial(_attn_kernel, has_cache=cache is not None, tq=tq, seq_len=seq_len,
                          layer=layer, lam_init=lam_init),
        grid=(n // grp_rows, qt),
        in_specs=in_specs,
        out_specs=pl.BlockSpec((tq, D_MODEL), tokmap),
        out_shape=jax.ShapeDtypeStruct(x.shape, F32),
        compiler_params=_cparams(2),
        name="attn",
    )(*args)


def _ffn_kernel(*refs, tile, seq_len, halo):
    it = iter(refs)
    x_ref = next(it)
    if halo:
        xp_ref, xn_ref = next(it), next(it)
    m_ref, g2_ref, wup_ref, cw_ref, wdn_ref, o_ref, hs_ref, u_ref, g_ref = (
        next(it) for _ in range(9))

    m = m_ref[...]
    sh2 = m[:, 3 * D_MODEL:4 * D_MODEL]
    sc2 = m[:, 4 * D_MODEL:5 * D_MODEL]
    gate = m[:, 5 * D_MODEL:6 * D_MODEL]

    def pre(xv):
        return ((xv * _rms_rows(xv)) * g2_ref[...]) * (1.0 + sc2) + sh2

    x = x_ref[...]
    i = pl.program_id(0)
    base = HALO if halo else 0
    hs_ref[base:base + tile, :] = pre(x).astype(BF16)
    if halo:
        tiles_per_seq = seq_len // tile
        jt = i % tiles_per_seq
        hs_ref[0:HALO, :] = jnp.where(jt == 0, 0.0, pre(xp_ref[...])).astype(BF16)
        hs_ref[HALO + tile:, :] = jnp.where(jt == tiles_per_seq - 1, 0.0,
                                            pre(xn_ref[...])).astype(BF16)
    if halo:
        segs = [(HALO, tile, 0)]
    else:
        segs = [(_U_PAD + s * (seq_len + _U_PAD), seq_len, s * seq_len)
                for s in range(tile // seq_len)]

        @pl.when(i == 0)
        def _():
            u_ref[...] = jnp.zeros_like(u_ref)

    hs = hs_ref[...]
    for c in range(N_FF_CHUNKS):
        slot = c % 2
        cols = (c * FF_CHUNK, D_FF + c * FF_CHUNK)
        for half, c0 in enumerate(cols):
            uu = jnp.dot(hs, wup_ref[:, c0:c0 + FF_CHUNK], preferred_element_type=F32)
            for j in range(_U_COLS):
                col = uu[:, j * LANES:(j + 1) * LANES]
                if halo:
                    u_ref[slot, half * _U_COLS + j] = col
                else:
                    for b0, ln, r0 in segs:
                        u_ref[slot, half * _U_COLS + j, b0:b0 + ln, :] = col[r0:r0 + ln]
        for b0, ln, r0 in segs:
            conv = []
            for half, c0 in enumerate(cols):
                w3 = cw_ref[:, c0:c0 + FF_CHUNK]
                taps = []
                for j in range(_U_COLS):
                    uc = u_ref.at[slot, half * _U_COLS + j]
                    w3j = w3[:, j * LANES:(j + 1) * LANES]
                    taps.append(uc[b0 - 1:b0 - 1 + ln, :] * w3j[0:1]
                                + uc[b0:b0 + ln, :] * w3j[1:2]
                                + uc[b0 + 1:b0 + 1 + ln, :] * w3j[2:3])
                conv.append(jnp.concatenate(taps, axis=1))
            ca, cb = conv
            g_ref[r0:r0 + ln, c * FF_CHUNK:(c + 1) * FF_CHUNK] = (
                (ca * jax.nn.sigmoid(ca)) * cb).astype(BF16)

    gs = g_ref[...]
    for n0 in range(0, D_MODEL, FF_CHUNK):
        d = jnp.dot(gs, wdn_ref[:, n0:n0 + FF_CHUNK], preferred_element_type=F32)
        o_ref[:, n0:n0 + FF_CHUNK] = x[:, n0:n0 + FF_CHUNK] + gate[:, n0:n0 + FF_CHUNK] * d


def _ffn(x, mod, layer, row0, mod_span, seq_len, halo, norm2_g, w_up_b, conv_w, w_down_b):
    n = x.shape[0]
    tile = FFN_TILE
    hb = tile // BF16_ROWS
    n_hblocks = n // BF16_ROWS
    in_specs = [pl.BlockSpec((tile, D_MODEL), lambda i: (i, 0))]
    args = [x]
    if halo:
        in_specs += [
            pl.BlockSpec((HALO, D_MODEL), lambda i: (jnp.maximum(i * hb - 1, 0), 0)),
            pl.BlockSpec((HALO, D_MODEL), lambda i: (jnp.minimum((i + 1) * hb, n_hblocks - 1), 0)),
        ]
        args += [x, x]
    in_specs += [
        _mod_spec(layer, lambda i: row0 + (i * tile) // mod_span),
        _layer_spec((1, D_MODEL), layer),
        _const_spec((D_MODEL, 2 * D_FF)),
        _layer_spec((3, 2 * D_FF), layer),
        _const_spec((D_FF, D_MODEL)),
    ]
    args += [mod, norm2_g, w_up_b, conv_w, w_down_b]
    if halo:
        rows_h = rows_u = tile + 2 * HALO
    else:
        rows_h = tile
        rows_u = _U_PAD + (tile // seq_len) * (seq_len + _U_PAD)
    return pl.pallas_call(
        functools.partial(_ffn_kernel, tile=tile, seq_len=seq_len, halo=halo),
        grid=(n // tile,),
        in_specs=in_specs,
        out_specs=pl.BlockSpec((tile, D_MODEL), lambda i: (i, 0)),
        out_shape=jax.ShapeDtypeStruct(x.shape, F32),
        scratch_shapes=[pltpu.VMEM((rows_h, D_MODEL), BF16),
                        pltpu.VMEM((2, 2 * _U_COLS, rows_u, LANES), F32),
                        pltpu.VMEM((tile, D_FF), BF16)],
        compiler_params=_cparams(1),
        name="ffn",
    )(*args)


def _rope_tables(n_tok):
    rows = n_tok // GRID_W
    row = jnp.repeat(jnp.arange(rows, dtype=F32), GRID_W)
    col = jnp.tile(jnp.arange(GRID_W, dtype=F32), rows)
    n_freq = HEAD_DIM // 4
    inv = ROPE_BASE ** (-jnp.arange(n_freq, dtype=F32) / n_freq)
    ar, ac = row[:, None] * inv[None], col[:, None] * inv[None]
    cr, sr, cc, sc = jnp.cos(ar), jnp.sin(ar), jnp.cos(ac), jnp.sin(ac)
    z = jnp.zeros_like(sr)
    rep = lambda parts: jnp.tile(jnp.concatenate(parts, axis=-1), (1, LANES // HEAD_DIM))
    return rep([cr, cr, cc, cc]), rep([-sr, z, -sc, z]), rep([z, sr, z, sc])


def kernel(x_prompt, x_sample, cache_k, cache_v, c, c_ctx, w_mod, b_mod, norm1_g, w_in, q_norm_g, k_norm_g, lambda_q1, lambda_k1, lambda_q2, lambda_k2, subln_g, conv_w, conv_norm_g, w_out, norm2_g, w_up, ffn_conv_w, w_down):
    batch, seq, _ = x_prompt.shape
    dec_batch, dec_seq, _ = x_sample.shape
    past = cache_k.shape[2]

    cvecs = jnp.zeros((MOD_ROWS, D_MODEL), F32).at[0].set(c_ctx).at[1:1 + dec_batch].set(c)
    mod = _modulation(cvecs, w_mod, b_mod).reshape(DEPTH, MOD_ROWS, 1, N_MOD * D_MODEL)

    grp = jnp.arange(D_ATTN) // HEAD_DIM
    gmat = jnp.where(grp[:, None] == grp[None, :], 1.0 / HEAD_DIM, 0.0).astype(BF16)
    row3 = lambda g: g.reshape(DEPTH, 1, -1)
    qg = jnp.tile(row3(q_norm_g), (1, 1, D_ATTN // HEAD_DIM))
    kg = jnp.tile(row3(k_norm_g), (1, 1, D_ATTN // HEAD_DIM))
    n1, n2, cng = row3(norm1_g), row3(norm2_g), row3(conv_norm_g)
    sg = jnp.broadcast_to(subln_g[:, :, None], (DEPTH, V_DIM, LANES))
    lam_ps = (lambda_q1, lambda_k1, lambda_q2, lambda_k2)
    rope_tabs = _rope_tables(dec_seq)
    cache = (cache_k.reshape(dec_batch, DEPTH, past, D_ATTN),
             cache_v.reshape(dec_batch, DEPTH, past, D_ATTN))

    ffn_w = {}

    def run(x, row0, mod_span, seq_len, tq, ffn_halo, rope, cache, first_stream):
        kv = "new" if first_stream else None
        for l in range(DEPTH):
            lam_init = 0.8 - 0.6 * math.exp(-0.3 * l)
            outs = _proj(x, mod, l, row0, mod_span, seq_len, n1, w_in, qg, kg, gmat, rope, kv,
                         ffn_weights=(w_up, w_down) if first_stream else None)
            q, k, v, u, bg = outs[:5]
            if first_stream:
                kv = tuple(outs[5:7])
                ffn_w[l] = tuple(outs[7:9])
            x = _attn(x, mod, l, row0, mod_span, seq_len, tq, q, k, v, cache, u, bg, lam_ps, sg,
                      conv_w, cng, w_out, lam_init)
            x = _ffn(x, mod, l, row0, mod_span, seq_len, ffn_halo, n2, ffn_w[l][0], ffn_conv_w,
                     ffn_w[l][1])
        return x, kv

    xp, (new_k, new_v) = run(x_prompt.reshape(batch * seq, D_MODEL), 0, batch * seq, seq,
                             CTX_Q_TILE, False, None, None, True)
    xs, _ = run(x_sample.reshape(dec_batch * dec_seq, D_MODEL), 1, dec_seq, dec_seq, Q_TILE, True,
                rope_tabs, cache, False)

    return (xp.reshape(batch, seq, D_MODEL), xs.reshape(dec_batch, dec_seq, D_MODEL),
            new_k.reshape(batch, DEPTH, seq, N_HEADS, 2, HEAD_DIM),
            new_v.reshape(batch, DEPTH, seq, N_HEADS, V_DIM))
```

```python
import functools
import math

import jax
import jax.numpy as jnp
from jax import lax
from jax.experimental import pallas as pl
from jax.experimental.pallas import tpu as pltpu

D_MODEL = 1024
DEPTH = 2
GRID_W = 64
D_ATTN = 512
D_CONV = 512
N_HEADS = 4
HEAD_DIM = 64
V_DIM = 128
D_FF = 2816
ROPE_BASE = 10000.0
EPS = 1e-6
N_MOD = 6
D_IN = 3 * D_ATTN + 3 * D_CONV

F32 = jnp.float32
BF16 = jnp.bfloat16

LANES = 128
BF16_ROWS = 16
F32_ROWS = 8
MOD_ROWS = F32_ROWS
_U_PAD = F32_ROWS
VMEM_LIMIT = 60 * 1024 * 1024

FF_CHUNK = 256
N_FF_CHUNKS = D_FF // FF_CHUNK
_U_COLS = FF_CHUNK // LANES
_DOWN_PARTS = (8, N_FF_CHUNKS)
HALO = BF16_ROWS
MOD_TILE = 1536
TOKEN_TILE = 512
FFN_TILE = 512
Q_TILE = 512
CTX_Q_TILE = 512


def _cparams(n_axes, flags=None):
    return pltpu.CompilerParams(
        dimension_semantics=("arbitrary",) * n_axes, vmem_limit_bytes=VMEM_LIMIT, flags=flags)


def _layer_spec(shape, layer):
    nd = len(shape)
    return pl.BlockSpec((None,) + tuple(shape), lambda *_: (layer,) + (0,) * nd,
                        pipeline_mode=pl.Buffered(1))


def _const_spec(shape):
    nd = len(shape)
    return pl.BlockSpec(shape, lambda *_: (0,) * nd, pipeline_mode=pl.Buffered(1))


def _mod_spec(layer, row_of):
    return pl.BlockSpec((None, None, 1, N_MOD * D_MODEL),
                        lambda *idx: (layer, row_of(*idx), 0, 0))


def _rms_rows(x):
    return lax.rsqrt(jnp.mean(x * x, axis=-1, keepdims=True) + EPS)


def _mod_kernel(cv_ref, w_ref, b_ref, o_ref):
    cv = cv_ref[...]
    s = (cv * jax.nn.sigmoid(cv)).astype(BF16)
    o_ref[...] = jnp.dot(s, w_ref[...].astype(BF16), preferred_element_type=F32) + b_ref[...]


def _modulation(cvecs, w_mod, b_mod):
    tn = MOD_TILE
    n_out = N_MOD * D_MODEL
    return pl.pallas_call(
        _mod_kernel,
        grid=(DEPTH, n_out // tn),
        in_specs=[
            pl.BlockSpec((MOD_ROWS, D_MODEL), lambda l, j: (0, 0)),
            pl.BlockSpec((None, D_MODEL, tn), lambda l, j: (l, 0, j)),
            pl.BlockSpec((None, 1, tn), lambda l, j: (l, 0, j)),
        ],
        out_specs=pl.BlockSpec((None, MOD_ROWS, tn), lambda l, j: (l, 0, j)),
        out_shape=jax.ShapeDtypeStruct((DEPTH, MOD_ROWS, n_out), F32),
        compiler_params=_cparams(2),
        name="adaln_mod",
    )(cvecs, w_mod, b_mod.reshape(DEPTH, 1, n_out))


def _group_norm_gain(z, gmat_ref, gain):
    ms = jnp.dot((z * z).astype(BF16), gmat_ref[...], preferred_element_type=F32)
    return z * lax.rsqrt(ms + EPS) * gain


def _rope(z, c_ref, sa_ref, sb_ref):
    c, sa, sb = c_ref[...], sa_ref[...], sb_ref[...]
    cols = []
    for j in range(z.shape[1] // LANES):
        zj = z[:, j * LANES:(j + 1) * LANES]
        hi = pltpu.roll(zj, LANES - HEAD_DIM // 4, axis=1)
        lo = pltpu.roll(zj, HEAD_DIM // 4, axis=1)
        cols.append(zj * c + hi * sa + lo * sb)
    return jnp.concatenate(cols, axis=1)


def _proj_kernel(*refs, rope, emit_f32, seq_len, layer):
    it = iter(refs)
    x_ref, m_ref, g1_ref, w_ref, qg_ref, kg_ref, gmat_ref = (next(it) for _ in range(7))
    if rope:
        c_ref, sa_ref, sb_ref = (next(it) for _ in range(3))
    if emit_f32 == "alias":
        next(it), next(it)
    q_ref, k_ref, v_ref, u_ref, bg_ref = (next(it) for _ in range(5))
    if emit_f32:
        kf_ref, vf_ref = next(it), next(it)

    x = x_ref[...]
    m = m_ref[...]
    sh1 = m[:, 0:D_MODEL]
    sc1 = m[:, D_MODEL:2 * D_MODEL]
    h = ((x * _rms_rows(x)) * g1_ref[...]) * (1.0 + sc1) + sh1
    hb = h.astype(BF16)

    def col(i0, n):
        return jnp.dot(hb, w_ref[:, i0:i0 + n].astype(BF16), preferred_element_type=F32)

    q = _group_norm_gain(col(0, D_ATTN), gmat_ref, qg_ref[...])
    k = _group_norm_gain(col(D_ATTN, D_ATTN), gmat_ref, kg_ref[...])
    if rope:
        q = _rope(q, c_ref, sa_ref, sb_ref)
        k = _rope(k, c_ref, sa_ref, sb_ref)
    v = col(2 * D_ATTN, D_ATTN)
    q_ref[...] = (q * (HEAD_DIM ** -0.5 * math.log2(math.e))).astype(BF16)
    k_ref[...] = k.astype(BF16)
    v_ref[...] = v.T.astype(BF16)
    if emit_f32 == "alias":
        n_seq = x.shape[0] // seq_len
        kf_ref[...] = k.reshape(n_seq, seq_len, D_ATTN)
        vf_ref[...] = v.reshape(n_seq, seq_len, D_ATTN)
    elif emit_f32:
        n_seq = x.shape[0] // seq_len
        zero = jnp.zeros((n_seq, seq_len, D_ATTN), F32)
        for l in range(DEPTH):
            kf_ref[:, l] = k.reshape(n_seq, seq_len, D_ATTN) if l == layer else zero
            vf_ref[:, l] = v.reshape(n_seq, seq_len, D_ATTN) if l == layer else zero
    bg_ref[...] = col(3 * D_ATTN, D_CONV).astype(BF16)
    cg = col(3 * D_ATTN + D_CONV, D_CONV)
    xc = col(3 * D_ATTN + 2 * D_CONV, D_CONV)
    u_ref[...] = (cg * xc).astype(BF16)


def _proj(x, mod, layer, row0, mod_span, seq_len, norm1_g, w_in, qg, kg, gmat, rope_tabs, kv_out):
    n = x.shape[0]
    tile = TOKEN_TILE
    tiles_per_seq = max(seq_len // tile, 1)
    in_specs = [
        pl.BlockSpec((tile, D_MODEL), lambda i: (i, 0)),
        _mod_spec(layer, lambda i: row0 + (i * tile) // mod_span),
        _layer_spec((1, D_MODEL), layer),
        _layer_spec((D_MODEL, D_IN), layer),
        _layer_spec((1, D_ATTN), layer),
        _layer_spec((1, D_ATTN), layer),
        _const_spec((D_ATTN, D_ATTN)),
    ]
    args = [x, mod, norm1_g, w_in, qg, kg, gmat]
    if rope_tabs is not None:
        for t in rope_tabs:
            in_specs.append(pl.BlockSpec((tile, LANES), lambda i: (i % tiles_per_seq, 0)))
            args.append(t)
    aliases = {}
    emit = False
    if kv_out is not None and kv_out != "new":
        emit = "alias"
        for a in kv_out:
            aliases[len(args)] = 5 + len(aliases)
            in_specs.append(pl.BlockSpec(memory_space=pl.ANY))
            args.append(a)
    elif kv_out == "new":
        emit = True
    tok = jax.ShapeDtypeStruct((n, D_ATTN), BF16)
    tok_spec = pl.BlockSpec((tile, D_ATTN), lambda i: (i, 0))
    out_shape = [tok, tok, jax.ShapeDtypeStruct((D_ATTN, n), BF16), tok, tok]
    out_specs = [tok_spec, tok_spec, pl.BlockSpec((D_ATTN, tile), lambda i: (0, i)), tok_spec,
                 tok_spec]
    if emit:
        n_seq = tile // seq_len
        kv_shape = jax.ShapeDtypeStruct((n // seq_len, DEPTH, seq_len, D_ATTN), F32)
        out_shape += [kv_shape] * 2
        if emit == "alias":
            kv_spec = pl.BlockSpec((n_seq, None, seq_len, D_ATTN), lambda i: (i, layer, 0, 0))
        else:
            kv_spec = pl.BlockSpec((n_seq, DEPTH, seq_len, D_ATTN), lambda i: (i, 0, 0, 0))
        out_specs += [kv_spec] * 2
    return pl.pallas_call(
        functools.partial(_proj_kernel, rope=rope_tabs is not None, emit_f32=emit, seq_len=seq_len,
                          layer=layer),
        grid=(n // tile,),
        in_specs=in_specs,
        out_specs=out_specs,
        out_shape=out_shape,
        input_output_aliases=aliases,
        compiler_params=_cparams(1),
        name="proj",
    )(*args)


def _attn_kernel(*refs, has_cache, cast_ffn, tq, seq_len, layer, lam_init):
    it = iter(refs)
    x_ref, m_ref, q_ref, k_ref, vt_ref = (next(it) for _ in range(5))
    if has_cache:
        kc_ref, vc_ref = next(it), next(it)
    u_ref, up_ref, un_ref, bg_ref = (next(it) for _ in range(4))
    lam_refs = [next(it) for _ in range(4)]
    sg_ref, cw_ref, cg_ref, wo_ref = (next(it) for _ in range(4))
    if cast_ffn:
        wu_ref, wd_ref = next(it), next(it)
    o_ref = next(it)
    if cast_ffn:
        wub_ref, wdb_ref = next(it), next(it)
        wub_ref[...] = wu_ref[...].astype(BF16)
        wdb_ref[...] = wd_ref[...].astype(BF16)
    cu_ref = next(it)

    lq1, lk1, lq2, lk2 = (r[layer:layer + 1, :] for r in lam_refs)
    lam = (jnp.exp(jnp.sum(lq1 * lk1, axis=-1, keepdims=True))
           - jnp.exp(jnp.sum(lq2 * lk2, axis=-1, keepdims=True)) + lam_init)

    nq = min(tq, seq_len)
    if seq_len >= tq:
        groups = [(0, 0, k_ref.shape[0])]
    else:
        groups = [(g * seq_len, g * seq_len, seq_len) for g in range(tq // seq_len)]

    lane = lax.broadcasted_iota(jnp.int32, (1, LANES), 1)
    sub0 = jnp.where(lane < HEAD_DIM, 1.0, 0.0).astype(BF16)
    sub1 = jnp.where(lane < HEAD_DIM, 0.0, 1.0).astype(BF16)
    nt = (((1,), (1,)), ((), ()))
    ones = jnp.ones((BF16_ROWS, groups[0][2]), BF16)
    sgain = jnp.concatenate([sg_ref[...]] * (nq // LANES), axis=1)
    if has_cache:
        nc = kc_ref.shape[0]
        ones_c = jnp.ones((BF16_ROWS, nc), BF16)
        vct = vc_ref[...].T.astype(BF16)

    def scores(job):
        (q0, k0, nk), h = job
        sl = slice(h * LANES, (h + 1) * LANES)
        qh = q_ref[q0:q0 + nq, sl]
        q2 = jnp.concatenate([qh * sub0, qh * sub1], axis=0)
        st = lax.dot_general(k_ref[k0:k0 + nk, sl], q2, nt, preferred_element_type=F32)
        if not has_cache:
            return st, None
        return st, lax.dot_general(kc_ref[:, sl].astype(BF16), q2, nt,
                                   preferred_element_type=F32)

    jobs = [(grp, h) for grp in groups for h in range(N_HEADS)]
    nxt = scores(jobs[0])

    j = pl.program_id(1)
    u = u_ref[...].astype(F32)
    n_cols = D_CONV // LANES
    if seq_len >= tq:
        segs = [(F32_ROWS, tq, 0)]
        tiles_per_seq = seq_len // tq
        prev = jnp.where(j % tiles_per_seq == 0, 0.0,
                         up_ref[BF16_ROWS - 1:BF16_ROWS, :].astype(F32))
        nxt_row = jnp.where(j % tiles_per_seq == tiles_per_seq - 1, 0.0,
                            un_ref[0:1, :].astype(F32))
        for c in range(n_cols):
            lanes = slice(c * LANES, (c + 1) * LANES)
            cu_ref[c, F32_ROWS - 1:F32_ROWS, :] = prev[:, lanes]
            cu_ref[c, F32_ROWS + tq:F32_ROWS + tq + 1, :] = nxt_row[:, lanes]
    else:
        segs = [(_U_PAD + s * (seq_len + _U_PAD), seq_len, s * seq_len)
                for s in range(tq // seq_len)]

        @pl.when(jnp.logical_and(pl.program_id(0) == 0, j == 0))
        def _():
            cu_ref[...] = jnp.zeros_like(cu_ref)

    for c in range(n_cols):
        for b0, ln, r0 in segs:
            cu_ref[c, b0:b0 + ln, :] = u[r0:r0 + ln, c * LANES:(c + 1) * LANES]
    cw = cw_ref[...]
    conv_cols = []
    for c in range(n_cols):
        w3 = cw[:, c * LANES:(c + 1) * LANES]
        conv_cols.append(jnp.concatenate(
            [cu_ref[c, b0 - 1:b0 - 1 + ln, :] * w3[0:1] + cu_ref[c, b0:b0 + ln, :] * w3[1:2]
             + cu_ref[c, b0 + 1:b0 + 1 + ln, :] * w3[2:3] for b0, ln, r0 in segs], axis=0))
    t = bg_ref[...].astype(F32) * jnp.concatenate(conv_cols, axis=1)
    y = ((t * _rms_rows(t)) * cg_ref[...]).astype(BF16)
    y_proj = jnp.dot(y, wo_ref[D_ATTN:, :].astype(BF16), preferred_element_type=F32)

    outs = []
    for n, ((q0, k0, nk), h) in enumerate(jobs):
        sl = slice(h * LANES, (h + 1) * LANES)
        st, sct = nxt
        if n + 1 < len(jobs):
            nxt = scores(jobs[n + 1])
        mx = jnp.max(st, axis=0, keepdims=True)
        if has_cache:
            mx = jnp.maximum(mx, jnp.max(sct, axis=0, keepdims=True))
        e = jnp.exp2(st - mx).astype(BF16)
        oe = jnp.dot(jnp.concatenate([vt_ref[sl, k0:k0 + nk], ones], axis=0), e,
                     preferred_element_type=F32)
        if has_cache:
            ec = jnp.exp2(sct - mx).astype(BF16)
            oe = oe + jnp.dot(jnp.concatenate([vct[sl, :], ones_c], axis=0), ec,
                              preferred_element_type=F32)
        den = oe[V_DIM:V_DIM + 1, :]
        c0 = 1.0 / den[:, :nq]
        c1 = lam / den[:, nq:]
        oh = oe[:V_DIM, :nq] * c0 - oe[:V_DIM, nq:] * c1
        r = lax.rsqrt(jnp.mean(oh * oh, axis=0, keepdims=True) + EPS)
        outs.append(((oh * r) * sgain) * (1.0 - lam_init))
    ot = jnp.concatenate(
        [jnp.concatenate(outs[g * N_HEADS:(g + 1) * N_HEADS], axis=0) for g in range(len(groups))],
        axis=1)
    o = ot.T.astype(BF16)
    mix = y_proj + jnp.dot(o, wo_ref[0:D_ATTN, :].astype(BF16), preferred_element_type=F32)
    g1 = m_ref[...][:, 2 * D_MODEL:3 * D_MODEL]
    o_ref[...] = x_ref[...] + g1 * mix


def _attn(x, mod, layer, row0, mod_span, seq_len, tq, q, k, vt, cache, u, bg, lam_ps, subln_g,
          conv_w, conv_g, w_out, lam_init, ffn_weights=None):
    n = x.shape[0]
    grp_rows = max(seq_len, tq)
    qt = grp_rows // tq
    hb = tq // BF16_ROWS
    n_hblocks = n // BF16_ROWS
    tokmap = lambda b, j: (b * qt + j, 0)
    in_specs = [
        pl.BlockSpec((tq, D_MODEL), tokmap),
        _mod_spec(layer, lambda b, j: row0 + (b * grp_rows) // mod_span),
        pl.BlockSpec((tq, D_ATTN), tokmap),
        pl.BlockSpec((grp_rows, D_ATTN), lambda b, j: (b, 0)),
        pl.BlockSpec((D_ATTN, grp_rows), lambda b, j: (0, b)),
    ]
    args = [x, mod, q, k, vt]
    if cache is not None:
        for c in cache:
            p = c.shape[2]
            in_specs.append(pl.BlockSpec((None, None, p, D_ATTN), lambda b, j: (b, layer, 0, 0)))
            args.append(c)
    in_specs += [
        pl.BlockSpec((tq, D_CONV), tokmap),
        pl.BlockSpec((BF16_ROWS, D_CONV), lambda b, j: (jnp.maximum((b * qt + j) * hb - 1, 0), 0)),
        pl.BlockSpec((BF16_ROWS, D_CONV),
                     lambda b, j: (jnp.minimum((b * qt + j + 1) * hb, n_hblocks - 1), 0)),
        pl.BlockSpec((tq, D_CONV), tokmap),
    ]
    in_specs += [_const_spec((DEPTH, HEAD_DIM))] * 4
    in_specs += [
        _layer_spec((V_DIM, LANES), layer),
        _layer_spec((3, D_CONV), layer),
        _layer_spec((1, D_CONV), layer),
        _layer_spec((D_MODEL, D_MODEL), layer),
    ]
    args += [u, u, u, bg, *lam_ps, subln_g, conv_w, conv_g, w_out]
    out_specs = [pl.BlockSpec((tq, D_MODEL), tokmap)]
    out_shape = [jax.ShapeDtypeStruct(x.shape, F32)]
    if ffn_weights is not None:
        n_steps = n // tq
        for w in ffn_weights:
            rows, cols = w.shape[1:]
            slab = rows // n_steps
            assert slab * n_steps == rows and slab % BF16_ROWS == 0, (rows, n_steps)
            in_specs.append(pl.BlockSpec((None, slab, cols), lambda b, j: (layer, b * qt + j, 0)))
            out_specs.append(pl.BlockSpec((slab, cols), lambda b, j: (b * qt + j, 0)))
            out_shape.append(jax.ShapeDtypeStruct((rows, cols), BF16))
            args.append(w)
    if seq_len >= tq:
        rows_c = tq + 2 * F32_ROWS
    else:
        rows_c = _U_PAD + (tq // seq_len) * (seq_len + _U_PAD)
    outs = pl.pallas_call(
        functools.partial(_attn_kernel, has_cache=cache is not None,
                          cast_ffn=ffn_weights is not None, tq=tq, seq_len=seq_len, layer=layer,
                          lam_init=lam_init),
        grid=(n // grp_rows, qt),
        in_specs=in_specs,
        out_specs=out_specs,
        out_shape=out_shape,
        scratch_shapes=[pltpu.VMEM((D_CONV // LANES, rows_c, LANES), F32)],
        compiler_params=_cparams(2),
        name="attn",
    )(*args)
    return outs[0], tuple(outs[1:])


def _ffn_kernel(*refs, tile, seq_len, halo):
    it = iter(refs)
    x_ref = next(it)
    if halo:
        xp_ref, xn_ref = next(it), next(it)
    m_ref, g2_ref, wup_ref, cw_ref, wdn_ref, o_ref, hs_ref, u_ref, g_ref = (
        next(it) for _ in range(9))

    m = m_ref[...]
    sh2 = m[:, 3 * D_MODEL:4 * D_MODEL]
    sc2 = m[:, 4 * D_MODEL:5 * D_MODEL]
    gate = m[:, 5 * D_MODEL:6 * D_MODEL]

    gain = g2_ref[...] * (1.0 + sc2)

    def pre(xv):
        return (xv * _rms_rows(xv)) * gain + sh2

    x = x_ref[...]
    i = pl.program_id(0)
    base = HALO if halo else 0
    hs_ref[base:base + tile, :] = pre(x).astype(BF16)
    if halo:
        tiles_per_seq = seq_len // tile
        jt = i % tiles_per_seq
        hs_ref[0:HALO, :] = jnp.where(jt == 0, 0.0, pre(xp_ref[...])).astype(BF16)
        hs_ref[HALO + tile:, :] = jnp.where(jt == tiles_per_seq - 1, 0.0,
                                            pre(xn_ref[...])).astype(BF16)
    if halo:
        segs = [(HALO, tile, 0)]
    else:
        segs = [(_U_PAD + s * (seq_len + _U_PAD), seq_len, s * seq_len)
                for s in range(tile // seq_len)]

        @pl.when(i == 0)
        def _():
            u_ref[...] = jnp.zeros_like(u_ref)

    hs = hs_ref[...]
    acc, done = None, 0
    for c in range(N_FF_CHUNKS):
        slot = c % 2
        cols = (c * FF_CHUNK, D_FF + c * FF_CHUNK)
        for half, c0 in enumerate(cols):
            uu = jnp.dot(hs, wup_ref[:, c0:c0 + FF_CHUNK], preferred_element_type=F32)
            for j in range(_U_COLS):
                col = uu[:, j * LANES:(j + 1) * LANES]
                if halo:
                    u_ref[slot, half * _U_COLS + j] = col
                else:
                    for b0, ln, r0 in segs:
                        u_ref[slot, half * _U_COLS + j, b0:b0 + ln, :] = col[r0:r0 + ln]
        for b0, ln, r0 in segs:
            conv = []
            for half, c0 in enumerate(cols):
                w3 = cw_ref[:, c0:c0 + FF_CHUNK]
                taps = []
                for j in range(_U_COLS):
                    uc = u_ref.at[slot, half * _U_COLS + j]
                    w3j = w3[:, j * LANES:(j + 1) * LANES]
                    taps.append(uc[b0 - 1:b0 - 1 + ln, :] * w3j[0:1]
                                + uc[b0:b0 + ln, :] * w3j[1:2]
                                + uc[b0 + 1:b0 + 1 + ln, :] * w3j[2:3])
                conv.append(jnp.concatenate(taps, axis=1))
            ca, cb = conv
            g_ref[r0:r0 + ln, c * FF_CHUNK:(c + 1) * FF_CHUNK] = (
                (ca * jax.nn.sigmoid(ca)) * cb).astype(BF16)
        if c + 1 in _DOWN_PARTS:
            k0, k1 = done * FF_CHUNK, (c + 1) * FF_CHUNK
            part = [jnp.dot(g_ref[:, k0:k1], wdn_ref[k0:k1, n0:n0 + FF_CHUNK],
                            preferred_element_type=F32) for n0 in range(0, D_MODEL, FF_CHUNK)]
            acc = part if acc is None else [a + p for a, p in zip(acc, part)]
            done = c + 1

    for i0, n0 in enumerate(range(0, D_MODEL, FF_CHUNK)):
        o_ref[:, n0:n0 + FF_CHUNK] = x[:, n0:n0 + FF_CHUNK] + gate[:, n0:n0 + FF_CHUNK] * acc[i0]


def _ffn(x, mod, layer, row0, mod_span, seq_len, halo, norm2_g, w_up_b, conv_w, w_down_b):
    n = x.shape[0]
    tile = FFN_TILE
    hb = tile // BF16_ROWS
    n_hblocks = n // BF16_ROWS
    in_specs = [pl.BlockSpec((tile, D_MODEL), lambda i: (i, 0))]
    args = [x]
    if halo:
        in_specs += [
            pl.BlockSpec((HALO, D_MODEL), lambda i: (jnp.maximum(i * hb - 1, 0), 0)),
            pl.BlockSpec((HALO, D_MODEL), lambda i: (jnp.minimum((i + 1) * hb, n_hblocks - 1), 0)),
        ]
        args += [x, x]
    in_specs += [
        _mod_spec(layer, lambda i: row0 + (i * tile) // mod_span),
        _layer_spec((1, D_MODEL), layer),
        _const_spec((D_MODEL, 2 * D_FF)),
        _layer_spec((3, 2 * D_FF), layer),
        _const_spec((D_FF, D_MODEL)),
    ]
    args += [mod, norm2_g, w_up_b, conv_w, w_down_b]
    if halo:
        rows_h = rows_u = tile + 2 * HALO
    else:
        rows_h = tile
        rows_u = _U_PAD + (tile // seq_len) * (seq_len + _U_PAD)
    return pl.pallas_call(
        functools.partial(_ffn_kernel, tile=tile, seq_len=seq_len, halo=halo),
        grid=(n // tile,),
        in_specs=in_specs,
        out_specs=pl.BlockSpec((tile, D_MODEL), lambda i: (i, 0)),
        out_shape=jax.ShapeDtypeStruct(x.shape, F32),
        scratch_shapes=[pltpu.VMEM((rows_h, D_MODEL), BF16),
                        pltpu.VMEM((2, 2 * _U_COLS, rows_u, LANES), F32),
                        pltpu.VMEM((tile, D_FF), BF16)],
        compiler_params=_cparams(1),
        name="ffn",
    )(*args)


def _rope_tables(n_tok):
    rows = n_tok // GRID_W
    row = jnp.repeat(jnp.arange(rows, dtype=F32), GRID_W)
    col = jnp.tile(jnp.arange(GRID_W, dtype=F32), rows)
    n_freq = HEAD_DIM // 4
    inv = ROPE_BASE ** (-jnp.arange(n_freq, dtype=F32) / n_freq)
    ar, ac = row[:, None] * inv[None], col[:, None] * inv[None]
    cr, sr, cc, sc = jnp.cos(ar), jnp.sin(ar), jnp.cos(ac), jnp.sin(ac)
    z = jnp.zeros_like(sr)
    rep = lambda parts: jnp.tile(jnp.concatenate(parts, axis=-1), (1, LANES // HEAD_DIM))
    return rep([cr, cr, cc, cc]), rep([-sr, z, -sc, z]), rep([z, sr, z, sc])


def kernel(x_prompt, x_sample, cache_k, cache_v, c, c_ctx, w_mod, b_mod, norm1_g, w_in, q_norm_g, k_norm_g, lambda_q1, lambda_k1, lambda_q2, lambda_k2, subln_g, conv_w, conv_norm_g, w_out, norm2_g, w_up, ffn_conv_w, w_down):
    batch, seq, _ = x_prompt.shape
    dec_batch, dec_seq, _ = x_sample.shape
    past = cache_k.shape[2]

    cvecs = jnp.zeros((MOD_ROWS, D_MODEL), F32).at[0].set(c_ctx).at[1:1 + dec_batch].set(c)
    mod = _modulation(cvecs, w_mod, b_mod).reshape(DEPTH, MOD_ROWS, 1, N_MOD * D_MODEL)

    grp = jnp.arange(D_ATTN) // HEAD_DIM
    gmat = jnp.where(grp[:, None] == grp[None, :], 1.0 / HEAD_DIM, 0.0).astype(BF16)
    row3 = lambda g: g.reshape(DEPTH, 1, -1)
    qg = jnp.tile(row3(q_norm_g), (1, 1, D_ATTN // HEAD_DIM))
    kg = jnp.tile(row3(k_norm_g), (1, 1, D_ATTN // HEAD_DIM))
    n1, n2, cng = row3(norm1_g), row3(norm2_g), row3(conv_norm_g)
    sg = jnp.broadcast_to(subln_g[:, :, None], (DEPTH, V_DIM, LANES))
    lam_ps = (lambda_q1, lambda_k1, lambda_q2, lambda_k2)
    rope_tabs = _rope_tables(dec_seq)
    cache = (cache_k.reshape(dec_batch, DEPTH, past, D_ATTN),
             cache_v.reshape(dec_batch, DEPTH, past, D_ATTN))

    ffn_w = {}

    def run(x, row0, mod_span, seq_len, tq, ffn_halo, rope, cache, first_stream):
        kv = "new" if first_stream else None
        for l in range(DEPTH):
            lam_init = 0.8 - 0.6 * math.exp(-0.3 * l)
            outs = _proj(x, mod, l, row0, mod_span, seq_len, n1, w_in, qg, kg, gmat, rope, kv)
            q, k, v, u, bg = outs[:5]
            if first_stream:
                kv = tuple(outs[5:])
            x, cast = _attn(x, mod, l, row0, mod_span, seq_len, tq, q, k, v, cache, u, bg, lam_ps,
                            sg, conv_w, cng, w_out, lam_init,
                            ffn_weights=(w_up, w_down) if first_stream else None)
            if first_stream:
                ffn_w[l] = cast
            x = _ffn(x, mod, l, row0, mod_span, seq_len, ffn_halo, n2, ffn_w[l][0], ffn_conv_w,
                     ffn_w[l][1])
        return x, kv

    xp, (new_k, new_v) = run(x_prompt.reshape(batch * seq, D_MODEL), 0, batch * seq, seq,
                             CTX_Q_TILE, False, None, None, True)
    xs, _ = run(x_sample.reshape(dec_batch * dec_seq, D_MODEL), 1, dec_seq, dec_seq, Q_TILE, True,
                rope_tabs, cache, False)

    return (xp.reshape(batch, seq, D_MODEL), xs.reshape(dec_batch, dec_seq, D_MODEL),
            new_k.reshape(batch, DEPTH, seq, N_HEADS, 2, HEAD_DIM),
            new_v.reshape(batch, DEPTH, seq, N_HEADS, V_DIM))
```

```python
import functools
import math

import jax
import jax.numpy as jnp
import numpy as np
from jax import lax
from jax.experimental import pallas as pl
from jax.experimental.pallas import tpu as pltpu

D_MODEL = 1024
DEPTH = 2
GRID_W = 64
D_ATTN = 512
D_CONV = 512
N_HEADS = 4
HEAD_DIM = 64
V_DIM = 128
D_FF = 2816
ROPE_BASE = 10000.0
EPS = 1e-6
N_MOD = 6
D_IN = 3 * D_ATTN + 3 * D_CONV

F32 = jnp.float32
BF16 = jnp.bfloat16

LANES = 128
BF16_ROWS = 16
F32_ROWS = 8
MOD_ROWS = F32_ROWS
_U_PAD = F32_ROWS
VMEM_LIMIT = 60 * 1024 * 1024

FF_CHUNK = 256
N_FF_CHUNKS = D_FF // FF_CHUNK
_U_COLS = FF_CHUNK // LANES
_DOWN_PARTS = (8, N_FF_CHUNKS)
HALO = BF16_ROWS
MOD_TILE = 1536
TOKEN_TILE = 512
FFN_TILE = 512
Q_TILE = 512
CTX_Q_TILE = 512


def _cparams(n_axes, flags=None):
    return pltpu.CompilerParams(
        dimension_semantics=("arbitrary",) * n_axes, vmem_limit_bytes=VMEM_LIMIT, flags=flags)


def _layer_spec(shape, layer):
    nd = len(shape)
    return pl.BlockSpec((None,) + tuple(shape), lambda *_: (layer,) + (0,) * nd,
                        pipeline_mode=pl.Buffered(1))


def _const_spec(shape):
    nd = len(shape)
    return pl.BlockSpec(shape, lambda *_: (0,) * nd, pipeline_mode=pl.Buffered(1))


def _mod_spec(layer, row_of):
    return pl.BlockSpec((None, None, 1, N_MOD * D_MODEL),
                        lambda *idx: (layer, row_of(*idx), 0, 0))


def _rms_rows(x):
    return lax.rsqrt(jnp.mean(x * x, axis=-1, keepdims=True) + EPS)


def _mod_kernel(cv_ref, w_ref, b_ref, o_ref):
    cv = cv_ref[...]
    s = (cv * jax.nn.sigmoid(cv)).astype(BF16)
    o_ref[...] = jnp.dot(s, w_ref[...].astype(BF16), preferred_element_type=F32) + b_ref[...]


def _modulation(cvecs, w_mod, b_mod):
    tn = MOD_TILE
    n_out = N_MOD * D_MODEL
    return pl.pallas_call(
        _mod_kernel,
        grid=(DEPTH, n_out // tn),
        in_specs=[
            pl.BlockSpec((MOD_ROWS, D_MODEL), lambda l, j: (0, 0)),
            pl.BlockSpec((None, D_MODEL, tn), lambda l, j: (l, 0, j)),
            pl.BlockSpec((None, 1, tn), lambda l, j: (l, 0, j)),
        ],
        out_specs=pl.BlockSpec((None, MOD_ROWS, tn), lambda l, j: (l, 0, j)),
        out_shape=jax.ShapeDtypeStruct((DEPTH, MOD_ROWS, n_out), F32),
        compiler_params=_cparams(2),
        name="adaln_mod",
    )(cvecs, w_mod, b_mod.reshape(DEPTH, 1, n_out))


def _group_norm_gain(z, gmat_ref, gain):
    ms = jnp.dot((z * z).astype(BF16), gmat_ref[...], preferred_element_type=F32)
    return z * lax.rsqrt(ms + EPS) * gain


def _rope(z, c_ref, sa_ref, sb_ref):
    c, sa, sb = c_ref[...], sa_ref[...], sb_ref[...]
    cols = []
    for j in range(z.shape[1] // LANES):
        zj = z[:, j * LANES:(j + 1) * LANES]
        hi = pltpu.roll(zj, LANES - HEAD_DIM // 4, axis=1)
        lo = pltpu.roll(zj, HEAD_DIM // 4, axis=1)
        cols.append(zj * c + hi * sa + lo * sb)
    return jnp.concatenate(cols, axis=1)


def _proj_kernel(*refs, rope, emit_f32, seq_len, layer):
    it = iter(refs)
    x_ref, m_ref, g1_ref, w_ref, qg_ref, kg_ref, gmat_ref = (next(it) for _ in range(7))
    if rope:
        c_ref, sa_ref, sb_ref = (next(it) for _ in range(3))
    if emit_f32 == "alias":
        next(it), next(it)
    q_ref, k_ref, v_ref, u_ref, bg_ref = (next(it) for _ in range(5))
    if emit_f32:
        kf_ref, vf_ref = next(it), next(it)

    x = x_ref[...]
    m = m_ref[...]
    sh1 = m[:, 0:D_MODEL]
    sc1 = m[:, D_MODEL:2 * D_MODEL]
    h = ((x * _rms_rows(x)) * g1_ref[...]) * (1.0 + sc1) + sh1
    hb = h.astype(BF16)

    def col(i0, n):
        return jnp.dot(hb, w_ref[:, i0:i0 + n].astype(BF16), preferred_element_type=F32)

    q = _group_norm_gain(col(0, D_ATTN), gmat_ref, qg_ref[...])
    k = _group_norm_gain(col(D_ATTN, D_ATTN), gmat_ref, kg_ref[...])
    if rope:
        q = _rope(q, c_ref, sa_ref, sb_ref)
        k = _rope(k, c_ref, sa_ref, sb_ref)
    v = col(2 * D_ATTN, D_ATTN)
    q_ref[...] = (q * (HEAD_DIM ** -0.5 * math.log2(math.e))).astype(BF16)
    k_ref[...] = k.astype(BF16)
    v_ref[...] = v.T.astype(BF16)
    if emit_f32 == "alias":
        n_seq = x.shape[0] // seq_len
        kf_ref[...] = k.reshape(n_seq, seq_len, D_ATTN)
        vf_ref[...] = v.reshape(n_seq, seq_len, D_ATTN)
    elif emit_f32:
        n_seq = x.shape[0] // seq_len
        zero = jnp.zeros((n_seq, seq_len, D_ATTN), F32)
        for l in range(DEPTH):
            kf_ref[:, l] = k.reshape(n_seq, seq_len, D_ATTN) if l == layer else zero
            vf_ref[:, l] = v.reshape(n_seq, seq_len, D_ATTN) if l == layer else zero
    bg_ref[...] = col(3 * D_ATTN, D_CONV).astype(BF16)
    cg = col(3 * D_ATTN + D_CONV, D_CONV)
    xc = col(3 * D_ATTN + 2 * D_CONV, D_CONV)
    u_ref[...] = (cg * xc).astype(BF16)


def _proj(x, mod, layer, row0, mod_span, seq_len, norm1_g, w_in, qg, kg, gmat, rope_tabs, kv_out):
    n = x.shape[0]
    tile = TOKEN_TILE
    tiles_per_seq = max(seq_len // tile, 1)
    in_specs = [
        pl.BlockSpec((tile, D_MODEL), lambda i: (i, 0)),
        _mod_spec(layer, lambda i: row0 + (i * tile) // mod_span),
        _layer_spec((1, D_MODEL), layer),
        _layer_spec((D_MODEL, D_IN), layer),
        _layer_spec((1, D_ATTN), layer),
        _layer_spec((1, D_ATTN), layer),
        _const_spec((D_ATTN, D_ATTN)),
    ]
    args = [x, mod, norm1_g, w_in, qg, kg, gmat]
    if rope_tabs is not None:
        for t in rope_tabs:
            in_specs.append(pl.BlockSpec((tile, LANES), lambda i: (i % tiles_per_seq, 0)))
            args.append(t)
    aliases = {}
    emit = False
    if kv_out is not None and kv_out != "new":
        emit = "alias"
        for a in kv_out:
            aliases[len(args)] = 5 + len(aliases)
            in_specs.append(pl.BlockSpec(memory_space=pl.ANY))
            args.append(a)
    elif kv_out == "new":
        emit = True
    tok = jax.ShapeDtypeStruct((n, D_ATTN), BF16)
    tok_spec = pl.BlockSpec((tile, D_ATTN), lambda i: (i, 0))
    out_shape = [tok, tok, jax.ShapeDtypeStruct((D_ATTN, n), BF16), tok, tok]
    out_specs = [tok_spec, tok_spec, pl.BlockSpec((D_ATTN, tile), lambda i: (0, i)), tok_spec,
                 tok_spec]
    if emit:
        n_seq = tile // seq_len
        kv_shape = jax.ShapeDtypeStruct((n // seq_len, DEPTH, seq_len, D_ATTN), F32)
        out_shape += [kv_shape] * 2
        if emit == "alias":
            kv_spec = pl.BlockSpec((n_seq, None, seq_len, D_ATTN), lambda i: (i, layer, 0, 0))
        else:
            kv_spec = pl.BlockSpec((n_seq, DEPTH, seq_len, D_ATTN), lambda i: (i, 0, 0, 0))
        out_specs += [kv_spec] * 2
    return pl.pallas_call(
        functools.partial(_proj_kernel, rope=rope_tabs is not None, emit_f32=emit, seq_len=seq_len,
                          layer=layer),
        grid=(n // tile,),
        in_specs=in_specs,
        out_specs=out_specs,
        out_shape=out_shape,
        input_output_aliases=aliases,
        compiler_params=_cparams(1),
        name="proj",
    )(*args)


def _attn_kernel(*refs, has_cache, cast_ffn, tq, seq_len, layer, lam_init):
    it = iter(refs)
    x_ref, m_ref, q_ref, k_ref, vt_ref = (next(it) for _ in range(5))
    if has_cache:
        kc_ref, vc_ref = next(it), next(it)
    u_ref, up_ref, un_ref, bg_ref = (next(it) for _ in range(4))
    lam_refs = [next(it) for _ in range(4)]
    sg_ref, cw_ref, cg_ref, wo_ref = (next(it) for _ in range(4))
    if cast_ffn:
        wu_ref, wd_ref = next(it), next(it)
    o_ref = next(it)
    if cast_ffn:
        wub_ref, wdb_ref = next(it), next(it)
        wub_ref[...] = wu_ref[...].astype(BF16)
        wdb_ref[...] = wd_ref[...].astype(BF16)
    cu_ref = next(it)

    lq1, lk1, lq2, lk2 = (r[layer:layer + 1, :] for r in lam_refs)
    lam = (jnp.exp(jnp.sum(lq1 * lk1, axis=-1, keepdims=True))
           - jnp.exp(jnp.sum(lq2 * lk2, axis=-1, keepdims=True)) + lam_init)

    nq = min(tq, seq_len)
    if seq_len >= tq:
        groups = [(0, 0, k_ref.shape[0])]
    else:
        groups = [(g * seq_len, g * seq_len, seq_len) for g in range(tq // seq_len)]

    lane = lax.broadcasted_iota(jnp.int32, (1, LANES), 1)
    sub0 = jnp.where(lane < HEAD_DIM, 1.0, 0.0).astype(BF16)
    sub1 = jnp.where(lane < HEAD_DIM, 0.0, 1.0).astype(BF16)
    nt = (((1,), (1,)), ((), ()))
    ones = jnp.ones((BF16_ROWS, groups[0][2]), BF16)
    sgain = jnp.concatenate([sg_ref[...]] * (nq // LANES), axis=1)
    if has_cache:
        nc = kc_ref.shape[0]
        ones_c = jnp.ones((BF16_ROWS, nc), BF16)
        vct = vc_ref[...].T.astype(BF16)

    def scores(job):
        (q0, k0, nk), h = job
        sl = slice(h * LANES, (h + 1) * LANES)
        qh = q_ref[q0:q0 + nq, sl]
        q2 = jnp.concatenate([qh * sub0, qh * sub1], axis=0)
        st = lax.dot_general(k_ref[k0:k0 + nk, sl], q2, nt, preferred_element_type=F32)
        if not has_cache:
            return st, None
        return st, lax.dot_general(kc_ref[:, sl].astype(BF16), q2, nt,
                                   preferred_element_type=F32)

    jobs = [(grp, h) for grp in groups for h in range(N_HEADS)]
    nxt = scores(jobs[0])

    j = pl.program_id(1)
    u = u_ref[...].astype(F32)
    n_cols = D_CONV // LANES
    if seq_len >= tq:
        segs = [(F32_ROWS, tq, 0)]
        tiles_per_seq = seq_len // tq
        prev = jnp.where(j % tiles_per_seq == 0, 0.0,
                         up_ref[BF16_ROWS - 1:BF16_ROWS, :].astype(F32))
        nxt_row = jnp.where(j % tiles_per_seq == tiles_per_seq - 1, 0.0,
                            un_ref[0:1, :].astype(F32))
        for c in range(n_cols):
            lanes = slice(c * LANES, (c + 1) * LANES)
            cu_ref[c, F32_ROWS - 1:F32_ROWS, :] = prev[:, lanes]
            cu_ref[c, F32_ROWS + tq:F32_ROWS + tq + 1, :] = nxt_row[:, lanes]
    else:
        segs = [(_U_PAD + s * (seq_len + _U_PAD), seq_len, s * seq_len)
                for s in range(tq // seq_len)]

        @pl.when(jnp.logical_and(pl.program_id(0) == 0, j == 0))
        def _():
            cu_ref[...] = jnp.zeros_like(cu_ref)

    for c in range(n_cols):
        for b0, ln, r0 in segs:
            cu_ref[c, b0:b0 + ln, :] = u[r0:r0 + ln, c * LANES:(c + 1) * LANES]
    cw = cw_ref[...]
    conv_cols = []
    for c in range(n_cols):
        w3 = cw[:, c * LANES:(c + 1) * LANES]
        conv_cols.append(jnp.concatenate(
            [cu_ref[c, b0 - 1:b0 - 1 + ln, :] * w3[0:1] + cu_ref[c, b0:b0 + ln, :] * w3[1:2]
             + cu_ref[c, b0 + 1:b0 + 1 + ln, :] * w3[2:3] for b0, ln, r0 in segs], axis=0))
    t = bg_ref[...].astype(F32) * jnp.concatenate(conv_cols, axis=1)
    y = ((t * _rms_rows(t)) * cg_ref[...]).astype(BF16)
    y_proj = jnp.dot(y, wo_ref[D_ATTN:, :].astype(BF16), preferred_element_type=F32)

    outs = []
    for n, ((q0, k0, nk), h) in enumerate(jobs):
        sl = slice(h * LANES, (h + 1) * LANES)
        st, sct = nxt
        if n + 1 < len(jobs):
            nxt = scores(jobs[n + 1])
        mx = jnp.max(st, axis=0, keepdims=True)
        if has_cache:
            mx = jnp.maximum(mx, jnp.max(sct, axis=0, keepdims=True))
        e = jnp.exp2(st - mx).astype(BF16)
        oe = jnp.dot(jnp.concatenate([vt_ref[sl, k0:k0 + nk], ones], axis=0), e,
                     preferred_element_type=F32)
        if has_cache:
            ec = jnp.exp2(sct - mx).astype(BF16)
            oe = oe + jnp.dot(jnp.concatenate([vct[sl, :], ones_c], axis=0), ec,
                              preferred_element_type=F32)
        den = oe[V_DIM:V_DIM + 1, :]
        c0 = 1.0 / den[:, :nq]
        c1 = lam / den[:, nq:]
        oh = oe[:V_DIM, :nq] * c0 - oe[:V_DIM, nq:] * c1
        r = lax.rsqrt(jnp.mean(oh * oh, axis=0, keepdims=True) + EPS)
        outs.append(((oh * r) * sgain) * (1.0 - lam_init))
    ot = jnp.concatenate(
        [jnp.concatenate(outs[g * N_HEADS:(g + 1) * N_HEADS], axis=0) for g in range(len(groups))],
        axis=1)
    o = ot.T.astype(BF16)
    mix = y_proj + jnp.dot(o, wo_ref[0:D_ATTN, :].astype(BF16), preferred_element_type=F32)
    g1 = m_ref[...][:, 2 * D_MODEL:3 * D_MODEL]
    o_ref[...] = x_ref[...] + g1 * mix


def _attn(x, mod, layer, row0, mod_span, seq_len, tq, q, k, vt, cache, u, bg, lam_ps, subln_g,
          conv_w, conv_g, w_out, lam_init, ffn_weights=None):
    n = x.shape[0]
    grp_rows = max(seq_len, tq)
    qt = grp_rows // tq
    hb = tq // BF16_ROWS
    n_hblocks = n // BF16_ROWS
    tokmap = lambda b, j: (b * qt + j, 0)
    in_specs = [
        pl.BlockSpec((tq, D_MODEL), tokmap),
        _mod_spec(layer, lambda b, j: row0 + (b * grp_rows) // mod_span),
        pl.BlockSpec((tq, D_ATTN), tokmap),
        pl.BlockSpec((grp_rows, D_ATTN), lambda b, j: (b, 0)),
        pl.BlockSpec((D_ATTN, grp_rows), lambda b, j: (0, b)),
    ]
    args = [x, mod, q, k, vt]
    if cache is not None:
        for c in cache:
            p = c.shape[2]
            in_specs.append(pl.BlockSpec((None, None, p, D_ATTN), lambda b, j: (b, layer, 0, 0)))
            args.append(c)
    in_specs += [
        pl.BlockSpec((tq, D_CONV), tokmap),
        pl.BlockSpec((BF16_ROWS, D_CONV), lambda b, j: (jnp.maximum((b * qt + j) * hb - 1, 0), 0)),
        pl.BlockSpec((BF16_ROWS, D_CONV),
                     lambda b, j: (jnp.minimum((b * qt + j + 1) * hb, n_hblocks - 1), 0)),
        pl.BlockSpec((tq, D_CONV), tokmap),
    ]
    in_specs += [_const_spec((DEPTH, HEAD_DIM))] * 4
    in_specs += [
        _layer_spec((V_DIM, LANES), layer),
        _layer_spec((3, D_CONV), layer),
        _layer_spec((1, D_CONV), layer),
        _layer_spec((D_MODEL, D_MODEL), layer),
    ]
    args += [u, u, u, bg, *lam_ps, subln_g, conv_w, conv_g, w_out]
    out_specs = [pl.BlockSpec((tq, D_MODEL), tokmap)]
    out_shape = [jax.ShapeDtypeStruct(x.shape, F32)]
    if ffn_weights is not None:
        n_steps = n // tq
        for w in ffn_weights:
            rows, cols = w.shape[1:]
            slab = rows // n_steps
            assert slab * n_steps == rows and slab % BF16_ROWS == 0, (rows, n_steps)
            in_specs.append(pl.BlockSpec((None, slab, cols), lambda b, j: (layer, b * qt + j, 0)))
            out_specs.append(pl.BlockSpec((slab, cols), lambda b, j: (b * qt + j, 0)))
            out_shape.append(jax.ShapeDtypeStruct((rows, cols), BF16))
            args.append(w)
    if seq_len >= tq:
        rows_c = tq + 2 * F32_ROWS
    else:
        rows_c = _U_PAD + (tq // seq_len) * (seq_len + _U_PAD)
    outs = pl.pallas_call(
        functools.partial(_attn_kernel, has_cache=cache is not None,
                          cast_ffn=ffn_weights is not None, tq=tq, seq_len=seq_len, layer=layer,
                          lam_init=lam_init),
        grid=(n // grp_rows, qt),
        in_specs=in_specs,
        out_specs=out_specs,
        out_shape=out_shape,
        scratch_shapes=[pltpu.VMEM((D_CONV // LANES, rows_c, LANES), F32)],
        compiler_params=_cparams(2),
        name="attn",
    )(*args)
    return outs[0], tuple(outs[1:])


def _ffn_kernel(*refs, tile, seq_len, halo):
    it = iter(refs)
    x_ref = next(it)
    if halo:
        xp_ref, xn_ref = next(it), next(it)
    m_ref, g2_ref, wup_ref, cw_ref, wdn_ref, o_ref, hs_ref, u_ref, g_ref = (
        next(it) for _ in range(9))

    m = m_ref[...]
    sh2 = m[:, 3 * D_MODEL:4 * D_MODEL]
    sc2 = m[:, 4 * D_MODEL:5 * D_MODEL]
    gate = m[:, 5 * D_MODEL:6 * D_MODEL]

    gain = g2_ref[...] * (1.0 + sc2)

    def pre(xv):
        return (xv * _rms_rows(xv)) * gain + sh2

    x = x_ref[...]
    i = pl.program_id(0)
    base = HALO if halo else 0
    hs_ref[base:base + tile, :] = pre(x).astype(BF16)
    if halo:
        tiles_per_seq = seq_len // tile
        jt = i % tiles_per_seq
        hs_ref[0:HALO, :] = jnp.where(jt == 0, 0.0, pre(xp_ref[...])).astype(BF16)
        hs_ref[HALO + tile:, :] = jnp.where(jt == tiles_per_seq - 1, 0.0,
                                            pre(xn_ref[...])).astype(BF16)
    if halo:
        segs = [(HALO, tile, 0)]
    else:
        segs = [(_U_PAD + s * (seq_len + _U_PAD), seq_len, s * seq_len)
                for s in range(tile // seq_len)]

        @pl.when(i == 0)
        def _():
            u_ref[...] = jnp.zeros_like(u_ref)

    hs = hs_ref[...]
    acc, done = None, 0
    for c in range(N_FF_CHUNKS):
        slot = c % 2
        cols = (c * FF_CHUNK, D_FF + c * FF_CHUNK)
        for half, c0 in enumerate(cols):
            uu = jnp.dot(hs, wup_ref[:, c0:c0 + FF_CHUNK], preferred_element_type=F32)
            for j in range(_U_COLS):
                col = uu[:, j * LANES:(j + 1) * LANES]
                if halo:
                    u_ref[slot, half * _U_COLS + j] = col
                else:
                    for b0, ln, r0 in segs:
                        u_ref[slot, half * _U_COLS + j, b0:b0 + ln, :] = col[r0:r0 + ln]
        for b0, ln, r0 in segs:
            conv = []
            for half, c0 in enumerate(cols):
                w3 = cw_ref[:, c0:c0 + FF_CHUNK]
                taps = []
                for j in range(_U_COLS):
                    uc = u_ref.at[slot, half * _U_COLS + j]
                    w3j = w3[:, j * LANES:(j + 1) * LANES]
                    taps.append(uc[b0 - 1:b0 - 1 + ln, :] * w3j[0:1]
                                + uc[b0:b0 + ln, :] * w3j[1:2]
                                + uc[b0 + 1:b0 + 1 + ln, :] * w3j[2:3])
                conv.append(jnp.concatenate(taps, axis=1))
            ca, cb = conv
            g_ref[r0:r0 + ln, c * FF_CHUNK:(c + 1) * FF_CHUNK] = (
                (ca * jax.nn.sigmoid(ca)) * cb).astype(BF16)
        if c + 1 in _DOWN_PARTS:
            k0, k1 = done * FF_CHUNK, (c + 1) * FF_CHUNK
            part = [jnp.dot(g_ref[:, k0:k1], wdn_ref[k0:k1, n0:n0 + FF_CHUNK],
                            preferred_element_type=F32) for n0 in range(0, D_MODEL, FF_CHUNK)]
            acc = part if acc is None else [a + p for a, p in zip(acc, part)]
            done = c + 1

    for i0, n0 in enumerate(range(0, D_MODEL, FF_CHUNK)):
        o_ref[:, n0:n0 + FF_CHUNK] = x[:, n0:n0 + FF_CHUNK] + gate[:, n0:n0 + FF_CHUNK] * acc[i0]


def _ffn(x, mod, layer, row0, mod_span, seq_len, halo, norm2_g, w_up_b, conv_w, w_down_b):
    n = x.shape[0]
    tile = FFN_TILE
    hb = tile // BF16_ROWS
    n_hblocks = n // BF16_ROWS
    in_specs = [pl.BlockSpec((tile, D_MODEL), lambda i: (i, 0))]
    args = [x]
    if halo:
        in_specs += [
            pl.BlockSpec((HALO, D_MODEL), lambda i: (jnp.maximum(i * hb - 1, 0), 0)),
            pl.BlockSpec((HALO, D_MODEL), lambda i: (jnp.minimum((i + 1) * hb, n_hblocks - 1), 0)),
        ]
        args += [x, x]
    in_specs += [
        _mod_spec(layer, lambda i: row0 + (i * tile) // mod_span),
        _layer_spec((1, D_MODEL), layer),
        _const_spec((D_MODEL, 2 * D_FF)),
        _layer_spec((3, 2 * D_FF), layer),
        _const_spec((D_FF, D_MODEL)),
    ]
    args += [mod, norm2_g, w_up_b, conv_w, w_down_b]
    if halo:
        rows_h = rows_u = tile + 2 * HALO
    else:
        rows_h = tile
        rows_u = _U_PAD + (tile // seq_len) * (seq_len + _U_PAD)
    return pl.pallas_call(
        functools.partial(_ffn_kernel, tile=tile, seq_len=seq_len, halo=halo),
        grid=(n // tile,),
        in_specs=in_specs,
        out_specs=pl.BlockSpec((tile, D_MODEL), lambda i: (i, 0)),
        out_shape=jax.ShapeDtypeStruct(x.shape, F32),
        scratch_shapes=[pltpu.VMEM((rows_h, D_MODEL), BF16),
                        pltpu.VMEM((2, 2 * _U_COLS, rows_u, LANES), F32),
                        pltpu.VMEM((tile, D_FF), BF16)],
        compiler_params=_cparams(1),
        name="ffn",
    )(*args)


def _rope_tables(n_tok):
    rows = n_tok // GRID_W
    row = np.repeat(np.arange(rows, dtype=np.float32), GRID_W)
    col = np.tile(np.arange(GRID_W, dtype=np.float32), rows)
    n_freq = HEAD_DIM // 4
    inv = (np.float32(ROPE_BASE) ** (-np.arange(n_freq, dtype=np.float32) / np.float32(n_freq)))
    inv = inv.astype(np.float32)
    ar, ac = row[:, None] * inv[None], col[:, None] * inv[None]
    cr, sr, cc, sc = np.cos(ar), np.sin(ar), np.cos(ac), np.sin(ac)
    z = np.zeros_like(sr)
    rep = lambda parts: jnp.asarray(
        np.tile(np.concatenate(parts, axis=-1), (1, LANES // HEAD_DIM)).astype(np.float32))
    return rep([cr, cr, cc, cc]), rep([-sr, z, -sc, z]), rep([z, sr, z, sc])


def kernel(x_prompt, x_sample, cache_k, cache_v, c, c_ctx, w_mod, b_mod, norm1_g, w_in, q_norm_g, k_norm_g, lambda_q1, lambda_k1, lambda_q2, lambda_k2, subln_g, conv_w, conv_norm_g, w_out, norm2_g, w_up, ffn_conv_w, w_down):
    batch, seq, _ = x_prompt.shape
    dec_batch, dec_seq, _ = x_sample.shape
    past = cache_k.shape[2]

    cvecs = jnp.concatenate(
        [c_ctx[None], c, jnp.zeros((MOD_ROWS - 1 - dec_batch, D_MODEL), F32)], axis=0)
    mod = _modulation(cvecs, w_mod, b_mod).reshape(DEPTH, MOD_ROWS, 1, N_MOD * D_MODEL)

    grp = np.arange(D_ATTN) // HEAD_DIM
    gmat = jnp.asarray(np.where(grp[:, None] == grp[None, :], 1.0 / HEAD_DIM, 0.0), dtype=BF16)
    row3 = lambda g: g.reshape(DEPTH, 1, -1)
    qg = jnp.tile(row3(q_norm_g), (1, 1, D_ATTN // HEAD_DIM))
    kg = jnp.tile(row3(k_norm_g), (1, 1, D_ATTN // HEAD_DIM))
    n1, n2, cng = row3(norm1_g), row3(norm2_g), row3(conv_norm_g)
    sg = jnp.broadcast_to(subln_g[:, :, None], (DEPTH, V_DIM, LANES))
    lam_ps = (lambda_q1, lambda_k1, lambda_q2, lambda_k2)
    rope_tabs = _rope_tables(dec_seq)
    cache = (cache_k.reshape(dec_batch, DEPTH, past, D_ATTN),
             cache_v.reshape(dec_batch, DEPTH, past, D_ATTN))

    ffn_w = {}

    def run(x, row0, mod_span, seq_len, tq, ffn_halo, rope, cache, first_stream):
        kv = "new" if first_stream else None
        for l in range(DEPTH):
            lam_init = 0.8 - 0.6 * math.exp(-0.3 * l)
            outs = _proj(x, mod, l, row0, mod_span, seq_len, n1, w_in, qg, kg, gmat, rope, kv)
            q, k, v, u, bg = outs[:5]
            if first_stream:
                kv = tuple(outs[5:])
            x, cast = _attn(x, mod, l, row0, mod_span, seq_len, tq, q, k, v, cache, u, bg, lam_ps,
                            sg, conv_w, cng, w_out, lam_init,
                            ffn_weights=(w_up, w_down) if first_stream else None)
            if first_stream:
                ffn_w[l] = cast
            x = _ffn(x, mod, l, row0, mod_span, seq_len, ffn_halo, n2, ffn_w[l][0], ffn_conv_w,
                     ffn_w[l][1])
        return x, kv

    xp, (new_k, new_v) = run(x_prompt.reshape(batch * seq, D_MODEL), 0, batch * seq, seq,
                             CTX_Q_TILE, False, None, None, True)
    xs, _ = run(x_sample.reshape(dec_batch * dec_seq, D_MODEL), 1, dec_seq, dec_seq, Q_TILE, True,
                rope_tabs, cache, False)

    return (xp.reshape(batch, seq, D_MODEL), xs.reshape(dec_batch, dec_seq, D_MODEL),
            new_k.reshape(batch, DEPTH, seq, N_HEADS, 2, HEAD_DIM),
            new_v.reshape(batch, DEPTH, seq, N_HEADS, V_DIM))
```

```python
import functools
import math

import jax
import jax.numpy as jnp
import numpy as np
from jax import lax
from jax.experimental import pallas as pl
from jax.experimental.pallas import tpu as pltpu

D_MODEL = 1024
DEPTH = 2
GRID_W = 64
D_ATTN = 512
D_CONV = 512
N_HEADS = 4
HEAD_DIM = 64
V_DIM = 128
D_FF = 2816
ROPE_BASE = 10000.0
EPS = 1e-6
N_MOD = 6
D_IN = 3 * D_ATTN + 3 * D_CONV

F32 = jnp.float32
BF16 = jnp.bfloat16

LANES = 128
BF16_ROWS = 16
F32_ROWS = 8
MOD_ROWS = F32_ROWS
_U_PAD = F32_ROWS
VMEM_LIMIT = 60 * 1024 * 1024

FF_CHUNK = 256
N_FF_CHUNKS = D_FF // FF_CHUNK
_U_COLS = FF_CHUNK // LANES
_DOWN_PARTS = (8, N_FF_CHUNKS)
HALO = BF16_ROWS
MOD_TILE = 1536
TOKEN_TILE = 512
FFN_TILE = 512
Q_TILE = 512
CTX_Q_TILE = 512


def _cparams(n_axes, flags=None):
    return pltpu.CompilerParams(
        dimension_semantics=("arbitrary",) * n_axes, vmem_limit_bytes=VMEM_LIMIT, flags=flags)


def _layer_spec(shape, layer):
    nd = len(shape)
    return pl.BlockSpec((None,) + tuple(shape), lambda *_: (layer,) + (0,) * nd,
                        pipeline_mode=pl.Buffered(1))


def _const_spec(shape):
    nd = len(shape)
    return pl.BlockSpec(shape, lambda *_: (0,) * nd, pipeline_mode=pl.Buffered(1))


def _mod_spec(layer, row_of):
    return pl.BlockSpec((None, None, 1, N_MOD * D_MODEL),
                        lambda *idx: (layer, row_of(*idx), 0, 0))


def _rms_rows(x):
    return lax.rsqrt(jnp.mean(x * x, axis=-1, keepdims=True) + EPS)


def _mod_kernel(cv_ref, w_ref, b_ref, o_ref):
    cv = cv_ref[...]
    s = (cv * jax.nn.sigmoid(cv)).astype(BF16)
    o_ref[...] = jnp.dot(s, w_ref[...].astype(BF16), preferred_element_type=F32) + b_ref[...]


def _modulation(cvecs, w_mod, b_mod):
    tn = MOD_TILE
    n_out = N_MOD * D_MODEL
    return pl.pallas_call(
        _mod_kernel,
        grid=(DEPTH, n_out // tn),
        in_specs=[
            pl.BlockSpec((MOD_ROWS, D_MODEL), lambda l, j: (0, 0)),
            pl.BlockSpec((None, D_MODEL, tn), lambda l, j: (l, 0, j)),
            pl.BlockSpec((None, 1, tn), lambda l, j: (l, 0, j)),
        ],
        out_specs=pl.BlockSpec((None, MOD_ROWS, tn), lambda l, j: (l, 0, j)),
        out_shape=jax.ShapeDtypeStruct((DEPTH, MOD_ROWS, n_out), F32),
        compiler_params=_cparams(2),
        name="adaln_mod",
    )(cvecs, w_mod, b_mod.reshape(DEPTH, 1, n_out))


def _group_norm_gain(z, gmat_ref, gain):
    ms = jnp.dot((z * z).astype(BF16), gmat_ref[...], preferred_element_type=F32)
    return z * lax.rsqrt(ms + EPS) * gain


def _rope(z, c_ref, sa_ref, sb_ref):
    c, sa, sb = c_ref[...], sa_ref[...], sb_ref[...]
    cols = []
    for j in range(z.shape[1] // LANES):
        zj = z[:, j * LANES:(j + 1) * LANES]
        hi = pltpu.roll(zj, LANES - HEAD_DIM // 4, axis=1)
        lo = pltpu.roll(zj, HEAD_DIM // 4, axis=1)
        cols.append(zj * c + hi * sa + lo * sb)
    return jnp.concatenate(cols, axis=1)


def _proj_kernel(*refs, rope, emit_f32, seq_len, layer):
    it = iter(refs)
    x_ref, m_ref, g1_ref, w_ref, qg_ref, kg_ref, gmat_ref = (next(it) for _ in range(7))
    if rope:
        c_ref, sa_ref, sb_ref = (next(it) for _ in range(3))
    if emit_f32 == "alias":
        next(it), next(it)
    q_ref, k_ref, v_ref, u_ref, bg_ref = (next(it) for _ in range(5))
    if emit_f32:
        kf_ref, vf_ref = next(it), next(it)

    x = x_ref[...]
    m = m_ref[...]
    sh1 = m[:, 0:D_MODEL]
    sc1 = m[:, D_MODEL:2 * D_MODEL]
    row = slice(layer, layer + 1)
    h = ((x * _rms_rows(x)) * g1_ref[row, :]) * (1.0 + sc1) + sh1
    hb = h.astype(BF16)

    def col(i0, n):
        return jnp.dot(hb, w_ref[:, i0:i0 + n].astype(BF16), preferred_element_type=F32)

    q = _group_norm_gain(col(0, D_ATTN), gmat_ref, qg_ref[row, :])
    k = _group_norm_gain(col(D_ATTN, D_ATTN), gmat_ref, kg_ref[row, :])
    if rope:
        q = _rope(q, c_ref, sa_ref, sb_ref)
        k = _rope(k, c_ref, sa_ref, sb_ref)
    v = col(2 * D_ATTN, D_ATTN)
    q_ref[...] = (q * (HEAD_DIM ** -0.5 * math.log2(math.e))).astype(BF16)
    k_ref[...] = k.astype(BF16)
    v_ref[...] = v.T.astype(BF16)
    if emit_f32 == "alias":
        n_seq = x.shape[0] // seq_len
        kf_ref[...] = k.reshape(n_seq, seq_len, D_ATTN)
        vf_ref[...] = v.reshape(n_seq, seq_len, D_ATTN)
    elif emit_f32:
        n_seq = x.shape[0] // seq_len
        zero = jnp.zeros((n_seq, seq_len, D_ATTN), F32)
        for l in range(DEPTH):
            kf_ref[:, l] = k.reshape(n_seq, seq_len, D_ATTN) if l == layer else zero
            vf_ref[:, l] = v.reshape(n_seq, seq_len, D_ATTN) if l == layer else zero
    bg_ref[...] = col(3 * D_ATTN, D_CONV).astype(BF16)
    cg = col(3 * D_ATTN + D_CONV, D_CONV)
    xc = col(3 * D_ATTN + 2 * D_CONV, D_CONV)
    u_ref[...] = (cg * xc).astype(BF16)


def _proj(x, mod, layer, row0, mod_span, seq_len, norm1_g, w_in, qg, kg, gmat, rope_tabs, kv_out):
    n = x.shape[0]
    tile = TOKEN_TILE
    tiles_per_seq = max(seq_len // tile, 1)
    in_specs = [
        pl.BlockSpec((tile, D_MODEL), lambda i: (i, 0)),
        _mod_spec(layer, lambda i: row0 + (i * tile) // mod_span),
        _const_spec((DEPTH, D_MODEL)),
        _layer_spec((D_MODEL, D_IN), layer),
        _const_spec((DEPTH, D_ATTN)),
        _const_spec((DEPTH, D_ATTN)),
        _const_spec((D_ATTN, D_ATTN)),
    ]
    args = [x, mod, norm1_g, w_in, qg, kg, gmat]
    if rope_tabs is not None:
        for t in rope_tabs:
            in_specs.append(pl.BlockSpec((tile, LANES), lambda i: (i % tiles_per_seq, 0)))
            args.append(t)
    aliases = {}
    emit = False
    if kv_out is not None and kv_out != "new":
        emit = "alias"
        for a in kv_out:
            aliases[len(args)] = 5 + len(aliases)
            in_specs.append(pl.BlockSpec(memory_space=pl.ANY))
            args.append(a)
    elif kv_out == "new":
        emit = True
    tok = jax.ShapeDtypeStruct((n, D_ATTN), BF16)
    tok_spec = pl.BlockSpec((tile, D_ATTN), lambda i: (i, 0))
    out_shape = [tok, tok, jax.ShapeDtypeStruct((D_ATTN, n), BF16), tok, tok]
    out_specs = [tok_spec, tok_spec, pl.BlockSpec((D_ATTN, tile), lambda i: (0, i)), tok_spec,
                 tok_spec]
    if emit:
        n_seq = tile // seq_len
        kv_shape = jax.ShapeDtypeStruct((n // seq_len, DEPTH, seq_len, D_ATTN), F32)
        out_shape += [kv_shape] * 2
        if emit == "alias":
            kv_spec = pl.BlockSpec((n_seq, None, seq_len, D_ATTN), lambda i: (i, layer, 0, 0))
        else:
            kv_spec = pl.BlockSpec((n_seq, DEPTH, seq_len, D_ATTN), lambda i: (i, 0, 0, 0))
        out_specs += [kv_spec] * 2
    return pl.pallas_call(
        functools.partial(_proj_kernel, rope=rope_tabs is not None, emit_f32=emit, seq_len=seq_len,
                          layer=layer),
        grid=(n // tile,),
        in_specs=in_specs,
        out_specs=out_specs,
        out_shape=out_shape,
        input_output_aliases=aliases,
        compiler_params=_cparams(1),
        name="proj",
    )(*args)


def _attn_kernel(*refs, has_cache, cast_ffn, tq, seq_len, layer, lam_init):
    it = iter(refs)
    x_ref, m_ref, q_ref, k_ref, vt_ref = (next(it) for _ in range(5))
    if has_cache:
        kc_ref, vc_ref = next(it), next(it)
    u_ref, up_ref, un_ref, bg_ref = (next(it) for _ in range(4))
    lam_refs = [next(it) for _ in range(4)]
    sg_ref, cw_ref, cg_ref, wo_ref = (next(it) for _ in range(4))
    if cast_ffn:
        wu_ref, wd_ref = next(it), next(it)
    o_ref = next(it)
    if cast_ffn:
        wub_ref, wdb_ref = next(it), next(it)
        wub_ref[...] = wu_ref[...].astype(BF16)
        wdb_ref[...] = wd_ref[...].astype(BF16)
    cu_ref = next(it)

    lq1, lk1, lq2, lk2 = (r[layer:layer + 1, :] for r in lam_refs)
    lam = (jnp.exp(jnp.sum(lq1 * lk1, axis=-1, keepdims=True))
           - jnp.exp(jnp.sum(lq2 * lk2, axis=-1, keepdims=True)) + lam_init)

    nq = min(tq, seq_len)
    if seq_len >= tq:
        groups = [(0, 0, k_ref.shape[0])]
    else:
        groups = [(g * seq_len, g * seq_len, seq_len) for g in range(tq // seq_len)]

    lane = lax.broadcasted_iota(jnp.int32, (1, LANES), 1)
    sub0 = jnp.where(lane < HEAD_DIM, 1.0, 0.0).astype(BF16)
    sub1 = jnp.where(lane < HEAD_DIM, 0.0, 1.0).astype(BF16)
    nt = (((1,), (1,)), ((), ()))
    ones = jnp.ones((BF16_ROWS, groups[0][2]), BF16)
    sgain = jnp.concatenate([sg_ref[...]] * (nq // LANES), axis=1)
    if has_cache:
        nc = kc_ref.shape[0]
        ones_c = jnp.ones((BF16_ROWS, nc), BF16)
        vct = vc_ref[...].T.astype(BF16)

    def scores(job):
        (q0, k0, nk), h = job
        sl = slice(h * LANES, (h + 1) * LANES)
        qh = q_ref[q0:q0 + nq, sl]
        q2 = jnp.concatenate([qh * sub0, qh * sub1], axis=0)
        st = lax.dot_general(k_ref[k0:k0 + nk, sl], q2, nt, preferred_element_type=F32)
        if not has_cache:
            return st, None
        return st, lax.dot_general(kc_ref[:, sl].astype(BF16), q2, nt,
                                   preferred_element_type=F32)

    jobs = [(grp, h) for grp in groups for h in range(N_HEADS)]
    nxt = scores(jobs[0])

    j = pl.program_id(1)
    u = u_ref[...].astype(F32)
    n_cols = D_CONV // LANES
    if seq_len >= tq:
        segs = [(F32_ROWS, tq, 0)]
        tiles_per_seq = seq_len // tq
        prev = jnp.where(j % tiles_per_seq == 0, 0.0,
                         up_ref[BF16_ROWS - 1:BF16_ROWS, :].astype(F32))
        nxt_row = jnp.where(j % tiles_per_seq == tiles_per_seq - 1, 0.0,
                            un_ref[0:1, :].astype(F32))
        for c in range(n_cols):
            lanes = slice(c * LANES, (c + 1) * LANES)
            cu_ref[c, F32_ROWS - 1:F32_ROWS, :] = prev[:, lanes]
            cu_ref[c, F32_ROWS + tq:F32_ROWS + tq + 1, :] = nxt_row[:, lanes]
    else:
        segs = [(_U_PAD + s * (seq_len + _U_PAD), seq_len, s * seq_len)
                for s in range(tq // seq_len)]

        @pl.when(jnp.logical_and(pl.program_id(0) == 0, j == 0))
        def _():
            cu_ref[...] = jnp.zeros_like(cu_ref)

    for c in range(n_cols):
        for b0, ln, r0 in segs:
            cu_ref[c, b0:b0 + ln, :] = u[r0:r0 + ln, c * LANES:(c + 1) * LANES]
    cw = cw_ref[...]
    conv_cols = []
    for c in range(n_cols):
        w3 = cw[:, c * LANES:(c + 1) * LANES]
        conv_cols.append(jnp.concatenate(
            [cu_ref[c, b0 - 1:b0 - 1 + ln, :] * w3[0:1] + cu_ref[c, b0:b0 + ln, :] * w3[1:2]
             + cu_ref[c, b0 + 1:b0 + 1 + ln, :] * w3[2:3] for b0, ln, r0 in segs], axis=0))
    t = bg_ref[...].astype(F32) * jnp.concatenate(conv_cols, axis=1)
    y = ((t * _rms_rows(t)) * cg_ref[layer:layer + 1, :]).astype(BF16)
    y_proj = jnp.dot(y, wo_ref[D_ATTN:, :].astype(BF16), preferred_element_type=F32)

    outs = []
    for n, ((q0, k0, nk), h) in enumerate(jobs):
        sl = slice(h * LANES, (h + 1) * LANES)
        st, sct = nxt
        if n + 1 < len(jobs):
            nxt = scores(jobs[n + 1])
        mx = jnp.max(st, axis=0, keepdims=True)
        if has_cache:
            mx = jnp.maximum(mx, jnp.max(sct, axis=0, keepdims=True))
        e = jnp.exp2(st - mx).astype(BF16)
        oe = jnp.dot(jnp.concatenate([vt_ref[sl, k0:k0 + nk], ones], axis=0), e,
                     preferred_element_type=F32)
        if has_cache:
            ec = jnp.exp2(sct - mx).astype(BF16)
            oe = oe + jnp.dot(jnp.concatenate([vct[sl, :], ones_c], axis=0), ec,
                              preferred_element_type=F32)
        den = oe[V_DIM:V_DIM + 1, :]
        c0 = 1.0 / den[:, :nq]
        c1 = lam / den[:, nq:]
        oh = oe[:V_DIM, :nq] * c0 - oe[:V_DIM, nq:] * c1
        r = lax.rsqrt(jnp.mean(oh * oh, axis=0, keepdims=True) + EPS)
        outs.append(((oh * r) * sgain) * (1.0 - lam_init))
    ot = jnp.concatenate(
        [jnp.concatenate(outs[g * N_HEADS:(g + 1) * N_HEADS], axis=0) for g in range(len(groups))],
        axis=1)
    o = ot.T.astype(BF16)
    mix = y_proj + jnp.dot(o, wo_ref[0:D_ATTN, :].astype(BF16), preferred_element_type=F32)
    g1 = m_ref[...][:, 2 * D_MODEL:3 * D_MODEL]
    o_ref[...] = x_ref[...] + g1 * mix


def _attn(x, mod, layer, row0, mod_span, seq_len, tq, q, k, vt, cache, u, bg, lam_ps, subln_g,
          conv_w, conv_g, w_out, lam_init, ffn_weights=None):
    n = x.shape[0]
    grp_rows = max(seq_len, tq)
    qt = grp_rows // tq
    hb = tq // BF16_ROWS
    n_hblocks = n // BF16_ROWS
    tokmap = lambda b, j: (b * qt + j, 0)
    in_specs = [
        pl.BlockSpec((tq, D_MODEL), tokmap),
        _mod_spec(layer, lambda b, j: row0 + (b * grp_rows) // mod_span),
        pl.BlockSpec((tq, D_ATTN), tokmap),
        pl.BlockSpec((grp_rows, D_ATTN), lambda b, j: (b, 0)),
        pl.BlockSpec((D_ATTN, grp_rows), lambda b, j: (0, b)),
    ]
    args = [x, mod, q, k, vt]
    if cache is not None:
        for c in cache:
            p = c.shape[2]
            in_specs.append(pl.BlockSpec((None, None, p, D_ATTN), lambda b, j: (b, layer, 0, 0)))
            args.append(c)
    in_specs += [
        pl.BlockSpec((tq, D_CONV), tokmap),
        pl.BlockSpec((BF16_ROWS, D_CONV), lambda b, j: (jnp.maximum((b * qt + j) * hb - 1, 0), 0)),
        pl.BlockSpec((BF16_ROWS, D_CONV),
                     lambda b, j: (jnp.minimum((b * qt + j + 1) * hb, n_hblocks - 1), 0)),
        pl.BlockSpec((tq, D_CONV), tokmap),
    ]
    in_specs += [_const_spec((DEPTH, HEAD_DIM))] * 4
    in_specs += [
        _layer_spec((V_DIM, LANES), layer),
        _layer_spec((3, D_CONV), layer),
        _const_spec((DEPTH, D_CONV)),
        _layer_spec((D_MODEL, D_MODEL), layer),
    ]
    args += [u, u, u, bg, *lam_ps, subln_g, conv_w, conv_g, w_out]
    out_specs = [pl.BlockSpec((tq, D_MODEL), tokmap)]
    out_shape = [jax.ShapeDtypeStruct(x.shape, F32)]
    if ffn_weights is not None:
        n_steps = n // tq
        for w in ffn_weights:
            rows, cols = w.shape[1:]
            slab = rows // n_steps
            assert slab * n_steps == rows and slab % BF16_ROWS == 0, (rows, n_steps)
            in_specs.append(pl.BlockSpec((None, slab, cols), lambda b, j: (layer, b * qt + j, 0)))
            out_specs.append(pl.BlockSpec((slab, cols), lambda b, j: (b * qt + j, 0)))
            out_shape.append(jax.ShapeDtypeStruct((rows, cols), BF16))
            args.append(w)
    if seq_len >= tq:
        rows_c = tq + 2 * F32_ROWS
    else:
        rows_c = _U_PAD + (tq // seq_len) * (seq_len + _U_PAD)
    outs = pl.pallas_call(
        functools.partial(_attn_kernel, has_cache=cache is not None,
                          cast_ffn=ffn_weights is not None, tq=tq, seq_len=seq_len, layer=layer,
                          lam_init=lam_init),
        grid=(n // grp_rows, qt),
        in_specs=in_specs,
        out_specs=out_specs,
        out_shape=out_shape,
        scratch_shapes=[pltpu.VMEM((D_CONV // LANES, rows_c, LANES), F32)],
        compiler_params=_cparams(2),
        name="attn",
    )(*args)
    return outs[0], tuple(outs[1:])


def _ffn_kernel(*refs, tile, seq_len, halo, layer):
    it = iter(refs)
    x_ref = next(it)
    if halo:
        xp_ref, xn_ref = next(it), next(it)
    m_ref, g2_ref, wup_ref, cw_ref, wdn_ref, o_ref, hs_ref, u_ref, g_ref = (
        next(it) for _ in range(9))

    m = m_ref[...]
    sh2 = m[:, 3 * D_MODEL:4 * D_MODEL]
    sc2 = m[:, 4 * D_MODEL:5 * D_MODEL]
    gate = m[:, 5 * D_MODEL:6 * D_MODEL]

    gain = g2_ref[layer:layer + 1, :] * (1.0 + sc2)

    def pre(xv):
        return (xv * _rms_rows(xv)) * gain + sh2

    x = x_ref[...]
    i = pl.program_id(0)
    base = HALO if halo else 0
    hs_ref[base:base + tile, :] = pre(x).astype(BF16)
    if halo:
        tiles_per_seq = seq_len // tile
        jt = i % tiles_per_seq
        hs_ref[0:HALO, :] = jnp.where(jt == 0, 0.0, pre(xp_ref[...])).astype(BF16)
        hs_ref[HALO + tile:, :] = jnp.where(jt == tiles_per_seq - 1, 0.0,
                                            pre(xn_ref[...])).astype(BF16)
    if halo:
        segs = [(HALO, tile, 0)]
    else:
        segs = [(_U_PAD + s * (seq_len + _U_PAD), seq_len, s * seq_len)
                for s in range(tile // seq_len)]

        @pl.when(i == 0)
        def _():
            u_ref[...] = jnp.zeros_like(u_ref)

    hs = hs_ref[...]
    acc, done = None, 0
    for c in range(N_FF_CHUNKS):
        slot = c % 2
        cols = (c * FF_CHUNK, D_FF + c * FF_CHUNK)
        for half, c0 in enumerate(cols):
            uu = jnp.dot(hs, wup_ref[:, c0:c0 + FF_CHUNK], preferred_element_type=F32)
            for j in range(_U_COLS):
                col = uu[:, j * LANES:(j + 1) * LANES]
                if halo:
                    u_ref[slot, half * _U_COLS + j] = col
                else:
                    for b0, ln, r0 in segs:
                        u_ref[slot, half * _U_COLS + j, b0:b0 + ln, :] = col[r0:r0 + ln]
        for b0, ln, r0 in segs:
            conv = []
            for half, c0 in enumerate(cols):
                w3 = cw_ref[:, c0:c0 + FF_CHUNK]
                taps = []
                for j in range(_U_COLS):
                    uc = u_ref.at[slot, half * _U_COLS + j]
                    w3j = w3[:, j * LANES:(j + 1) * LANES]
                    taps.append(uc[b0 - 1:b0 - 1 + ln, :] * w3j[0:1]
                                + uc[b0:b0 + ln, :] * w3j[1:2]
                                + uc[b0 + 1:b0 + 1 + ln, :] * w3j[2:3])
                conv.append(jnp.concatenate(taps, axis=1))
            ca, cb = conv
            g_ref[r0:r0 + ln, c * FF_CHUNK:(c + 1) * FF_CHUNK] = (
                (ca * jax.nn.sigmoid(ca)) * cb).astype(BF16)
        if c + 1 in _DOWN_PARTS:
            k0, k1 = done * FF_CHUNK, (c + 1) * FF_CHUNK
            part = [jnp.dot(g_ref[:, k0:k1], wdn_ref[k0:k1, n0:n0 + FF_CHUNK],
                            preferred_element_type=F32) for n0 in range(0, D_MODEL, FF_CHUNK)]
            acc = part if acc is None else [a + p for a, p in zip(acc, part)]
            done = c + 1

    for i0, n0 in enumerate(range(0, D_MODEL, FF_CHUNK)):
        o_ref[:, n0:n0 + FF_CHUNK] = x[:, n0:n0 + FF_CHUNK] + gate[:, n0:n0 + FF_CHUNK] * acc[i0]


def _ffn(x, mod, layer, row0, mod_span, seq_len, halo, norm2_g, w_up_b, conv_w, w_down_b):
    n = x.shape[0]
    tile = FFN_TILE
    hb = tile // BF16_ROWS
    n_hblocks = n // BF16_ROWS
    in_specs = [pl.BlockSpec((tile, D_MODEL), lambda i: (i, 0))]
    args = [x]
    if halo:
        in_specs += [
            pl.BlockSpec((HALO, D_MODEL), lambda i: (jnp.maximum(i * hb - 1, 0), 0)),
            pl.BlockSpec((HALO, D_MODEL), lambda i: (jnp.minimum((i + 1) * hb, n_hblocks - 1), 0)),
        ]
        args += [x, x]
    in_specs += [
        _mod_spec(layer, lambda i: row0 + (i * tile) // mod_span),
        _const_spec((DEPTH, D_MODEL)),
        _const_spec((D_MODEL, 2 * D_FF)),
        _layer_spec((3, 2 * D_FF), layer),
        _const_spec((D_FF, D_MODEL)),
    ]
    args += [mod, norm2_g, w_up_b, conv_w, w_down_b]
    if halo:
        rows_h = rows_u = tile + 2 * HALO
    else:
        rows_h = tile
        rows_u = _U_PAD + (tile // seq_len) * (seq_len + _U_PAD)
    return pl.pallas_call(
        functools.partial(_ffn_kernel, tile=tile, seq_len=seq_len, halo=halo, layer=layer),
        grid=(n // tile,),
        in_specs=in_specs,
        out_specs=pl.BlockSpec((tile, D_MODEL), lambda i: (i, 0)),
        out_shape=jax.ShapeDtypeStruct(x.shape, F32),
        scratch_shapes=[pltpu.VMEM((rows_h, D_MODEL), BF16),
                        pltpu.VMEM((2, 2 * _U_COLS, rows_u, LANES), F32),
                        pltpu.VMEM((tile, D_FF), BF16)],
        compiler_params=_cparams(1),
        name="ffn",
    )(*args)


def _rope_tables(n_tok):
    rows = n_tok // GRID_W
    row = np.repeat(np.arange(rows, dtype=np.float32), GRID_W)
    col = np.tile(np.arange(GRID_W, dtype=np.float32), rows)
    n_freq = HEAD_DIM // 4
    inv = (np.float32(ROPE_BASE) ** (-np.arange(n_freq, dtype=np.float32) / np.float32(n_freq)))
    inv = inv.astype(np.float32)
    ar, ac = row[:, None] * inv[None], col[:, None] * inv[None]
    cr, sr, cc, sc = np.cos(ar), np.sin(ar), np.cos(ac), np.sin(ac)
    z = np.zeros_like(sr)
    rep = lambda parts: jnp.asarray(
        np.tile(np.concatenate(parts, axis=-1), (1, LANES // HEAD_DIM)).astype(np.float32))
    return rep([cr, cr, cc, cc]), rep([-sr, z, -sc, z]), rep([z, sr, z, sc])


def kernel(x_prompt, x_sample, cache_k, cache_v, c, c_ctx, w_mod, b_mod, norm1_g, w_in, q_norm_g, k_norm_g, lambda_q1, lambda_k1, lambda_q2, lambda_k2, subln_g, conv_w, conv_norm_g, w_out, norm2_g, w_up, ffn_conv_w, w_down):
    batch, seq, _ = x_prompt.shape
    dec_batch, dec_seq, _ = x_sample.shape
    past = cache_k.shape[2]

    cvecs = jnp.concatenate(
        [c_ctx[None], c, jnp.zeros((MOD_ROWS - 1 - dec_batch, D_MODEL), F32)], axis=0)
    mod = _modulation(cvecs, w_mod, b_mod).reshape(DEPTH, MOD_ROWS, 1, N_MOD * D_MODEL)

    grp = np.arange(D_ATTN) // HEAD_DIM
    gmat = jnp.asarray(np.where(grp[:, None] == grp[None, :], 1.0 / HEAD_DIM, 0.0), dtype=BF16)
    qg = jnp.tile(q_norm_g, (1, D_ATTN // HEAD_DIM))
    kg = jnp.tile(k_norm_g, (1, D_ATTN // HEAD_DIM))
    n1, n2, cng = norm1_g, norm2_g, conv_norm_g
    sg = jnp.broadcast_to(subln_g[:, :, None], (DEPTH, V_DIM, LANES))
    lam_ps = (lambda_q1, lambda_k1, lambda_q2, lambda_k2)
    rope_tabs = _rope_tables(dec_seq)
    cache = (cache_k.reshape(dec_batch, DEPTH, past, D_ATTN),
             cache_v.reshape(dec_batch, DEPTH, past, D_ATTN))

    ffn_w = {}

    def run(x, row0, mod_span, seq_len, tq, ffn_halo, rope, cache, first_stream):
        kv = "new" if first_stream else None
        for l in range(DEPTH):
            lam_init = 0.8 - 0.6 * math.exp(-0.3 * l)
            outs = _proj(x, mod, l, row0, mod_span, seq_len, n1, w_in, qg, kg, gmat, rope, kv)
            q, k, v, u, bg = outs[:5]
            if first_stream:
                kv = tuple(outs[5:])
            x, cast = _attn(x, mod, l, row0, mod_span, seq_len, tq, q, k, v, cache, u, bg, lam_ps,
                            sg, conv_w, cng, w_out, lam_init,
                            ffn_weights=(w_up, w_down) if first_stream else None)
            if first_stream:
                ffn_w[l] = cast
            x = _ffn(x, mod, l, row0, mod_span, seq_len, ffn_halo, n2, ffn_w[l][0], ffn_conv_w,
                     ffn_w[l][1])
        return x, kv

    xp, (new_k, new_v) = run(x_prompt.reshape(batch * seq, D_MODEL), 0, batch * seq, seq,
                             CTX_Q_TILE, False, None, None, True)
    xs, _ = run(x_sample.reshape(dec_batch * dec_seq, D_MODEL), 1, dec_seq, dec_seq, Q_TILE, True,
                rope_tabs, cache, False)

    return (xp.reshape(batch, seq, D_MODEL), xs.reshape(dec_batch, dec_seq, D_MODEL),
            new_k.reshape(batch, DEPTH, seq, N_HEADS, 2, HEAD_DIM),
            new_v.reshape(batch, DEPTH, seq, N_HEADS, V_DIM))
```

```python
import functools
import math

import jax
import jax.numpy as jnp
import numpy as np
from jax import lax
from jax.experimental import pallas as pl
from jax.experimental.pallas import tpu as pltpu

D_MODEL = 1024
DEPTH = 2
GRID_W = 64
D_ATTN = 512
D_CONV = 512
N_HEADS = 4
HEAD_DIM = 64
V_DIM = 128
D_FF = 2816
ROPE_BASE = 10000.0
EPS = 1e-6
N_MOD = 6
D_IN = 3 * D_ATTN + 3 * D_CONV

F32 = jnp.float32
BF16 = jnp.bfloat16

LANES = 128
BF16_ROWS = 16
F32_ROWS = 8
MOD_ROWS = F32_ROWS
_U_PAD = F32_ROWS
VMEM_LIMIT = 60 * 1024 * 1024

FF_CHUNK = 256
N_FF_CHUNKS = D_FF // FF_CHUNK
_U_COLS = FF_CHUNK // LANES
_DOWN_PARTS = (8, N_FF_CHUNKS)
HALO = BF16_ROWS
MOD_TILE = 1536
TOKEN_TILE = 512
FFN_TILE = 512
Q_TILE = 512
CTX_Q_TILE = 512


def _cparams(n_axes, flags=None):
    return pltpu.CompilerParams(
        dimension_semantics=("arbitrary",) * n_axes, vmem_limit_bytes=VMEM_LIMIT, flags=flags)


def _layer_spec(shape, layer):
    nd = len(shape)
    return pl.BlockSpec((None,) + tuple(shape), lambda *_: (layer,) + (0,) * nd,
                        pipeline_mode=pl.Buffered(1))


def _const_spec(shape):
    nd = len(shape)
    return pl.BlockSpec(shape, lambda *_: (0,) * nd, pipeline_mode=pl.Buffered(1))


def _mod_spec(layer, row_of):
    return pl.BlockSpec((None, None, 1, N_MOD * D_MODEL),
                        lambda *idx: (layer, row_of(*idx), 0, 0))


def _rms_rows(x):
    return lax.rsqrt(jnp.mean(x * x, axis=-1, keepdims=True) + EPS)


def _mod_kernel(cv_ref, w_ref, b_ref, o_ref):
    cv = cv_ref[...]
    s = (cv * jax.nn.sigmoid(cv)).astype(BF16)
    o_ref[...] = jnp.dot(s, w_ref[...].astype(BF16), preferred_element_type=F32) + b_ref[...]


def _modulation(cvecs, w_mod, b_mod):
    tn = MOD_TILE
    n_out = N_MOD * D_MODEL
    return pl.pallas_call(
        _mod_kernel,
        grid=(DEPTH, n_out // tn),
        in_specs=[
            pl.BlockSpec((MOD_ROWS, D_MODEL), lambda l, j: (0, 0)),
            pl.BlockSpec((None, D_MODEL, tn), lambda l, j: (l, 0, j)),
            pl.BlockSpec((None, 1, tn), lambda l, j: (l, 0, j)),
        ],
        out_specs=pl.BlockSpec((None, MOD_ROWS, tn), lambda l, j: (l, 0, j)),
        out_shape=jax.ShapeDtypeStruct((DEPTH, MOD_ROWS, n_out), F32),
        compiler_params=_cparams(2),
        name="adaln_mod",
    )(cvecs, w_mod, b_mod.reshape(DEPTH, 1, n_out))


def _group_norm_gain(z, gmat_ref, gain):
    zz = (z * z).astype(BF16)
    ms = jnp.concatenate(
        [jnp.dot(zz[:, j:j + 256], gmat_ref[j:j + 256, j:j + 256], preferred_element_type=F32)
         for j in range(0, D_ATTN, 256)], axis=1)
    return z * lax.rsqrt(ms + EPS) * gain


def _rope(z, c_ref, sa_ref, sb_ref):
    c, sa, sb = c_ref[...], sa_ref[...], sb_ref[...]
    cols = []
    for j in range(z.shape[1] // LANES):
        zj = z[:, j * LANES:(j + 1) * LANES]
        hi = pltpu.roll(zj, LANES - HEAD_DIM // 4, axis=1)
        lo = pltpu.roll(zj, HEAD_DIM // 4, axis=1)
        cols.append(zj * c + hi * sa + lo * sb)
    return jnp.concatenate(cols, axis=1)


def _proj_kernel(*refs, rope, emit_f32, seq_len, layer):
    it = iter(refs)
    x_ref, m_ref, g1_ref, w_ref, qg_ref, kg_ref, gmat_ref = (next(it) for _ in range(7))
    if rope:
        c_ref, sa_ref, sb_ref = (next(it) for _ in range(3))
    if emit_f32 == "alias":
        next(it), next(it)
    q_ref, k_ref, v_ref, u_ref, bg_ref = (next(it) for _ in range(5))
    if emit_f32:
        kf_ref, vf_ref = next(it), next(it)

    x = x_ref[...]
    m = m_ref[...]
    sh1 = m[:, 0:D_MODEL]
    sc1 = m[:, D_MODEL:2 * D_MODEL]
    row = slice(layer, layer + 1)
    h = ((x * _rms_rows(x)) * g1_ref[row, :]) * (1.0 + sc1) + sh1
    hb = h.astype(BF16)

    def col(i0, n):
        return jnp.dot(hb, w_ref[:, i0:i0 + n].astype(BF16), preferred_element_type=F32)

    q = _group_norm_gain(col(0, D_ATTN), gmat_ref, qg_ref[row, :])
    k = _group_norm_gain(col(D_ATTN, D_ATTN), gmat_ref, kg_ref[row, :])
    if rope:
        q = _rope(q, c_ref, sa_ref, sb_ref)
        k = _rope(k, c_ref, sa_ref, sb_ref)
    v = col(2 * D_ATTN, D_ATTN)
    q_ref[...] = (q * (HEAD_DIM ** -0.5 * math.log2(math.e))).astype(BF16)
    k_ref[...] = k.astype(BF16)
    v_ref[...] = v.T.astype(BF16)
    if emit_f32 == "alias":
        n_seq = x.shape[0] // seq_len
        kf_ref[...] = k.reshape(n_seq, seq_len, D_ATTN)
        vf_ref[...] = v.reshape(n_seq, seq_len, D_ATTN)
    elif emit_f32:
        n_seq = x.shape[0] // seq_len
        zero = jnp.zeros((n_seq, seq_len, D_ATTN), F32)
        for l in range(DEPTH):
            kf_ref[:, l] = k.reshape(n_seq, seq_len, D_ATTN) if l == layer else zero
            vf_ref[:, l] = v.reshape(n_seq, seq_len, D_ATTN) if l == layer else zero
    bg_ref[...] = col(3 * D_ATTN, D_CONV).astype(BF16)
    cg = col(3 * D_ATTN + D_CONV, D_CONV)
    xc = col(3 * D_ATTN + 2 * D_CONV, D_CONV)
    u_ref[...] = (cg * xc).astype(BF16)


def _proj(x, mod, layer, row0, mod_span, seq_len, norm1_g, w_in, qg, kg, gmat, rope_tabs, kv_out):
    n = x.shape[0]
    tile = TOKEN_TILE
    tiles_per_seq = max(seq_len // tile, 1)
    in_specs = [
        pl.BlockSpec((tile, D_MODEL), lambda i: (i, 0)),
        _mod_spec(layer, lambda i: row0 + (i * tile) // mod_span),
        _const_spec((DEPTH, D_MODEL)),
        _layer_spec((D_MODEL, D_IN), layer),
        _const_spec((DEPTH, D_ATTN)),
        _const_spec((DEPTH, D_ATTN)),
        _const_spec((D_ATTN, D_ATTN)),
    ]
    args = [x, mod, norm1_g, w_in, qg, kg, gmat]
    if rope_tabs is not None:
        for t in rope_tabs:
            in_specs.append(pl.BlockSpec((tile, LANES), lambda i: (i % tiles_per_seq, 0)))
            args.append(t)
    aliases = {}
    emit = False
    if kv_out is not None and kv_out != "new":
        emit = "alias"
        for a in kv_out:
            aliases[len(args)] = 5 + len(aliases)
            in_specs.append(pl.BlockSpec(memory_space=pl.ANY))
            args.append(a)
    elif kv_out == "new":
        emit = True
    tok = jax.ShapeDtypeStruct((n, D_ATTN), BF16)
    tok_spec = pl.BlockSpec((tile, D_ATTN), lambda i: (i, 0))
    out_shape = [tok, tok, jax.ShapeDtypeStruct((D_ATTN, n), BF16), tok, tok]
    out_specs = [tok_spec, tok_spec, pl.BlockSpec((D_ATTN, tile), lambda i: (0, i)), tok_spec,
                 tok_spec]
    if emit:
        n_seq = tile // seq_len
        kv_shape = jax.ShapeDtypeStruct((n // seq_len, DEPTH, seq_len, D_ATTN), F32)
        out_shape += [kv_shape] * 2
        if emit == "alias":
            kv_spec = pl.BlockSpec((n_seq, None, seq_len, D_ATTN), lambda i: (i, layer, 0, 0))
        else:
            kv_spec = pl.BlockSpec((n_seq, DEPTH, seq_len, D_ATTN), lambda i: (i, 0, 0, 0))
        out_specs += [kv_spec] * 2
    return pl.pallas_call(
        functools.partial(_proj_kernel, rope=rope_tabs is not None, emit_f32=emit, seq_len=seq_len,
                          layer=layer),
        grid=(n // tile,),
        in_specs=in_specs,
        out_specs=out_specs,
        out_shape=out_shape,
        input_output_aliases=aliases,
        compiler_params=_cparams(1),
        name="proj",
    )(*args)


def _attn_kernel(*refs, has_cache, cast_ffn, tq, seq_len, layer, lam_init):
    it = iter(refs)
    x_ref, m_ref, q_ref, k_ref, vt_ref = (next(it) for _ in range(5))
    if has_cache:
        kc_ref, vc_ref = next(it), next(it)
    u_ref, up_ref, un_ref, bg_ref = (next(it) for _ in range(4))
    lam_refs = [next(it) for _ in range(4)]
    sg_ref, cw_ref, cg_ref, wo_ref = (next(it) for _ in range(4))
    if cast_ffn:
        wu_ref, wd_ref = next(it), next(it)
    o_ref = next(it)
    if cast_ffn:
        wub_ref, wdb_ref = next(it), next(it)
        wub_ref[...] = wu_ref[...].astype(BF16)
        wdb_ref[...] = wd_ref[...].astype(BF16)
    cu_ref = next(it)

    lq1, lk1, lq2, lk2 = (r[layer:layer + 1, :] for r in lam_refs)
    lam = (jnp.exp(jnp.sum(lq1 * lk1, axis=-1, keepdims=True))
           - jnp.exp(jnp.sum(lq2 * lk2, axis=-1, keepdims=True)) + lam_init)

    nq = min(tq, seq_len)
    if seq_len >= tq:
        groups = [(0, 0, k_ref.shape[0])]
    else:
        groups = [(g * seq_len, g * seq_len, seq_len) for g in range(tq // seq_len)]

    lane = lax.broadcasted_iota(jnp.int32, (1, LANES), 1)
    sub0 = jnp.where(lane < HEAD_DIM, 1.0, 0.0).astype(BF16)
    sub1 = jnp.where(lane < HEAD_DIM, 0.0, 1.0).astype(BF16)
    nt = (((1,), (1,)), ((), ()))
    ones = jnp.ones((BF16_ROWS, groups[0][2]), BF16)
    sgain = jnp.concatenate([sg_ref[...]] * (nq // LANES), axis=1)
    if has_cache:
        nc = kc_ref.shape[0]
        ones_c = jnp.ones((BF16_ROWS, nc), BF16)
        vct = vc_ref[...].T.astype(BF16)

    def scores(job):
        (q0, k0, nk), h = job
        sl = slice(h * LANES, (h + 1) * LANES)
        qh = q_ref[q0:q0 + nq, sl]
        q2 = jnp.concatenate([qh * sub0, qh * sub1], axis=0)
        st = lax.dot_general(k_ref[k0:k0 + nk, sl], q2, nt, preferred_element_type=F32)
        if not has_cache:
            return st, None
        return st, lax.dot_general(kc_ref[:, sl].astype(BF16), q2, nt,
                                   preferred_element_type=F32)

    jobs = [(grp, h) for grp in groups for h in range(N_HEADS)]
    nxt = scores(jobs[0])

    j = pl.program_id(1)
    u = u_ref[...].astype(F32)
    n_cols = D_CONV // LANES
    if seq_len >= tq:
        segs = [(F32_ROWS, tq, 0)]
        tiles_per_seq = seq_len // tq
        prev = jnp.where(j % tiles_per_seq == 0, 0.0,
                         up_ref[BF16_ROWS - 1:BF16_ROWS, :].astype(F32))
        nxt_row = jnp.where(j % tiles_per_seq == tiles_per_seq - 1, 0.0,
                            un_ref[0:1, :].astype(F32))
        for c in range(n_cols):
            lanes = slice(c * LANES, (c + 1) * LANES)
            cu_ref[c, F32_ROWS - 1:F32_ROWS, :] = prev[:, lanes]
            cu_ref[c, F32_ROWS + tq:F32_ROWS + tq + 1, :] = nxt_row[:, lanes]
    else:
        segs = [(_U_PAD + s * (seq_len + _U_PAD), seq_len, s * seq_len)
                for s in range(tq // seq_len)]

        @pl.when(jnp.logical_and(pl.program_id(0) == 0, j == 0))
        def _():
            cu_ref[...] = jnp.zeros_like(cu_ref)

    for c in range(n_cols):
        for b0, ln, r0 in segs:
            cu_ref[c, b0:b0 + ln, :] = u[r0:r0 + ln, c * LANES:(c + 1) * LANES]
    cw = cw_ref[...]
    conv_cols = []
    for c in range(n_cols):
        w3 = cw[:, c * LANES:(c + 1) * LANES]
        conv_cols.append(jnp.concatenate(
            [cu_ref[c, b0 - 1:b0 - 1 + ln, :] * w3[0:1] + cu_ref[c, b0:b0 + ln, :] * w3[1:2]
             + cu_ref[c, b0 + 1:b0 + 1 + ln, :] * w3[2:3] for b0, ln, r0 in segs], axis=0))
    t = bg_ref[...].astype(F32) * jnp.concatenate(conv_cols, axis=1)
    y = ((t * _rms_rows(t)) * cg_ref[layer:layer + 1, :]).astype(BF16)
    y_proj = jnp.dot(y, wo_ref[D_ATTN:, :].astype(BF16), preferred_element_type=F32)

    outs = []
    for n, ((q0, k0, nk), h) in enumerate(jobs):
        sl = slice(h * LANES, (h + 1) * LANES)
        st, sct = nxt
        if n + 1 < len(jobs):
            nxt = scores(jobs[n + 1])
        mx = jnp.max(st, axis=0, keepdims=True)
        if has_cache:
            mx = jnp.maximum(mx, jnp.max(sct, axis=0, keepdims=True))
        e = jnp.exp2(st - mx).astype(BF16)
        oe = jnp.dot(jnp.concatenate([vt_ref[sl, k0:k0 + nk], ones], axis=0), e,
                     preferred_element_type=F32)
        if has_cache:
            ec = jnp.exp2(sct - mx).astype(BF16)
            oe = oe + jnp.dot(jnp.concatenate([vct[sl, :], ones_c], axis=0), ec,
                              preferred_element_type=F32)
        den = oe[V_DIM:V_DIM + 1, :]
        c0 = 1.0 / den[:, :nq]
        c1 = lam / den[:, nq:]
        oh = oe[:V_DIM, :nq] * c0 - oe[:V_DIM, nq:] * c1
        r = lax.rsqrt(jnp.mean(oh * oh, axis=0, keepdims=True) + EPS)
        outs.append(((oh * r) * sgain) * (1.0 - lam_init))
    ot = jnp.concatenate(
        [jnp.concatenate(outs[g * N_HEADS:(g + 1) * N_HEADS], axis=0) for g in range(len(groups))],
        axis=1)
    o = ot.T.astype(BF16)
    mix = y_proj + jnp.dot(o, wo_ref[0:D_ATTN, :].astype(BF16), preferred_element_type=F32)
    g1 = m_ref[...][:, 2 * D_MODEL:3 * D_MODEL]
    o_ref[...] = x_ref[...] + g1 * mix


def _attn(x, mod, layer, row0, mod_span, seq_len, tq, q, k, vt, cache, u, bg, lam_ps, subln_g,
          conv_w, conv_g, w_out, lam_init, ffn_weights=None):
    n = x.shape[0]
    grp_rows = max(seq_len, tq)
    qt = grp_rows // tq
    hb = tq // BF16_ROWS
    n_hblocks = n // BF16_ROWS
    tokmap = lambda b, j: (b * qt + j, 0)
    in_specs = [
        pl.BlockSpec((tq, D_MODEL), tokmap),
        _mod_spec(layer, lambda b, j: row0 + (b * grp_rows) // mod_span),
        pl.BlockSpec((tq, D_ATTN), tokmap),
        pl.BlockSpec((grp_rows, D_ATTN), lambda b, j: (b, 0)),
        pl.BlockSpec((D_ATTN, grp_rows), lambda b, j: (0, b)),
    ]
    args = [x, mod, q, k, vt]
    if cache is not None:
        for c in cache:
            p = c.shape[2]
            in_specs.append(pl.BlockSpec((None, None, p, D_ATTN), lambda b, j: (b, layer, 0, 0)))
            args.append(c)
    in_specs += [
        pl.BlockSpec((tq, D_CONV), tokmap),
        pl.BlockSpec((BF16_ROWS, D_CONV), lambda b, j: (jnp.maximum((b * qt + j) * hb - 1, 0), 0)),
        pl.BlockSpec((BF16_ROWS, D_CONV),
                     lambda b, j: (jnp.minimum((b * qt + j + 1) * hb, n_hblocks - 1), 0)),
        pl.BlockSpec((tq, D_CONV), tokmap),
    ]
    in_specs += [_const_spec((DEPTH, HEAD_DIM))] * 4
    in_specs += [
        _layer_spec((V_DIM, LANES), layer),
        _layer_spec((3, D_CONV), layer),
        _const_spec((DEPTH, D_CONV)),
        _layer_spec((D_MODEL, D_MODEL), layer),
    ]
    args += [u, u, u, bg, *lam_ps, subln_g, conv_w, conv_g, w_out]
    out_specs = [pl.BlockSpec((tq, D_MODEL), tokmap)]
    out_shape = [jax.ShapeDtypeStruct(x.shape, F32)]
    if ffn_weights is not None:
        n_steps = n // tq
        for w in ffn_weights:
            rows, cols = w.shape[1:]
            slab = rows // n_steps
            assert slab * n_steps == rows and slab % BF16_ROWS == 0, (rows, n_steps)
            in_specs.append(pl.BlockSpec((None, slab, cols), lambda b, j: (layer, b * qt + j, 0)))
            out_specs.append(pl.BlockSpec((slab, cols), lambda b, j: (b * qt + j, 0)))
            out_shape.append(jax.ShapeDtypeStruct((rows, cols), BF16))
            args.append(w)
    if seq_len >= tq:
        rows_c = tq + 2 * F32_ROWS
    else:
        rows_c = _U_PAD + (tq // seq_len) * (seq_len + _U_PAD)
    outs = pl.pallas_call(
        functools.partial(_attn_kernel, has_cache=cache is not None,
                          cast_ffn=ffn_weights is not None, tq=tq, seq_len=seq_len, layer=layer,
                          lam_init=lam_init),
        grid=(n // grp_rows, qt),
        in_specs=in_specs,
        out_specs=out_specs,
        out_shape=out_shape,
        scratch_shapes=[pltpu.VMEM((D_CONV // LANES, rows_c, LANES), F32)],
        compiler_params=_cparams(2),
        name="attn",
    )(*args)
    return outs[0], tuple(outs[1:])


def _ffn_kernel(*refs, tile, seq_len, halo, layer):
    it = iter(refs)
    x_ref = next(it)
    if halo:
        xp_ref, xn_ref = next(it), next(it)
    m_ref, g2_ref, wup_ref, cw_ref, wdn_ref, o_ref, hs_ref, u_ref, g_ref = (
        next(it) for _ in range(9))

    m = m_ref[...]
    sh2 = m[:, 3 * D_MODEL:4 * D_MODEL]
    sc2 = m[:, 4 * D_MODEL:5 * D_MODEL]
    gate = m[:, 5 * D_MODEL:6 * D_MODEL]

    gain = g2_ref[layer:layer + 1, :] * (1.0 + sc2)

    def pre(xv):
        return (xv * _rms_rows(xv)) * gain + sh2

    x = x_ref[...]
    i = pl.program_id(0)
    base = HALO if halo else 0
    hs_ref[base:base + tile, :] = pre(x).astype(BF16)
    if halo:
        tiles_per_seq = seq_len // tile
        jt = i % tiles_per_seq
        hs_ref[0:HALO, :] = jnp.where(jt == 0, 0.0, pre(xp_ref[...])).astype(BF16)
        hs_ref[HALO + tile:, :] = jnp.where(jt == tiles_per_seq - 1, 0.0,
                                            pre(xn_ref[...])).astype(BF16)
    if halo:
        segs = [(HALO, tile, 0)]
    else:
        segs = [(_U_PAD + s * (seq_len + _U_PAD), seq_len, s * seq_len)
                for s in range(tile // seq_len)]

        @pl.when(i == 0)
        def _():
            u_ref[...] = jnp.zeros_like(u_ref)

    hs = hs_ref[...]
    acc, done = None, 0
    for c in range(N_FF_CHUNKS):
        slot = c % 2
        cols = (c * FF_CHUNK, D_FF + c * FF_CHUNK)
        for half, c0 in enumerate(cols):
            uu = jnp.dot(hs, wup_ref[:, c0:c0 + FF_CHUNK], preferred_element_type=F32)
            for j in range(_U_COLS):
                col = uu[:, j * LANES:(j + 1) * LANES]
                if halo:
                    u_ref[slot, half * _U_COLS + j] = col
                else:
                    for b0, ln, r0 in segs:
                        u_ref[slot, half * _U_COLS + j, b0:b0 + ln, :] = col[r0:r0 + ln]
        for b0, ln, r0 in segs:
            conv = []
            for half, c0 in enumerate(cols):
                w3 = cw_ref[:, c0:c0 + FF_CHUNK]
                taps = []
                for j in range(_U_COLS):
                    uc = u_ref.at[slot, half * _U_COLS + j]
                    w3j = w3[:, j * LANES:(j + 1) * LANES]
                    taps.append(uc[b0 - 1:b0 - 1 + ln, :] * w3j[0:1]
                                + uc[b0:b0 + ln, :] * w3j[1:2]
                                + uc[b0 + 1:b0 + 1 + ln, :] * w3j[2:3])
                conv.append(jnp.concatenate(taps, axis=1))
            ca, cb = conv
            g_ref[r0:r0 + ln, c * FF_CHUNK:(c + 1) * FF_CHUNK] = (
                (ca * jax.nn.sigmoid(ca)) * cb).astype(BF16)
        if c + 1 in _DOWN_PARTS:
            k0, k1 = done * FF_CHUNK, (c + 1) * FF_CHUNK
            part = [jnp.dot(g_ref[:, k0:k1], wdn_ref[k0:k1, n0:n0 + FF_CHUNK],
                            preferred_element_type=F32) for n0 in range(0, D_MODEL, FF_CHUNK)]
            acc = part if acc is None else [a + p for a, p in zip(acc, part)]
            done = c + 1

    for i0, n0 in enumerate(range(0, D_MODEL, FF_CHUNK)):
        o_ref[:, n0:n0 + FF_CHUNK] = x[:, n0:n0 + FF_CHUNK] + gate[:, n0:n0 + FF_CHUNK] * acc[i0]


def _ffn(x, mod, layer, row0, mod_span, seq_len, halo, norm2_g, w_up_b, conv_w, w_down_b):
    n = x.shape[0]
    tile = FFN_TILE
    hb = tile // BF16_ROWS
    n_hblocks = n // BF16_ROWS
    in_specs = [pl.BlockSpec((tile, D_MODEL), lambda i: (i, 0))]
    args = [x]
    if halo:
        in_specs += [
            pl.BlockSpec((HALO, D_MODEL), lambda i: (jnp.maximum(i * hb - 1, 0), 0)),
            pl.BlockSpec((HALO, D_MODEL), lambda i: (jnp.minimum((i + 1) * hb, n_hblocks - 1), 0)),
        ]
        args += [x, x]
    in_specs += [
        _mod_spec(layer, lambda i: row0 + (i * tile) // mod_span),
        _const_spec((DEPTH, D_MODEL)),
        _const_spec((D_MODEL, 2 * D_FF)),
        _layer_spec((3, 2 * D_FF), layer),
        _const_spec((D_FF, D_MODEL)),
    ]
    args += [mod, norm2_g, w_up_b, conv_w, w_down_b]
    if halo:
        rows_h = rows_u = tile + 2 * HALO
    else:
        rows_h = tile
        rows_u = _U_PAD + (tile // seq_len) * (seq_len + _U_PAD)
    return pl.pallas_call(
        functools.partial(_ffn_kernel, tile=tile, seq_len=seq_len, halo=halo, layer=layer),
        grid=(n // tile,),
        in_specs=in_specs,
        out_specs=pl.BlockSpec((tile, D_MODEL), lambda i: (i, 0)),
        out_shape=jax.ShapeDtypeStruct(x.shape, F32),
        scratch_shapes=[pltpu.VMEM((rows_h, D_MODEL), BF16),
                        pltpu.VMEM((2, 2 * _U_COLS, rows_u, LANES), F32),
                        pltpu.VMEM((tile, D_FF), BF16)],
        compiler_params=_cparams(1),
        name="ffn",
    )(*args)


def _rope_tables(n_tok):
    rows = n_tok // GRID_W
    row = np.repeat(np.arange(rows, dtype=np.float32), GRID_W)
    col = np.tile(np.arange(GRID_W, dtype=np.float32), rows)
    n_freq = HEAD_DIM // 4
    inv = (np.float32(ROPE_BASE) ** (-np.arange(n_freq, dtype=np.float32) / np.float32(n_freq)))
    inv = inv.astype(np.float32)
    ar, ac = row[:, None] * inv[None], col[:, None] * inv[None]
    cr, sr, cc, sc = np.cos(ar), np.sin(ar), np.cos(ac), np.sin(ac)
    z = np.zeros_like(sr)
    rep = lambda parts: jnp.asarray(
        np.tile(np.concatenate(parts, axis=-1), (1, LANES // HEAD_DIM)).astype(np.float32))
    return rep([cr, cr, cc, cc]), rep([-sr, z, -sc, z]), rep([z, sr, z, sc])


def kernel(x_prompt, x_sample, cache_k, cache_v, c, c_ctx, w_mod, b_mod, norm1_g, w_in, q_norm_g, k_norm_g, lambda_q1, lambda_k1, lambda_q2, lambda_k2, subln_g, conv_w, conv_norm_g, w_out, norm2_g, w_up, ffn_conv_w, w_down):
    batch, seq, _ = x_prompt.shape
    dec_batch, dec_seq, _ = x_sample.shape
    past = cache_k.shape[2]

    cvecs = jnp.concatenate(
        [c_ctx[None], c, jnp.zeros((MOD_ROWS - 1 - dec_batch, D_MODEL), F32)], axis=0)
    mod = _modulation(cvecs, w_mod, b_mod).reshape(DEPTH, MOD_ROWS, 1, N_MOD * D_MODEL)

    grp = np.arange(D_ATTN) // HEAD_DIM
    gmat = jnp.asarray(np.where(grp[:, None] == grp[None, :], 1.0 / HEAD_DIM, 0.0), dtype=BF16)
    qg = jnp.tile(q_norm_g, (1, D_ATTN // HEAD_DIM))
    kg = jnp.tile(k_norm_g, (1, D_ATTN // HEAD_DIM))
    n1, n2, cng = norm1_g, norm2_g, conv_norm_g
    sg = jnp.broadcast_to(subln_g[:, :, None], (DEPTH, V_DIM, LANES))
    lam_ps = (lambda_q1, lambda_k1, lambda_q2, lambda_k2)
    rope_tabs = _rope_tables(dec_seq)
    cache = (cache_k.reshape(dec_batch, DEPTH, past, D_ATTN),
             cache_v.reshape(dec_batch, DEPTH, past, D_ATTN))

    ffn_w = {}

    def run(x, row0, mod_span, seq_len, tq, ffn_halo, rope, cache, first_stream):
        kv = "new" if first_stream else None
        for l in range(DEPTH):
            lam_init = 0.8 - 0.6 * math.exp(-0.3 * l)
            outs = _proj(x, mod, l, row0, mod_span, seq_len, n1, w_in, qg, kg, gmat, rope, kv)
            q, k, v, u, bg = outs[:5]
            if first_stream:
                kv = tuple(outs[5:])
            x, cast = _attn(x, mod, l, row0, mod_span, seq_len, tq, q, k, v, cache, u, bg, lam_ps,
                            sg, conv_w, cng, w_out, lam_init,
                            ffn_weights=(w_up, w_down) if first_stream else None)
            if first_stream:
                ffn_w[l] = cast
            x = _ffn(x, mod, l, row0, mod_span, seq_len, ffn_halo, n2, ffn_w[l][0], ffn_conv_w,
                     ffn_w[l][1])
        return x, kv

    xp, (new_k, new_v) = run(x_prompt.reshape(batch * seq, D_MODEL), 0, batch * seq, seq,
                             CTX_Q_TILE, False, None, None, True)
    xs, _ = run(x_sample.reshape(dec_batch * dec_seq, D_MODEL), 1, dec_seq, dec_seq, Q_TILE, True,
                rope_tabs, cache, False)

    return (xp.reshape(batch, seq, D_MODEL), xs.reshape(dec_batch, dec_seq, D_MODEL),
            new_k.reshape(batch, DEPTH, seq, N_HEADS, 2, HEAD_DIM),
            new_v.reshape(batch, DEPTH, seq, N_HEADS, V_DIM))
```

```python
import functools
import math

import jax
import jax.numpy as jnp
import numpy as np
from jax import lax
from jax.experimental import pallas as pl
from jax.experimental.pallas import tpu as pltpu

D_MODEL = 1024
DEPTH = 2
GRID_W = 64
D_ATTN = 512
D_CONV = 512
N_HEADS = 4
HEAD_DIM = 64
V_DIM = 128
D_FF = 2816
ROPE_BASE = 10000.0
EPS = 1e-6
N_MOD = 6
D_IN = 3 * D_ATTN + 3 * D_CONV

F32 = jnp.float32
BF16 = jnp.bfloat16

LANES = 128
BF16_ROWS = 16
F32_ROWS = 8
MOD_ROWS = F32_ROWS
_U_PAD = F32_ROWS
VMEM_LIMIT = 60 * 1024 * 1024

FF_CHUNK = 256
N_FF_CHUNKS = D_FF // FF_CHUNK
_U_COLS = FF_CHUNK // LANES
_DOWN_PARTS = (8, N_FF_CHUNKS)
HALO = BF16_ROWS
MOD_TILE = 1536
TOKEN_TILE = 512
FFN_TILE = 512
Q_TILE = 512
CTX_Q_TILE = 512


def _cparams(n_axes, flags=None):
    return pltpu.CompilerParams(
        dimension_semantics=("arbitrary",) * n_axes, vmem_limit_bytes=VMEM_LIMIT, flags=flags)


def _layer_spec(shape, layer):
    nd = len(shape)
    return pl.BlockSpec((None,) + tuple(shape), lambda *_: (layer,) + (0,) * nd,
                        pipeline_mode=pl.Buffered(1))


def _const_spec(shape):
    nd = len(shape)
    return pl.BlockSpec(shape, lambda *_: (0,) * nd, pipeline_mode=pl.Buffered(1))


def _mod_spec(layer, row_of):
    return pl.BlockSpec((None, None, 1, N_MOD * D_MODEL),
                        lambda *idx: (layer, row_of(*idx), 0, 0))


def _rms_rows(x):
    return lax.rsqrt(jnp.mean(x * x, axis=-1, keepdims=True) + EPS)


def _mod_kernel(cc_ref, c_ref, w_ref, b_ref, o_ref):
    pad = jnp.zeros((MOD_ROWS - 1 - c_ref.shape[0], D_MODEL), F32)
    cv = jnp.concatenate([cc_ref[...], c_ref[...], pad], axis=0)
    s = (cv * jax.nn.sigmoid(cv)).astype(BF16)
    m = jnp.dot(s, w_ref[...].astype(BF16), preferred_element_type=F32) + b_ref[...]
    for r in range(MOD_ROWS):
        o_ref[r] = m[r:r + 1, :]


def _modulation(c_ctx, c, w_mod, b_mod):
    tn = MOD_TILE
    n_out = N_MOD * D_MODEL
    return pl.pallas_call(
        _mod_kernel,
        grid=(DEPTH, n_out // tn),
        in_specs=[
            pl.BlockSpec((1, D_MODEL), lambda l, j: (0, 0)),
            pl.BlockSpec(c.shape, lambda l, j: (0, 0)),
            pl.BlockSpec((None, D_MODEL, tn), lambda l, j: (l, 0, j)),
            pl.BlockSpec((None, 1, tn), lambda l, j: (l, 0, j)),
        ],
        out_specs=pl.BlockSpec((None, MOD_ROWS, 1, tn), lambda l, j: (l, 0, 0, j)),
        out_shape=jax.ShapeDtypeStruct((DEPTH, MOD_ROWS, 1, n_out), F32),
        compiler_params=_cparams(2),
        name="adaln_mod",
    )(c_ctx.reshape(1, D_MODEL), c, w_mod, b_mod.reshape(DEPTH, 1, n_out))


def _group_norm_gain(z, gmat_ref, gain):
    zz = (z * z).astype(BF16)
    ms = jnp.concatenate(
        [jnp.dot(zz[:, j:j + 256], gmat_ref[j:j + 256, j:j + 256], preferred_element_type=F32)
         for j in range(0, D_ATTN, 256)], axis=1)
    return z * lax.rsqrt(ms + EPS) * gain


def _rope(z, c_ref, sa_ref, sb_ref):
    c, sa, sb = c_ref[...], sa_ref[...], sb_ref[...]
    cols = []
    for j in range(z.shape[1] // LANES):
        zj = z[:, j * LANES:(j + 1) * LANES]
        hi = pltpu.roll(zj, LANES - HEAD_DIM // 4, axis=1)
        lo = pltpu.roll(zj, HEAD_DIM // 4, axis=1)
        cols.append(zj * c + hi * sa + lo * sb)
    return jnp.concatenate(cols, axis=1)


def _proj_kernel(*refs, rope, emit_f32, seq_len, layer):
    it = iter(refs)
    x_ref, m_ref, g1_ref, w_ref, qg_ref, kg_ref, gmat_ref = (next(it) for _ in range(7))
    if rope:
        c_ref, sa_ref, sb_ref = (next(it) for _ in range(3))
    if emit_f32 == "alias":
        next(it), next(it)
    q_ref, k_ref, v_ref, u_ref, bg_ref = (next(it) for _ in range(5))
    if emit_f32:
        kf_ref, vf_ref = next(it), next(it)

    x = x_ref[...]
    m = m_ref[...]
    sh1 = m[:, 0:D_MODEL]
    sc1 = m[:, D_MODEL:2 * D_MODEL]
    row = slice(layer, layer + 1)
    h = ((x * _rms_rows(x)) * g1_ref[row, :]) * (1.0 + sc1) + sh1
    hb = h.astype(BF16)

    def col(i0, n):
        return jnp.dot(hb, w_ref[:, i0:i0 + n].astype(BF16), preferred_element_type=F32)

    q = _group_norm_gain(col(0, D_ATTN), gmat_ref, qg_ref[row, :])
    k = _group_norm_gain(col(D_ATTN, D_ATTN), gmat_ref, kg_ref[row, :])
    if rope:
        q = _rope(q, c_ref, sa_ref, sb_ref)
        k = _rope(k, c_ref, sa_ref, sb_ref)
    v = col(2 * D_ATTN, D_ATTN)
    q_ref[...] = (q * (HEAD_DIM ** -0.5 * math.log2(math.e))).astype(BF16)
    k_ref[...] = k.astype(BF16)
    v_ref[...] = v.T.astype(BF16)
    if emit_f32 == "alias":
        n_seq = x.shape[0] // seq_len
        kf_ref[...] = k.reshape(n_seq, seq_len, D_ATTN)
        vf_ref[...] = v.reshape(n_seq, seq_len, D_ATTN)
    elif emit_f32:
        n_seq = x.shape[0] // seq_len
        zero = jnp.zeros((n_seq, seq_len, D_ATTN), F32)
        for l in range(DEPTH):
            kf_ref[:, l] = k.reshape(n_seq, seq_len, D_ATTN) if l == layer else zero
            vf_ref[:, l] = v.reshape(n_seq, seq_len, D_ATTN) if l == layer else zero
    bg_ref[...] = col(3 * D_ATTN, D_CONV).astype(BF16)
    cg = col(3 * D_ATTN + D_CONV, D_CONV)
    xc = col(3 * D_ATTN + 2 * D_CONV, D_CONV)
    u_ref[...] = (cg * xc).astype(BF16)


def _proj(x, mod, layer, row0, mod_span, seq_len, norm1_g, w_in, qg, kg, gmat, rope_tabs, kv_out):
    n = x.shape[0]
    tile = TOKEN_TILE
    tiles_per_seq = max(seq_len // tile, 1)
    in_specs = [
        pl.BlockSpec((tile, D_MODEL), lambda i: (i, 0)),
        _mod_spec(layer, lambda i: row0 + (i * tile) // mod_span),
        _const_spec((DEPTH, D_MODEL)),
        _layer_spec((D_MODEL, D_IN), layer),
        _const_spec((DEPTH, D_ATTN)),
        _const_spec((DEPTH, D_ATTN)),
        _const_spec((D_ATTN, D_ATTN)),
    ]
    args = [x, mod, norm1_g, w_in, qg, kg, gmat]
    if rope_tabs is not None:
        for t in rope_tabs:
            in_specs.append(pl.BlockSpec((tile, LANES), lambda i: (i % tiles_per_seq, 0)))
            args.append(t)
    aliases = {}
    emit = False
    if kv_out is not None and kv_out != "new":
        emit = "alias"
        for a in kv_out:
            aliases[len(args)] = 5 + len(aliases)
            in_specs.append(pl.BlockSpec(memory_space=pl.ANY))
            args.append(a)
    elif kv_out == "new":
        emit = True
    tok = jax.ShapeDtypeStruct((n, D_ATTN), BF16)
    tok_spec = pl.BlockSpec((tile, D_ATTN), lambda i: (i, 0))
    out_shape = [tok, tok, jax.ShapeDtypeStruct((D_ATTN, n), BF16), tok, tok]
    out_specs = [tok_spec, tok_spec, pl.BlockSpec((D_ATTN, tile), lambda i: (0, i)), tok_spec,
                 tok_spec]
    if emit:
        n_seq = tile // seq_len
        kv_shape = jax.ShapeDtypeStruct((n // seq_len, DEPTH, seq_len, D_ATTN), F32)
        out_shape += [kv_shape] * 2
        if emit == "alias":
            kv_spec = pl.BlockSpec((n_seq, None, seq_len, D_ATTN), lambda i: (i, layer, 0, 0))
        else:
            kv_spec = pl.BlockSpec((n_seq, DEPTH, seq_len, D_ATTN), lambda i: (i, 0, 0, 0))
        out_specs += [kv_spec] * 2
    return pl.pallas_call(
        functools.partial(_proj_kernel, rope=rope_tabs is not None, emit_f32=emit, seq_len=seq_len,
                          layer=layer),
        grid=(n // tile,),
        in_specs=in_specs,
        out_specs=out_specs,
        out_shape=out_shape,
        input_output_aliases=aliases,
        compiler_params=_cparams(1),
        name="proj",
    )(*args)


def _attn_kernel(*refs, has_cache, cast_ffn, tq, seq_len, layer, lam_init):
    it = iter(refs)
    x_ref, m_ref, q_ref, k_ref, vt_ref = (next(it) for _ in range(5))
    if has_cache:
        kc_ref, vc_ref = next(it), next(it)
    u_ref, up_ref, un_ref, bg_ref = (next(it) for _ in range(4))
    lam_refs = [next(it) for _ in range(4)]
    sg_ref, cw_ref, cg_ref, wo_ref = (next(it) for _ in range(4))
    if cast_ffn:
        wu_ref, wd_ref = next(it), next(it)
    o_ref = next(it)
    if cast_ffn:
        wub_ref, wdb_ref = next(it), next(it)
        wub_ref[...] = wu_ref[...].astype(BF16)
        wdb_ref[...] = wd_ref[...].astype(BF16)
    cu_ref = next(it)

    lq1, lk1, lq2, lk2 = (r[layer:layer + 1, :] for r in lam_refs)
    lam = (jnp.exp(jnp.sum(lq1 * lk1, axis=-1, keepdims=True))
           - jnp.exp(jnp.sum(lq2 * lk2, axis=-1, keepdims=True)) + lam_init)

    nq = min(tq, seq_len)
    if seq_len >= tq:
        groups = [(0, 0, k_ref.shape[0])]
    else:
        groups = [(g * seq_len, g * seq_len, seq_len) for g in range(tq // seq_len)]

    lane = lax.broadcasted_iota(jnp.int32, (1, LANES), 1)
    sub0 = jnp.where(lane < HEAD_DIM, 1.0, 0.0).astype(BF16)
    sub1 = jnp.where(lane < HEAD_DIM, 0.0, 1.0).astype(BF16)
    nt = (((1,), (1,)), ((), ()))
    ones = jnp.ones((BF16_ROWS, groups[0][2]), BF16)
    sgain = jnp.concatenate([sg_ref[...]] * (nq // LANES), axis=1)
    if has_cache:
        nc = kc_ref.shape[0]
        ones_c = jnp.ones((BF16_ROWS, nc), BF16)
        vct = vc_ref[...].T.astype(BF16)

    def scores(job):
        (q0, k0, nk), h = job
        sl = slice(h * LANES, (h + 1) * LANES)
        qh = q_ref[q0:q0 + nq, sl]
        q2 = jnp.concatenate([qh * sub0, qh * sub1], axis=0)
        st = lax.dot_general(k_ref[k0:k0 + nk, sl], q2, nt, preferred_element_type=F32)
        if not has_cache:
            return st, None
        return st, lax.dot_general(kc_ref[:, sl].astype(BF16), q2, nt,
                                   preferred_element_type=F32)

    jobs = [(grp, h) for grp in groups for h in range(N_HEADS)]
    nxt = scores(jobs[0])

    j = pl.program_id(1)
    u = u_ref[...].astype(F32)
    n_cols = D_CONV // LANES
    if seq_len >= tq:
        segs = [(F32_ROWS, tq, 0)]
        tiles_per_seq = seq_len // tq
        prev = jnp.where(j % tiles_per_seq == 0, 0.0,
                         up_ref[BF16_ROWS - 1:BF16_ROWS, :].astype(F32))
        nxt_row = jnp.where(j % tiles_per_seq == tiles_per_seq - 1, 0.0,
                            un_ref[0:1, :].astype(F32))
        for c in range(n_cols):
            lanes = slice(c * LANES, (c + 1) * LANES)
            cu_ref[c, F32_ROWS - 1:F32_ROWS, :] = prev[:, lanes]
            cu_ref[c, F32_ROWS + tq:F32_ROWS + tq + 1, :] = nxt_row[:, lanes]
    else:
        segs = [(_U_PAD + s * (seq_len + _U_PAD), seq_len, s * seq_len)
                for s in range(tq // seq_len)]

        @pl.when(jnp.logical_and(pl.program_id(0) == 0, j == 0))
        def _():
            cu_ref[...] = jnp.zeros_like(cu_ref)

    for c in range(n_cols):
        for b0, ln, r0 in segs:
            cu_ref[c, b0:b0 + ln, :] = u[r0:r0 + ln, c * LANES:(c + 1) * LANES]
    cw = cw_ref[...]
    conv_cols = []
    for c in range(n_cols):
        w3 = cw[:, c * LANES:(c + 1) * LANES]
        conv_cols.append(jnp.concatenate(
            [cu_ref[c, b0 - 1:b0 - 1 + ln, :] * w3[0:1] + cu_ref[c, b0:b0 + ln, :] * w3[1:2]
             + cu_ref[c, b0 + 1:b0 + 1 + ln, :] * w3[2:3] for b0, ln, r0 in segs], axis=0))
    t = bg_ref[...].astype(F32) * jnp.concatenate(conv_cols, axis=1)
    y = ((t * _rms_rows(t)) * cg_ref[layer:layer + 1, :]).astype(BF16)
    y_proj = jnp.dot(y, wo_ref[D_ATTN:, :].astype(BF16), preferred_element_type=F32)

    outs = []
    for n, ((q0, k0, nk), h) in enumerate(jobs):
        sl = slice(h * LANES, (h + 1) * LANES)
        st, sct = nxt
        if n + 1 < len(jobs):
            nxt = scores(jobs[n + 1])
        mx = jnp.max(st, axis=0, keepdims=True)
        if has_cache:
            mx = jnp.maximum(mx, jnp.max(sct, axis=0, keepdims=True))
        e = jnp.exp2(st - mx).astype(BF16)
        oe = jnp.dot(jnp.concatenate([vt_ref[sl, k0:k0 + nk], ones], axis=0), e,
                     preferred_element_type=F32)
        if has_cache:
            ec = jnp.exp2(sct - mx).astype(BF16)
            oe = oe + jnp.dot(jnp.concatenate([vct[sl, :], ones_c], axis=0), ec,
                              preferred_element_type=F32)
        den = oe[V_DIM:V_DIM + 1, :]
        c0 = 1.0 / den[:, :nq]
        c1 = lam / den[:, nq:]
        oh = oe[:V_DIM, :nq] * c0 - oe[:V_DIM, nq:] * c1
        r = lax.rsqrt(jnp.mean(oh * oh, axis=0, keepdims=True) + EPS)
        outs.append(((oh * r) * sgain) * (1.0 - lam_init))
    ot = jnp.concatenate(
        [jnp.concatenate(outs[g * N_HEADS:(g + 1) * N_HEADS], axis=0) for g in range(len(groups))],
        axis=1)
    o = ot.T.astype(BF16)
    mix = y_proj + jnp.dot(o, wo_ref[0:D_ATTN, :].astype(BF16), preferred_element_type=F32)
    g1 = m_ref[...][:, 2 * D_MODEL:3 * D_MODEL]
    o_ref[...] = x_ref[...] + g1 * mix


def _attn(x, mod, layer, row0, mod_span, seq_len, tq, q, k, vt, cache, u, bg, lam_ps, subln_g,
          conv_w, conv_g, w_out, lam_init, ffn_weights=None):
    n = x.shape[0]
    grp_rows = max(seq_len, tq)
    qt = grp_rows // tq
    hb = tq // BF16_ROWS
    n_hblocks = n // BF16_ROWS
    tokmap = lambda b, j: (b * qt + j, 0)
    in_specs = [
        pl.BlockSpec((tq, D_MODEL), tokmap),
        _mod_spec(layer, lambda b, j: row0 + (b * grp_rows) // mod_span),
        pl.BlockSpec((tq, D_ATTN), tokmap),
        pl.BlockSpec((grp_rows, D_ATTN), lambda b, j: (b, 0)),
        pl.BlockSpec((D_ATTN, grp_rows), lambda b, j: (0, b)),
    ]
    args = [x, mod, q, k, vt]
    if cache is not None:
        for c in cache:
            p = c.shape[2]
            in_specs.append(pl.BlockSpec((None, None, p, D_ATTN), lambda b, j: (b, layer, 0, 0)))
            args.append(c)
    in_specs += [
        pl.BlockSpec((tq, D_CONV), tokmap),
        pl.BlockSpec((BF16_ROWS, D_CONV), lambda b, j: (jnp.maximum((b * qt + j) * hb - 1, 0), 0)),
        pl.BlockSpec((BF16_ROWS, D_CONV),
                     lambda b, j: (jnp.minimum((b * qt + j + 1) * hb, n_hblocks - 1), 0)),
        pl.BlockSpec((tq, D_CONV), tokmap),
    ]
    in_specs += [_const_spec((DEPTH, HEAD_DIM))] * 4
    in_specs += [
        _layer_spec((V_DIM, LANES), layer),
        _layer_spec((3, D_CONV), layer),
        _const_spec((DEPTH, D_CONV)),
        _layer_spec((D_MODEL, D_MODEL), layer),
    ]
    args += [u, u, u, bg, *lam_ps, subln_g, conv_w, conv_g, w_out]
    out_specs = [pl.BlockSpec((tq, D_MODEL), tokmap)]
    out_shape = [jax.ShapeDtypeStruct(x.shape, F32)]
    if ffn_weights is not None:
        n_steps = n // tq
        for w in ffn_weights:
            rows, cols = w.shape[1:]
            slab = rows // n_steps
            assert slab * n_steps == rows and slab % BF16_ROWS == 0, (rows, n_steps)
            in_specs.append(pl.BlockSpec((None, slab, cols), lambda b, j: (layer, b * qt + j, 0)))
            out_specs.append(pl.BlockSpec((slab, cols), lambda b, j: (b * qt + j, 0)))
            out_shape.append(jax.ShapeDtypeStruct((rows, cols), BF16))
            args.append(w)
    if seq_len >= tq:
        rows_c = tq + 2 * F32_ROWS
    else:
        rows_c = _U_PAD + (tq // seq_len) * (seq_len + _U_PAD)
    outs = pl.pallas_call(
        functools.partial(_attn_kernel, has_cache=cache is not None,
                          cast_ffn=ffn_weights is not None, tq=tq, seq_len=seq_len, layer=layer,
                          lam_init=lam_init),
        grid=(n // grp_rows, qt),
        in_specs=in_specs,
        out_specs=out_specs,
        out_shape=out_shape,
        scratch_shapes=[pltpu.VMEM((D_CONV // LANES, rows_c, LANES), F32)],
        compiler_params=_cparams(2),
        name="attn",
    )(*args)
    return outs[0], tuple(outs[1:])


def _ffn_kernel(*refs, tile, seq_len, halo, layer):
    it = iter(refs)
    x_ref = next(it)
    if halo:
        xp_ref, xn_ref = next(it), next(it)
    m_ref, g2_ref, wup_ref, cw_ref, wdn_ref, o_ref, hs_ref, u_ref, g_ref = (
        next(it) for _ in range(9))

    m = m_ref[...]
    sh2 = m[:, 3 * D_MODEL:4 * D_MODEL]
    sc2 = m[:, 4 * D_MODEL:5 * D_MODEL]
    gate = m[:, 5 * D_MODEL:6 * D_MODEL]

    gain = g2_ref[layer:layer + 1, :] * (1.0 + sc2)

    def pre(xv):
        return (xv * _rms_rows(xv)) * gain + sh2

    x = x_ref[...]
    i = pl.program_id(0)
    base = HALO if halo else 0
    hs_ref[base:base + tile, :] = pre(x).astype(BF16)
    if halo:
        tiles_per_seq = seq_len // tile
        jt = i % tiles_per_seq
        hs_ref[0:HALO, :] = jnp.where(jt == 0, 0.0, pre(xp_ref[...])).astype(BF16)
        hs_ref[HALO + tile:, :] = jnp.where(jt == tiles_per_seq - 1, 0.0,
                                            pre(xn_ref[...])).astype(BF16)
    if halo:
        segs = [(HALO, tile, 0)]
    else:
        segs = [(_U_PAD + s * (seq_len + _U_PAD), seq_len, s * seq_len)
                for s in range(tile // seq_len)]

        @pl.when(i == 0)
        def _():
            u_ref[...] = jnp.zeros_like(u_ref)

    hs = hs_ref[...]
    acc, done = None, 0
    for c in range(N_FF_CHUNKS):
        slot = c % 2
        cols = (c * FF_CHUNK, D_FF + c * FF_CHUNK)
        for half, c0 in enumerate(cols):
            uu = jnp.dot(hs, wup_ref[:, c0:c0 + FF_CHUNK], preferred_element_type=F32)
            for j in range(_U_COLS):
                col = uu[:, j * LANES:(j + 1) * LANES]
                if halo:
                    u_ref[slot, half * _U_COLS + j] = col
                else:
                    for b0, ln, r0 in segs:
                        u_ref[slot, half * _U_COLS + j, b0:b0 + ln, :] = col[r0:r0 + ln]
        for b0, ln, r0 in segs:
            conv = []
            for half, c0 in enumerate(cols):
                w3 = cw_ref[:, c0:c0 + FF_CHUNK]
                taps = []
                for j in range(_U_COLS):
                    uc = u_ref.at[slot, half * _U_COLS + j]
                    w3j = w3[:, j * LANES:(j + 1) * LANES]
                    taps.append(uc[b0 - 1:b0 - 1 + ln, :] * w3j[0:1]
                                + uc[b0:b0 + ln, :] * w3j[1:2]
                                + uc[b0 + 1:b0 + 1 + ln, :] * w3j[2:3])
                conv.append(jnp.concatenate(taps, axis=1))
            ca, cb = conv
            g_ref[r0:r0 + ln, c * FF_CHUNK:(c + 1) * FF_CHUNK] = (
                (ca * jax.nn.sigmoid(ca)) * cb).astype(BF16)
        if c + 1 in _DOWN_PARTS:
            k0, k1 = done * FF_CHUNK, (c + 1) * FF_CHUNK
            part = [jnp.dot(g_ref[:, k0:k1], wdn_ref[k0:k1, n0:n0 + FF_CHUNK],
                            preferred_element_type=F32) for n0 in range(0, D_MODEL, FF_CHUNK)]
            acc = part if acc is None else [a + p for a, p in zip(acc, part)]
            done = c + 1

    for i0, n0 in enumerate(range(0, D_MODEL, FF_CHUNK)):
        o_ref[:, n0:n0 + FF_CHUNK] = x[:, n0:n0 + FF_CHUNK] + gate[:, n0:n0 + FF_CHUNK] * acc[i0]


def _ffn(x, mod, layer, row0, mod_span, seq_len, halo, norm2_g, w_up_b, conv_w, w_down_b):
    n = x.shape[0]
    tile = FFN_TILE
    hb = tile // BF16_ROWS
    n_hblocks = n // BF16_ROWS
    in_specs = [pl.BlockSpec((tile, D_MODEL), lambda i: (i, 0))]
    args = [x]
    if halo:
        in_specs += [
            pl.BlockSpec((HALO, D_MODEL), lambda i: (jnp.maximum(i * hb - 1, 0), 0)),
            pl.BlockSpec((HALO, D_MODEL), lambda i: (jnp.minimum((i + 1) * hb, n_hblocks - 1), 0)),
        ]
        args += [x, x]
    in_specs += [
        _mod_spec(layer, lambda i: row0 + (i * tile) // mod_span),
        _const_spec((DEPTH, D_MODEL)),
        _const_spec((D_MODEL, 2 * D_FF)),
        _layer_spec((3, 2 * D_FF), layer),
        _const_spec((D_FF, D_MODEL)),
    ]
    args += [mod, norm2_g, w_up_b, conv_w, w_down_b]
    if halo:
        rows_h = rows_u = tile + 2 * HALO
    else:
        rows_h = tile
        rows_u = _U_PAD + (tile // seq_len) * (seq_len + _U_PAD)
    return pl.pallas_call(
        functools.partial(_ffn_kernel, tile=tile, seq_len=seq_len, halo=halo, layer=layer),
        grid=(n // tile,),
        in_specs=in_specs,
        out_specs=pl.BlockSpec((tile, D_MODEL), lambda i: (i, 0)),
        out_shape=jax.ShapeDtypeStruct(x.shape, F32),
        scratch_shapes=[pltpu.VMEM((rows_h, D_MODEL), BF16),
                        pltpu.VMEM((2, 2 * _U_COLS, rows_u, LANES), F32),
                        pltpu.VMEM((tile, D_FF), BF16)],
        compiler_params=_cparams(1),
        name="ffn",
    )(*args)


def _rope_tables(n_tok):
    rows = n_tok // GRID_W
    row = np.repeat(np.arange(rows, dtype=np.float32), GRID_W)
    col = np.tile(np.arange(GRID_W, dtype=np.float32), rows)
    n_freq = HEAD_DIM // 4
    inv = (np.float32(ROPE_BASE) ** (-np.arange(n_freq, dtype=np.float32) / np.float32(n_freq)))
    inv = inv.astype(np.float32)
    ar, ac = row[:, None] * inv[None], col[:, None] * inv[None]
    cr, sr, cc, sc = np.cos(ar), np.sin(ar), np.cos(ac), np.sin(ac)
    z = np.zeros_like(sr)
    rep = lambda parts: jnp.asarray(
        np.tile(np.concatenate(parts, axis=-1), (1, LANES // HEAD_DIM)).astype(np.float32))
    return rep([cr, cr, cc, cc]), rep([-sr, z, -sc, z]), rep([z, sr, z, sc])


def kernel(x_prompt, x_sample, cache_k, cache_v, c, c_ctx, w_mod, b_mod, norm1_g, w_in, q_norm_g, k_norm_g, lambda_q1, lambda_k1, lambda_q2, lambda_k2, subln_g, conv_w, conv_norm_g, w_out, norm2_g, w_up, ffn_conv_w, w_down):
    batch, seq, _ = x_prompt.shape
    dec_batch, dec_seq, _ = x_sample.shape
    past = cache_k.shape[2]

    mod = _modulation(c_ctx, c, w_mod, b_mod)

    grp = np.arange(D_ATTN) // HEAD_DIM
    gmat = jnp.asarray(np.where(grp[:, None] == grp[None, :], 1.0 / HEAD_DIM, 0.0), dtype=BF16)
    qg = jnp.tile(q_norm_g, (1, D_ATTN // HEAD_DIM))
    kg = jnp.tile(k_norm_g, (1, D_ATTN // HEAD_DIM))
    n1, n2, cng = norm1_g, norm2_g, conv_norm_g
    sg = jnp.broadcast_to(subln_g[:, :, None], (DEPTH, V_DIM, LANES))
    lam_ps = (lambda_q1, lambda_k1, lambda_q2, lambda_k2)
    rope_tabs = _rope_tables(dec_seq)
    cache = (cache_k.reshape(dec_batch, DEPTH, past, D_ATTN),
             cache_v.reshape(dec_batch, DEPTH, past, D_ATTN))

    ffn_w = {}

    def run(x, row0, mod_span, seq_len, tq, ffn_halo, rope, cache, first_stream):
        kv = "new" if first_stream else None
        for l in range(DEPTH):
            lam_init = 0.8 - 0.6 * math.exp(-0.3 * l)
            outs = _proj(x, mod, l, row0, mod_span, seq_len, n1, w_in, qg, kg, gmat, rope, kv)
            q, k, v, u, bg = outs[:5]
            if first_stream:
                kv = tuple(outs[5:])
            x, cast = _attn(x, mod, l, row0, mod_span, seq_len, tq, q, k, v, cache, u, bg, lam_ps,
                            sg, conv_w, cng, w_out, lam_init,
                            ffn_weights=(w_up, w_down) if first_stream else None)
            if first_stream:
                ffn_w[l] = cast
            x = _ffn(x, mod, l, row0, mod_span, seq_len, ffn_halo, n2, ffn_w[l][0], ffn_conv_w,
                     ffn_w[l][1])
        return x, kv

    xp, (new_k, new_v) = run(x_prompt.reshape(batch * seq, D_MODEL), 0, batch * seq, seq,
                             CTX_Q_TILE, False, None, None, True)
    xs, _ = run(x_sample.reshape(dec_batch * dec_seq, D_MODEL), 1, dec_seq, dec_seq, Q_TILE, True,
                rope_tabs, cache, False)

    return (xp.reshape(batch, seq, D_MODEL), xs.reshape(dec_batch, dec_seq, D_MODEL),
            new_k.reshape(batch, DEPTH, seq, N_HEADS, 2, HEAD_DIM),
            new_v.reshape(batch, DEPTH, seq, N_HEADS, V_DIM))
```

```python
import functools
import math

import jax
import jax.numpy as jnp
import numpy as np
from jax import lax
from jax.experimental import pallas as pl
from jax.experimental.pallas import tpu as pltpu

D_MODEL = 1024
DEPTH = 2
GRID_W = 64
D_ATTN = 512
D_CONV = 512
N_HEADS = 4
HEAD_DIM = 64
V_DIM = 128
D_FF = 2816
ROPE_BASE = 10000.0
EPS = 1e-6
N_MOD = 6
D_IN = 3 * D_ATTN + 3 * D_CONV

F32 = jnp.float32
BF16 = jnp.bfloat16

LANES = 128
BF16_ROWS = 16
F32_ROWS = 8
MOD_ROWS = F32_ROWS
_U_PAD = F32_ROWS
VMEM_LIMIT = 60 * 1024 * 1024

FF_CHUNK = 256
N_FF_CHUNKS = D_FF // FF_CHUNK
_U_COLS = FF_CHUNK // LANES
_DOWN_PARTS = (8, N_FF_CHUNKS)
HALO = BF16_ROWS
MOD_TILE = 1536
TOKEN_TILE = 512
FFN_TILE = 512
Q_TILE = 512
CTX_Q_TILE = 512


def _cparams(n_axes, flags=None):
    return pltpu.CompilerParams(
        dimension_semantics=("arbitrary",) * n_axes, vmem_limit_bytes=VMEM_LIMIT, flags=flags)


def _layer_spec(shape, layer):
    nd = len(shape)
    return pl.BlockSpec((None,) + tuple(shape), lambda *_: (layer,) + (0,) * nd,
                        pipeline_mode=pl.Buffered(1))


def _const_spec(shape):
    nd = len(shape)
    return pl.BlockSpec(shape, lambda *_: (0,) * nd, pipeline_mode=pl.Buffered(1))


def _mod_spec(layer, row_of):
    return pl.BlockSpec((None, None, 1, N_MOD * D_MODEL),
                        lambda *idx: (layer, row_of(*idx), 0, 0))


def _rms_rows(x):
    return lax.rsqrt(jnp.mean(x * x, axis=-1, keepdims=True) + EPS)


def _mod_kernel(cv_ref, w_ref, b_ref, o_ref):
    cv = cv_ref[...]
    s = (cv * jax.nn.sigmoid(cv)).astype(BF16)
    o_ref[...] = jnp.dot(s, w_ref[...].astype(BF16), preferred_element_type=F32) + b_ref[...]


def _modulation(cvecs, w_mod, b_mod):
    tn = MOD_TILE
    n_out = N_MOD * D_MODEL
    return pl.pallas_call(
        _mod_kernel,
        grid=(DEPTH, n_out // tn),
        in_specs=[
            pl.BlockSpec((MOD_ROWS, D_MODEL), lambda l, j: (0, 0)),
            pl.BlockSpec((None, D_MODEL, tn), lambda l, j: (l, 0, j)),
            pl.BlockSpec((None, 1, tn), lambda l, j: (l, 0, j)),
        ],
        out_specs=pl.BlockSpec((None, MOD_ROWS, tn), lambda l, j: (l, 0, j)),
        out_shape=jax.ShapeDtypeStruct((DEPTH, MOD_ROWS, n_out), F32),
        compiler_params=_cparams(2),
        name="adaln_mod",
    )(cvecs, w_mod, b_mod.reshape(DEPTH, 1, n_out))


def _group_norm_gain(z, gmat_ref, gain):
    zz = (z * z).astype(BF16)
    ms = jnp.concatenate(
        [jnp.dot(zz[:, j:j + 256], gmat_ref[j:j + 256, j:j + 256], preferred_element_type=F32)
         for j in range(0, D_ATTN, 256)], axis=1)
    return z * lax.rsqrt(ms + EPS) * gain


def _rope(z, c_ref, sa_ref, sb_ref):
    c, sa, sb = c_ref[...], sa_ref[...], sb_ref[...]
    cols = []
    for j in range(z.shape[1] // LANES):
        zj = z[:, j * LANES:(j + 1) * LANES]
        hi = pltpu.roll(zj, LANES - HEAD_DIM // 4, axis=1)
        lo = pltpu.roll(zj, HEAD_DIM // 4, axis=1)
        cols.append(zj * c + hi * sa + lo * sb)
    return jnp.concatenate(cols, axis=1)


def _proj_kernel(*refs, rope, emit_f32, seq_len, layer):
    it = iter(refs)
    x_ref, m_ref, g1_ref, w_ref, qg_ref, kg_ref, gmat_ref = (next(it) for _ in range(7))
    if rope:
        c_ref, sa_ref, sb_ref = (next(it) for _ in range(3))
    if emit_f32 == "alias":
        next(it), next(it)
    q_ref, k_ref, v_ref, u_ref, bg_ref = (next(it) for _ in range(5))
    if emit_f32:
        kf_ref, vf_ref = next(it), next(it)

    x = x_ref[...]
    m = m_ref[...]
    sh1 = m[:, 0:D_MODEL]
    sc1 = m[:, D_MODEL:2 * D_MODEL]
    row = slice(layer, layer + 1)
    h = ((x * _rms_rows(x)) * g1_ref[row, :]) * (1.0 + sc1) + sh1
    hb = h.astype(BF16)

    def col(i0, n):
        return jnp.dot(hb, w_ref[:, i0:i0 + n].astype(BF16), preferred_element_type=F32)

    q = _group_norm_gain(col(0, D_ATTN), gmat_ref, qg_ref[row, :])
    k = _group_norm_gain(col(D_ATTN, D_ATTN), gmat_ref, kg_ref[row, :])
    if rope:
        q = _rope(q, c_ref, sa_ref, sb_ref)
        k = _rope(k, c_ref, sa_ref, sb_ref)
    v = col(2 * D_ATTN, D_ATTN)
    q_ref[...] = (q * (HEAD_DIM ** -0.5 * math.log2(math.e))).astype(BF16)
    k_ref[...] = k.astype(BF16)
    v_ref[...] = v.T.astype(BF16)
    if emit_f32 == "alias":
        n_seq = x.shape[0] // seq_len
        kf_ref[...] = k.reshape(n_seq, seq_len, D_ATTN)
        vf_ref[...] = v.reshape(n_seq, seq_len, D_ATTN)
    elif emit_f32:
        n_seq = x.shape[0] // seq_len
        zero = jnp.zeros((n_seq, seq_len, D_ATTN), F32)
        for l in range(DEPTH):
            kf_ref[:, l] = k.reshape(n_seq, seq_len, D_ATTN) if l == layer else zero
            vf_ref[:, l] = v.reshape(n_seq, seq_len, D_ATTN) if l == layer else zero
    bg_ref[...] = col(3 * D_ATTN, D_CONV).astype(BF16)
    cg = col(3 * D_ATTN + D_CONV, D_CONV)
    xc = col(3 * D_ATTN + 2 * D_CONV, D_CONV)
    u_ref[...] = (cg * xc).astype(BF16)


def _proj(x, mod, layer, row0, mod_span, seq_len, norm1_g, w_in, qg, kg, gmat, rope_tabs, kv_out):
    n = x.shape[0]
    tile = TOKEN_TILE if kv_out is not None else 2 * TOKEN_TILE
    tiles_per_seq = max(seq_len // tile, 1)
    in_specs = [
        pl.BlockSpec((tile, D_MODEL), lambda i: (i, 0)),
        _mod_spec(layer, lambda i: row0 + (i * tile) // mod_span),
        _const_spec((DEPTH, D_MODEL)),
        _layer_spec((D_MODEL, D_IN), layer),
        _const_spec((DEPTH, D_ATTN)),
        _const_spec((DEPTH, D_ATTN)),
        _const_spec((D_ATTN, D_ATTN)),
    ]
    args = [x, mod, norm1_g, w_in, qg, kg, gmat]
    if rope_tabs is not None:
        for t in rope_tabs:
            in_specs.append(pl.BlockSpec((tile, LANES), lambda i: (i % tiles_per_seq, 0)))
            args.append(t)
    aliases = {}
    emit = False
    if kv_out is not None and kv_out != "new":
        emit = "alias"
        for a in kv_out:
            aliases[len(args)] = 5 + len(aliases)
            in_specs.append(pl.BlockSpec(memory_space=pl.ANY))
            args.append(a)
    elif kv_out == "new":
        emit = True
    tok = jax.ShapeDtypeStruct((n, D_ATTN), BF16)
    tok_spec = pl.BlockSpec((tile, D_ATTN), lambda i: (i, 0))
    out_shape = [tok, tok, jax.ShapeDtypeStruct((D_ATTN, n), BF16), tok, tok]
    out_specs = [tok_spec, tok_spec, pl.BlockSpec((D_ATTN, tile), lambda i: (0, i)), tok_spec,
                 tok_spec]
    if emit:
        n_seq = tile // seq_len
        kv_shape = jax.ShapeDtypeStruct((n // seq_len, DEPTH, seq_len, D_ATTN), F32)
        out_shape += [kv_shape] * 2
        if emit == "alias":
            kv_spec = pl.BlockSpec((n_seq, None, seq_len, D_ATTN), lambda i: (i, layer, 0, 0))
        else:
            kv_spec = pl.BlockSpec((n_seq, DEPTH, seq_len, D_ATTN), lambda i: (i, 0, 0, 0))
        out_specs += [kv_spec] * 2
    return pl.pallas_call(
        functools.partial(_proj_kernel, rope=rope_tabs is not None, emit_f32=emit, seq_len=seq_len,
                          layer=layer),
        grid=(n // tile,),
        in_specs=in_specs,
        out_specs=out_specs,
        out_shape=out_shape,
        input_output_aliases=aliases,
        compiler_params=_cparams(1),
        name="proj",
    )(*args)


def _attn_kernel(*refs, has_cache, cast_ffn, tq, seq_len, layer, lam_init):
    it = iter(refs)
    x_ref, m_ref, q_ref, k_ref, vt_ref = (next(it) for _ in range(5))
    if has_cache:
        kc_ref, vc_ref = next(it), next(it)
    u_ref, up_ref, un_ref, bg_ref = (next(it) for _ in range(4))
    lam_refs = [next(it) for _ in range(4)]
    sg_ref, cw_ref, cg_ref, wo_ref = (next(it) for _ in range(4))
    if cast_ffn:
        wu_ref, wd_ref = next(it), next(it)
    o_ref = next(it)
    if cast_ffn:
        wub_ref, wdb_ref = next(it), next(it)
        wub_ref[...] = wu_ref[...].astype(BF16)
        wdb_ref[...] = wd_ref[...].astype(BF16)
    cu_ref = next(it)

    lq1, lk1, lq2, lk2 = (r[layer:layer + 1, :] for r in lam_refs)
    lam = (jnp.exp(jnp.sum(lq1 * lk1, axis=-1, keepdims=True))
           - jnp.exp(jnp.sum(lq2 * lk2, axis=-1, keepdims=True)) + lam_init)

    nq = min(tq, seq_len)
    if seq_len >= tq:
        groups = [(0, 0, k_ref.shape[0])]
    else:
        groups = [(g * seq_len, g * seq_len, seq_len) for g in range(tq // seq_len)]

    lane = lax.broadcasted_iota(jnp.int32, (1, LANES), 1)
    sub0 = jnp.where(lane < HEAD_DIM, 1.0, 0.0).astype(BF16)
    sub1 = jnp.where(lane < HEAD_DIM, 0.0, 1.0).astype(BF16)
    nt = (((1,), (1,)), ((), ()))
    ones = jnp.ones((BF16_ROWS, groups[0][2]), BF16)
    sgain = jnp.concatenate([sg_ref[...]] * (nq // LANES), axis=1)
    if has_cache:
        nc = kc_ref.shape[0]
        ones_c = jnp.ones((BF16_ROWS, nc), BF16)
        vct = vc_ref[...].T.astype(BF16)

    def scores(job):
        (q0, k0, nk), h = job
        sl = slice(h * LANES, (h + 1) * LANES)
        qh = q_ref[q0:q0 + nq, sl]
        q2 = jnp.concatenate([qh * sub0, qh * sub1], axis=0)
        st = lax.dot_general(k_ref[k0:k0 + nk, sl], q2, nt, preferred_element_type=F32)
        if not has_cache:
            return st, None
        return st, lax.dot_general(kc_ref[:, sl].astype(BF16), q2, nt,
                                   preferred_element_type=F32)

    jobs = [(grp, h) for grp in groups for h in range(N_HEADS)]
    nxt = scores(jobs[0])

    j = pl.program_id(1)
    u = u_ref[...].astype(F32)
    n_cols = D_CONV // LANES
    if seq_len >= tq:
        segs = [(F32_ROWS, tq, 0)]
        tiles_per_seq = seq_len // tq
        prev = jnp.where(j % tiles_per_seq == 0, 0.0,
                         up_ref[BF16_ROWS - 1:BF16_ROWS, :].astype(F32))
        nxt_row = jnp.where(j % tiles_per_seq == tiles_per_seq - 1, 0.0,
                            un_ref[0:1, :].astype(F32))
        for c in range(n_cols):
            lanes = slice(c * LANES, (c + 1) * LANES)
            cu_ref[c, F32_ROWS - 1:F32_ROWS, :] = prev[:, lanes]
            cu_ref[c, F32_ROWS + tq:F32_ROWS + tq + 1, :] = nxt_row[:, lanes]
    else:
        segs = [(_U_PAD + s * (seq_len + _U_PAD), seq_len, s * seq_len)
                for s in range(tq // seq_len)]

        @pl.when(jnp.logical_and(pl.program_id(0) == 0, j == 0))
        def _():
            cu_ref[...] = jnp.zeros_like(cu_ref)

    for c in range(n_cols):
        for b0, ln, r0 in segs:
            cu_ref[c, b0:b0 + ln, :] = u[r0:r0 + ln, c * LANES:(c + 1) * LANES]
    cw = cw_ref[...]
    conv_cols = []
    for c in range(n_cols):
        w3 = cw[:, c * LANES:(c + 1) * LANES]
        conv_cols.append(jnp.concatenate(
            [cu_ref[c, b0 - 1:b0 - 1 + ln, :] * w3[0:1] + cu_ref[c, b0:b0 + ln, :] * w3[1:2]
             + cu_ref[c, b0 + 1:b0 + 1 + ln, :] * w3[2:3] for b0, ln, r0 in segs], axis=0))
    t = bg_ref[...].astype(F32) * jnp.concatenate(conv_cols, axis=1)
    y = ((t * _rms_rows(t)) * cg_ref[layer:layer + 1, :]).astype(BF16)
    y_proj = jnp.dot(y, wo_ref[D_ATTN:, :].astype(BF16), preferred_element_type=F32)

    outs = []
    for n, ((q0, k0, nk), h) in enumerate(jobs):
        sl = slice(h * LANES, (h + 1) * LANES)
        st, sct = nxt
        if n + 1 < len(jobs):
            nxt = scores(jobs[n + 1])
        mx = jnp.max(st, axis=0, keepdims=True)
        if has_cache:
            mx = jnp.maximum(mx, jnp.max(sct, axis=0, keepdims=True))
        e = jnp.exp2(st - mx).astype(BF16)
        oe = jnp.dot(jnp.concatenate([vt_ref[sl, k0:k0 + nk], ones], axis=0), e,
                     preferred_element_type=F32)
        if has_cache:
            ec = jnp.exp2(sct - mx).astype(BF16)
            oe = oe + jnp.dot(jnp.concatenate([vct[sl, :], ones_c], axis=0), ec,
                              preferred_element_type=F32)
        den = oe[V_DIM:V_DIM + 1, :]
        c0 = 1.0 / den[:, :nq]
        c1 = lam / den[:, nq:]
        oh = oe[:V_DIM, :nq] * c0 - oe[:V_DIM, nq:] * c1
        r = lax.rsqrt(jnp.mean(oh * oh, axis=0, keepdims=True) + EPS)
        outs.append(((oh * r) * sgain) * (1.0 - lam_init))
    ot = jnp.concatenate(
        [jnp.concatenate(outs[g * N_HEADS:(g + 1) * N_HEADS], axis=0) for g in range(len(groups))],
        axis=1)
    o = ot.T.astype(BF16)
    mix = y_proj + jnp.dot(o, wo_ref[0:D_ATTN, :].astype(BF16), preferred_element_type=F32)
    g1 = m_ref[...][:, 2 * D_MODEL:3 * D_MODEL]
    o_ref[...] = x_ref[...] + g1 * mix


def _attn(x, mod, layer, row0, mod_span, seq_len, tq, q, k, vt, cache, u, bg, lam_ps, subln_g,
          conv_w, conv_g, w_out, lam_init, ffn_weights=None):
    n = x.shape[0]
    grp_rows = max(seq_len, tq)
    qt = grp_rows // tq
    hb = tq // BF16_ROWS
    n_hblocks = n // BF16_ROWS
    tokmap = lambda b, j: (b * qt + j, 0)
    in_specs = [
        pl.BlockSpec((tq, D_MODEL), tokmap),
        _mod_spec(layer, lambda b, j: row0 + (b * grp_rows) // mod_span),
        pl.BlockSpec((tq, D_ATTN), tokmap),
        pl.BlockSpec((grp_rows, D_ATTN), lambda b, j: (b, 0)),
        pl.BlockSpec((D_ATTN, grp_rows), lambda b, j: (0, b)),
    ]
    args = [x, mod, q, k, vt]
    if cache is not None:
        for c in cache:
            p = c.shape[2]
            in_specs.append(pl.BlockSpec((None, None, p, D_ATTN), lambda b, j: (b, layer, 0, 0)))
            args.append(c)
    in_specs += [
        pl.BlockSpec((tq, D_CONV), tokmap),
        pl.BlockSpec((BF16_ROWS, D_CONV), lambda b, j: (jnp.maximum((b * qt + j) * hb - 1, 0), 0)),
        pl.BlockSpec((BF16_ROWS, D_CONV),
                     lambda b, j: (jnp.minimum((b * qt + j + 1) * hb, n_hblocks - 1), 0)),
        pl.BlockSpec((tq, D_CONV), tokmap),
    ]
    in_specs += [_const_spec((DEPTH, HEAD_DIM))] * 4
    in_specs += [
        _layer_spec((V_DIM, LANES), layer),
        _layer_spec((3, D_CONV), layer),
        _const_spec((DEPTH, D_CONV)),
        _layer_spec((D_MODEL, D_MODEL), layer),
    ]
    args += [u, u, u, bg, *lam_ps, subln_g, conv_w, conv_g, w_out]
    out_specs = [pl.BlockSpec((tq, D_MODEL), tokmap)]
    out_shape = [jax.ShapeDtypeStruct(x.shape, F32)]
    if ffn_weights is not None:
        n_steps = n // tq
        for w in ffn_weights:
            rows, cols = w.shape[1:]
            slab = rows // n_steps
            assert slab * n_steps == rows and slab % BF16_ROWS == 0, (rows, n_steps)
            in_specs.append(pl.BlockSpec((None, slab, cols), lambda b, j: (layer, b * qt + j, 0)))
            out_specs.append(pl.BlockSpec((slab, cols), lambda b, j: (b * qt + j, 0)))
            out_shape.append(jax.ShapeDtypeStruct((rows, cols), BF16))
            args.append(w)
    if seq_len >= tq:
        rows_c = tq + 2 * F32_ROWS
    else:
        rows_c = _U_PAD + (tq // seq_len) * (seq_len + _U_PAD)
    outs = pl.pallas_call(
        functools.partial(_attn_kernel, has_cache=cache is not None,
                          cast_ffn=ffn_weights is not None, tq=tq, seq_len=seq_len, layer=layer,
                          lam_init=lam_init),
        grid=(n // grp_rows, qt),
        in_specs=in_specs,
        out_specs=out_specs,
        out_shape=out_shape,
        scratch_shapes=[pltpu.VMEM((D_CONV // LANES, rows_c, LANES), F32)],
        compiler_params=_cparams(2),
        name="attn",
    )(*args)
    return outs[0], tuple(outs[1:])


def _ffn_kernel(*refs, tile, seq_len, halo, layer):
    it = iter(refs)
    x_ref = next(it)
    if halo:
        xp_ref, xn_ref = next(it), next(it)
    m_ref, g2_ref, wup_ref, cw_ref, wdn_ref, o_ref, hs_ref, u_ref, g_ref = (
        next(it) for _ in range(9))

    m = m_ref[...]
    sh2 = m[:, 3 * D_MODEL:4 * D_MODEL]
    sc2 = m[:, 4 * D_MODEL:5 * D_MODEL]
    gate = m[:, 5 * D_MODEL:6 * D_MODEL]

    gain = g2_ref[layer:layer + 1, :] * (1.0 + sc2)

    def pre(xv):
        return (xv * _rms_rows(xv)) * gain + sh2

    x = x_ref[...]
    i = pl.program_id(0)
    base = HALO if halo else 0
    hs_ref[base:base + tile, :] = pre(x).astype(BF16)
    if halo:
        tiles_per_seq = seq_len // tile
        jt = i % tiles_per_seq
        hs_ref[0:HALO, :] = jnp.where(jt == 0, 0.0, pre(xp_ref[...])).astype(BF16)
        hs_ref[HALO + tile:, :] = jnp.where(jt == tiles_per_seq - 1, 0.0,
                                            pre(xn_ref[...])).astype(BF16)
    if halo:
        segs = [(HALO, tile, 0)]
    else:
        segs = [(_U_PAD + s * (seq_len + _U_PAD), seq_len, s * seq_len)
                for s in range(tile // seq_len)]

        @pl.when(i == 0)
        def _():
            u_ref[...] = jnp.zeros_like(u_ref)

    hs = hs_ref[...]
    acc, done = None, 0
    for c in range(N_FF_CHUNKS):
        slot = c % 2
        cols = (c * FF_CHUNK, D_FF + c * FF_CHUNK)
        for half, c0 in enumerate(cols):
            uu = jnp.dot(hs, wup_ref[:, c0:c0 + FF_CHUNK], preferred_element_type=F32)
            for j in range(_U_COLS):
                col = uu[:, j * LANES:(j + 1) * LANES]
                if halo:
                    u_ref[slot, half * _U_COLS + j] = col
                else:
                    for b0, ln, r0 in segs:
                        u_ref[slot, half * _U_COLS + j, b0:b0 + ln, :] = col[r0:r0 + ln]
        for b0, ln, r0 in segs:
            conv = []
            for half, c0 in enumerate(cols):
                w3 = cw_ref[:, c0:c0 + FF_CHUNK]
                taps = []
                for j in range(_U_COLS):
                    uc = u_ref.at[slot, half * _U_COLS + j]
                    w3j = w3[:, j * LANES:(j + 1) * LANES]
                    taps.append(uc[b0 - 1:b0 - 1 + ln, :] * w3j[0:1]
                                + uc[b0:b0 + ln, :] * w3j[1:2]
                                + uc[b0 + 1:b0 + 1 + ln, :] * w3j[2:3])
                conv.append(jnp.concatenate(taps, axis=1))
            ca, cb = conv
            g_ref[r0:r0 + ln, c * FF_CHUNK:(c + 1) * FF_CHUNK] = (
                (ca * jax.nn.sigmoid(ca)) * cb).astype(BF16)
        if c + 1 in _DOWN_PARTS:
            k0, k1 = done * FF_CHUNK, (c + 1) * FF_CHUNK
            part = [jnp.dot(g_ref[:, k0:k1], wdn_ref[k0:k1, n0:n0 + FF_CHUNK],
                            preferred_element_type=F32) for n0 in range(0, D_MODEL, FF_CHUNK)]
            acc = part if acc is None else [a + p for a, p in zip(acc, part)]
            done = c + 1

    for i0, n0 in enumerate(range(0, D_MODEL, FF_CHUNK)):
        o_ref[:, n0:n0 + FF_CHUNK] = x[:, n0:n0 + FF_CHUNK] + gate[:, n0:n0 + FF_CHUNK] * acc[i0]


def _ffn(x, mod, layer, row0, mod_span, seq_len, halo, norm2_g, w_up_b, conv_w, w_down_b):
    n = x.shape[0]
    tile = FFN_TILE
    hb = tile // BF16_ROWS
    n_hblocks = n // BF16_ROWS
    in_specs = [pl.BlockSpec((tile, D_MODEL), lambda i: (i, 0))]
    args = [x]
    if halo:
        in_specs += [
            pl.BlockSpec((HALO, D_MODEL), lambda i: (jnp.maximum(i * hb - 1, 0), 0)),
            pl.BlockSpec((HALO, D_MODEL), lambda i: (jnp.minimum((i + 1) * hb, n_hblocks - 1), 0)),
        ]
        args += [x, x]
    in_specs += [
        _mod_spec(layer, lambda i: row0 + (i * tile) // mod_span),
        _const_spec((DEPTH, D_MODEL)),
        _const_spec((D_MODEL, 2 * D_FF)),
        _layer_spec((3, 2 * D_FF), layer),
        _const_spec((D_FF, D_MODEL)),
    ]
    args += [mod, norm2_g, w_up_b, conv_w, w_down_b]
    if halo:
        rows_h = rows_u = tile + 2 * HALO
    else:
        rows_h = tile
        rows_u = _U_PAD + (tile // seq_len) * (seq_len + _U_PAD)
    return pl.pallas_call(
        functools.partial(_ffn_kernel, tile=tile, seq_len=seq_len, halo=halo, layer=layer),
        grid=(n // tile,),
        in_specs=in_specs,
        out_specs=pl.BlockSpec((tile, D_MODEL), lambda i: (i, 0)),
        out_shape=jax.ShapeDtypeStruct(x.shape, F32),
        scratch_shapes=[pltpu.VMEM((rows_h, D_MODEL), BF16),
                        pltpu.VMEM((2, 2 * _U_COLS, rows_u, LANES), F32),
                        pltpu.VMEM((tile, D_FF), BF16)],
        compiler_params=_cparams(1),
        name="ffn",
    )(*args)


def _rope_tables(n_tok):
    rows = n_tok // GRID_W
    row = np.repeat(np.arange(rows, dtype=np.float32), GRID_W)
    col = np.tile(np.arange(GRID_W, dtype=np.float32), rows)
    n_freq = HEAD_DIM // 4
    inv = (np.float32(ROPE_BASE) ** (-np.arange(n_freq, dtype=np.float32) / np.float32(n_freq)))
    inv = inv.astype(np.float32)
    ar, ac = row[:, None] * inv[None], col[:, None] * inv[None]
    cr, sr, cc, sc = np.cos(ar), np.sin(ar), np.cos(ac), np.sin(ac)
    z = np.zeros_like(sr)
    rep = lambda parts: jnp.asarray(
        np.tile(np.concatenate(parts, axis=-1), (1, LANES // HEAD_DIM)).astype(np.float32))
    return rep([cr, cr, cc, cc]), rep([-sr, z, -sc, z]), rep([z, sr, z, sc])


def kernel(x_prompt, x_sample, cache_k, cache_v, c, c_ctx, w_mod, b_mod, norm1_g, w_in, q_norm_g, k_norm_g, lambda_q1, lambda_k1, lambda_q2, lambda_k2, subln_g, conv_w, conv_norm_g, w_out, norm2_g, w_up, ffn_conv_w, w_down):
    batch, seq, _ = x_prompt.shape
    dec_batch, dec_seq, _ = x_sample.shape
    past = cache_k.shape[2]

    cvecs = jnp.concatenate(
        [c_ctx[None], c, jnp.zeros((MOD_ROWS - 1 - dec_batch, D_MODEL), F32)], axis=0)
    mod = _modulation(cvecs, w_mod, b_mod).reshape(DEPTH, MOD_ROWS, 1, N_MOD * D_MODEL)

    grp = np.arange(D_ATTN) // HEAD_DIM
    gmat = jnp.asarray(np.where(grp[:, None] == grp[None, :], 1.0 / HEAD_DIM, 0.0), dtype=BF16)
    qg = jnp.tile(q_norm_g, (1, D_ATTN // HEAD_DIM))
    kg = jnp.tile(k_norm_g, (1, D_ATTN // HEAD_DIM))
    n1, n2, cng = norm1_g, norm2_g, conv_norm_g
    sg = jnp.broadcast_to(subln_g[:, :, None], (DEPTH, V_DIM, LANES))
    lam_ps = (lambda_q1, lambda_k1, lambda_q2, lambda_k2)
    rope_tabs = _rope_tables(dec_seq)
    cache = (cache_k.reshape(dec_batch, DEPTH, past, D_ATTN),
             cache_v.reshape(dec_batch, DEPTH, past, D_ATTN))

    ffn_w = {}

    def run(x, row0, mod_span, seq_len, tq, ffn_halo, rope, cache, first_stream):
        kv = "new" if first_stream else None
        for l in range(DEPTH):
            lam_init = 0.8 - 0.6 * math.exp(-0.3 * l)
            outs = _proj(x, mod, l, row0, mod_span, seq_len, n1, w_in, qg, kg, gmat, rope, kv)
            q, k, v, u, bg = outs[:5]
            if first_stream:
                kv = tuple(outs[5:])
            x, cast = _attn(x, mod, l, row0, mod_span, seq_len, tq, q, k, v, cache, u, bg, lam_ps,
                            sg, conv_w, cng, w_out, lam_init,
                            ffn_weights=(w_up, w_down) if first_stream else None)
            if first_stream:
                ffn_w[l] = cast
            x = _ffn(x, mod, l, row0, mod_span, seq_len, ffn_halo, n2, ffn_w[l][0], ffn_conv_w,
                     ffn_w[l][1])
        return x, kv

    xp, (new_k, new_v) = run(x_prompt.reshape(batch * seq, D_MODEL), 0, batch * seq, seq,
                             CTX_Q_TILE, False, None, None, True)
    xs, _ = run(x_sample.reshape(dec_batch * dec_seq, D_MODEL), 1, dec_seq, dec_seq, Q_TILE, True,
                rope_tabs, cache, False)

    return (xp.reshape(batch, seq, D_MODEL), xs.reshape(dec_batch, dec_seq, D_MODEL),
            new_k.reshape(batch, DEPTH, seq, N_HEADS, 2, HEAD_DIM),
            new_v.reshape(batch, DEPTH, seq, N_HEADS, V_DIM))
```

```python
import functools
import math

import jax
import jax.numpy as jnp
import numpy as np
from jax import lax
from jax.experimental import pallas as pl
from jax.experimental.pallas import tpu as pltpu

D_MODEL = 1024
DEPTH = 2
GRID_W = 64
D_ATTN = 512
D_CONV = 512
N_HEADS = 4
HEAD_DIM = 64
V_DIM = 128
D_FF = 2816
ROPE_BASE = 10000.0
EPS = 1e-6
N_MOD = 6
D_IN = 3 * D_ATTN + 3 * D_CONV

F32 = jnp.float32
BF16 = jnp.bfloat16

LANES = 128
BF16_ROWS = 16
F32_ROWS = 8
MOD_ROWS = F32_ROWS
_U_PAD = F32_ROWS
VMEM_LIMIT = 60 * 1024 * 1024

FF_CHUNK = 256
N_FF_CHUNKS = D_FF // FF_CHUNK
_U_COLS = FF_CHUNK // LANES
_DOWN_PARTS = (8, N_FF_CHUNKS)
HALO = BF16_ROWS
MOD_TILE = 1536
TOKEN_TILE = 512
FFN_TILE = 512
Q_TILE = 512
CTX_Q_TILE = 512


def _cparams(n_axes, flags=None):
    return pltpu.CompilerParams(
        dimension_semantics=("arbitrary",) * n_axes, vmem_limit_bytes=VMEM_LIMIT, flags=flags)


def _layer_spec(shape, layer):
    nd = len(shape)
    return pl.BlockSpec((None,) + tuple(shape), lambda *_: (layer,) + (0,) * nd,
                        pipeline_mode=pl.Buffered(1))


def _const_spec(shape):
    nd = len(shape)
    return pl.BlockSpec(shape, lambda *_: (0,) * nd, pipeline_mode=pl.Buffered(1))


def _mod_spec(layer, row_of):
    return pl.BlockSpec((None, None, 1, N_MOD * D_MODEL),
                        lambda *idx: (layer, row_of(*idx), 0, 0))


def _rms_rows(x):
    return lax.rsqrt(jnp.mean(x * x, axis=-1, keepdims=True) + EPS)


def _mod_kernel(cv_ref, w_ref, b_ref, o_ref):
    cv = cv_ref[...]
    s = (cv * jax.nn.sigmoid(cv)).astype(BF16)
    o_ref[...] = jnp.dot(s, w_ref[...].astype(BF16), preferred_element_type=F32) + b_ref[...]


def _modulation(cvecs, w_mod, b_mod):
    tn = MOD_TILE
    n_out = N_MOD * D_MODEL
    return pl.pallas_call(
        _mod_kernel,
        grid=(DEPTH, n_out // tn),
        in_specs=[
            pl.BlockSpec((MOD_ROWS, D_MODEL), lambda l, j: (0, 0)),
            pl.BlockSpec((None, D_MODEL, tn), lambda l, j: (l, 0, j)),
            pl.BlockSpec((None, 1, tn), lambda l, j: (l, 0, j)),
        ],
        out_specs=pl.BlockSpec((None, MOD_ROWS, tn), lambda l, j: (l, 0, j)),
        out_shape=jax.ShapeDtypeStruct((DEPTH, MOD_ROWS, n_out), F32),
        compiler_params=_cparams(2),
        name="adaln_mod",
    )(cvecs, w_mod, b_mod.reshape(DEPTH, 1, n_out))


def _group_norm_gain(z, gmat_ref, gain):
    zz = (z * z).astype(BF16)
    ms = jnp.concatenate(
        [jnp.dot(zz[:, j:j + 256], gmat_ref[j:j + 256, j:j + 256], preferred_element_type=F32)
         for j in range(0, D_ATTN, 256)], axis=1)
    return z * lax.rsqrt(ms + EPS) * gain


def _rope(z, c_ref, sa_ref, sb_ref):
    c, sa, sb = c_ref[...], sa_ref[...], sb_ref[...]
    cols = []
    for j in range(z.shape[1] // LANES):
        zj = z[:, j * LANES:(j + 1) * LANES]
        hi = pltpu.roll(zj, LANES - HEAD_DIM // 4, axis=1)
        lo = pltpu.roll(zj, HEAD_DIM // 4, axis=1)
        cols.append(zj * c + hi * sa + lo * sb)
    return jnp.concatenate(cols, axis=1)


def _proj_kernel(*refs, rope, emit_f32, seq_len, layer):
    it = iter(refs)
    x_ref, m_ref, g1_ref, w_ref, qg_ref, kg_ref, gmat_ref = (next(it) for _ in range(7))
    if rope:
        c_ref, sa_ref, sb_ref = (next(it) for _ in range(3))
    if emit_f32 == "alias":
        next(it), next(it)
    q_ref, k_ref, v_ref, u_ref, bg_ref = (next(it) for _ in range(5))
    if emit_f32:
        kf_ref, vf_ref = next(it), next(it)

    x = x_ref[...]
    m = m_ref[...]
    sh1 = m[:, 0:D_MODEL]
    sc1 = m[:, D_MODEL:2 * D_MODEL]
    row = slice(layer, layer + 1)
    h = ((x * _rms_rows(x)) * g1_ref[row, :]) * (1.0 + sc1) + sh1
    hb = h.astype(BF16)

    def col(i0, n):
        return jnp.dot(hb, w_ref[:, i0:i0 + n].astype(BF16), preferred_element_type=F32)

    q = _group_norm_gain(col(0, D_ATTN), gmat_ref, qg_ref[row, :])
    k = _group_norm_gain(col(D_ATTN, D_ATTN), gmat_ref, kg_ref[row, :])
    if rope:
        q = _rope(q, c_ref, sa_ref, sb_ref)
        k = _rope(k, c_ref, sa_ref, sb_ref)
    v = col(2 * D_ATTN, D_ATTN)
    q_ref[...] = (q * (HEAD_DIM ** -0.5 * math.log2(math.e))).astype(BF16)
    k_ref[...] = k.astype(BF16)
    v_ref[...] = v.T.astype(BF16)
    def put_v(dst):
        for j in range(x.shape[0] // seq_len):
            for hd in range(N_HEADS):
                dst(j)[pl.ds(hd, seq_len, stride=N_HEADS), :] = (
                    v[j * seq_len:(j + 1) * seq_len, hd * V_DIM:(hd + 1) * V_DIM])

    if emit_f32 == "alias":
        n_seq = x.shape[0] // seq_len
        kf_ref[...] = k.reshape(n_seq, seq_len, D_ATTN)
        put_v(lambda j: vf_ref.at[j])
    elif emit_f32:
        n_seq = x.shape[0] // seq_len
        zero = jnp.zeros((n_seq, seq_len, D_ATTN), F32)
        for l in range(DEPTH):
            kf_ref[:, l] = k.reshape(n_seq, seq_len, D_ATTN) if l == layer else zero
            if l == layer:
                put_v(lambda j: vf_ref.at[j, l])
            else:
                vf_ref[:, l] = jnp.zeros((n_seq, seq_len * N_HEADS, V_DIM), F32)
    bg_ref[...] = col(3 * D_ATTN, D_CONV).astype(BF16)
    cg = col(3 * D_ATTN + D_CONV, D_CONV)
    xc = col(3 * D_ATTN + 2 * D_CONV, D_CONV)
    u_ref[...] = (cg * xc).astype(BF16)


def _proj(x, mod, layer, row0, mod_span, seq_len, norm1_g, w_in, qg, kg, gmat, rope_tabs, kv_out):
    n = x.shape[0]
    tile = TOKEN_TILE
    tiles_per_seq = max(seq_len // tile, 1)
    in_specs = [
        pl.BlockSpec((tile, D_MODEL), lambda i: (i, 0)),
        _mod_spec(layer, lambda i: row0 + (i * tile) // mod_span),
        _const_spec((DEPTH, D_MODEL)),
        _layer_spec((D_MODEL, D_IN), layer),
        _const_spec((DEPTH, D_ATTN)),
        _const_spec((DEPTH, D_ATTN)),
        _const_spec((D_ATTN, D_ATTN)),
    ]
    args = [x, mod, norm1_g, w_in, qg, kg, gmat]
    if rope_tabs is not None:
        for t in rope_tabs:
            in_specs.append(pl.BlockSpec((tile, LANES), lambda i: (i % tiles_per_seq, 0)))
            args.append(t)
    aliases = {}
    emit = False
    if kv_out is not None and kv_out != "new":
        emit = "alias"
        for a in kv_out:
            aliases[len(args)] = 5 + len(aliases)
            in_specs.append(pl.BlockSpec(memory_space=pl.ANY))
            args.append(a)
    elif kv_out == "new":
        emit = True
    tok = jax.ShapeDtypeStruct((n, D_ATTN), BF16)
    tok_spec = pl.BlockSpec((tile, D_ATTN), lambda i: (i, 0))
    out_shape = [tok, tok, jax.ShapeDtypeStruct((D_ATTN, n), BF16), tok, tok]
    out_specs = [tok_spec, tok_spec, pl.BlockSpec((D_ATTN, tile), lambda i: (0, i)), tok_spec,
                 tok_spec]
    if emit:
        n_seq = tile // seq_len
        kv_shape = jax.ShapeDtypeStruct((n // seq_len, DEPTH, seq_len, D_ATTN), F32)
        v_rows = seq_len * N_HEADS
        out_shape += [kv_shape, jax.ShapeDtypeStruct((n // seq_len, DEPTH, v_rows, V_DIM), F32)]
        if emit == "alias":
            out_specs += [
                pl.BlockSpec((n_seq, None, seq_len, D_ATTN), lambda i: (i, layer, 0, 0)),
                pl.BlockSpec((n_seq, None, v_rows, V_DIM), lambda i: (i, layer, 0, 0))]
        else:
            out_specs += [
                pl.BlockSpec((n_seq, DEPTH, seq_len, D_ATTN), lambda i: (i, 0, 0, 0)),
                pl.BlockSpec((n_seq, DEPTH, v_rows, V_DIM), lambda i: (i, 0, 0, 0))]
    return pl.pallas_call(
        functools.partial(_proj_kernel, rope=rope_tabs is not None, emit_f32=emit, seq_len=seq_len,
                          layer=layer),
        grid=(n // tile,),
        in_specs=in_specs,
        out_specs=out_specs,
        out_shape=out_shape,
        input_output_aliases=aliases,
        compiler_params=_cparams(1),
        name="proj",
    )(*args)


def _attn_kernel(*refs, has_cache, cast_ffn, tq, seq_len, layer, lam_init):
    it = iter(refs)
    x_ref, m_ref, q_ref, k_ref, vt_ref = (next(it) for _ in range(5))
    if has_cache:
        kc_ref, vc_ref = next(it), next(it)
    u_ref, up_ref, un_ref, bg_ref = (next(it) for _ in range(4))
    lam_refs = [next(it) for _ in range(4)]
    sg_ref, cw_ref, cg_ref, wo_ref = (next(it) for _ in range(4))
    if cast_ffn:
        wu_ref, wd_ref = next(it), next(it)
    o_ref = next(it)
    if cast_ffn:
        wub_ref, wdb_ref = next(it), next(it)
        wub_ref[...] = wu_ref[...].astype(BF16)
        wdb_ref[...] = wd_ref[...].astype(BF16)
    cu_ref = next(it)

    lq1, lk1, lq2, lk2 = (r[layer:layer + 1, :] for r in lam_refs)
    lam = (jnp.exp(jnp.sum(lq1 * lk1, axis=-1, keepdims=True))
           - jnp.exp(jnp.sum(lq2 * lk2, axis=-1, keepdims=True)) + lam_init)

    nq = min(tq, seq_len)
    if seq_len >= tq:
        groups = [(0, 0, k_ref.shape[0])]
    else:
        groups = [(g * seq_len, g * seq_len, seq_len) for g in range(tq // seq_len)]

    lane = lax.broadcasted_iota(jnp.int32, (1, LANES), 1)
    sub0 = jnp.where(lane < HEAD_DIM, 1.0, 0.0).astype(BF16)
    sub1 = jnp.where(lane < HEAD_DIM, 0.0, 1.0).astype(BF16)
    nt = (((1,), (1,)), ((), ()))
    ones = jnp.ones((BF16_ROWS, groups[0][2]), BF16)
    sgain = jnp.concatenate([sg_ref[...]] * (nq // LANES), axis=1)
    if has_cache:
        nc = kc_ref.shape[0]
        ones_c = jnp.ones((BF16_ROWS, nc), BF16)
        vct = vc_ref[...].T.astype(BF16)

    def scores(job):
        (q0, k0, nk), h = job
        sl = slice(h * LANES, (h + 1) * LANES)
        qh = q_ref[q0:q0 + nq, sl]
        q2 = jnp.concatenate([qh * sub0, qh * sub1], axis=0)
        st = lax.dot_general(k_ref[k0:k0 + nk, sl], q2, nt, preferred_element_type=F32)
        if not has_cache:
            return st, None
        return st, lax.dot_general(kc_ref[:, sl].astype(BF16), q2, nt,
                                   preferred_element_type=F32)

    jobs = [(grp, h) for grp in groups for h in range(N_HEADS)]
    nxt = scores(jobs[0])

    j = pl.program_id(1)
    u = u_ref[...].astype(F32)
    n_cols = D_CONV // LANES
    if seq_len >= tq:
        segs = [(F32_ROWS, tq, 0)]
        tiles_per_seq = seq_len // tq
        prev = jnp.where(j % tiles_per_seq == 0, 0.0,
                         up_ref[BF16_ROWS - 1:BF16_ROWS, :].astype(F32))
        nxt_row = jnp.where(j % tiles_per_seq == tiles_per_seq - 1, 0.0,
                            un_ref[0:1, :].astype(F32))
        for c in range(n_cols):
            lanes = slice(c * LANES, (c + 1) * LANES)
            cu_ref[c, F32_ROWS - 1:F32_ROWS, :] = prev[:, lanes]
            cu_ref[c, F32_ROWS + tq:F32_ROWS + tq + 1, :] = nxt_row[:, lanes]
    else:
        segs = [(_U_PAD + s * (seq_len + _U_PAD), seq_len, s * seq_len)
                for s in range(tq // seq_len)]

        @pl.when(jnp.logical_and(pl.program_id(0) == 0, j == 0))
        def _():
            cu_ref[...] = jnp.zeros_like(cu_ref)

    for c in range(n_cols):
        for b0, ln, r0 in segs:
            cu_ref[c, b0:b0 + ln, :] = u[r0:r0 + ln, c * LANES:(c + 1) * LANES]
    cw = cw_ref[...]
    conv_cols = []
    for c in range(n_cols):
        w3 = cw[:, c * LANES:(c + 1) * LANES]
        conv_cols.append(jnp.concatenate(
            [cu_ref[c, b0 - 1:b0 - 1 + ln, :] * w3[0:1] + cu_ref[c, b0:b0 + ln, :] * w3[1:2]
             + cu_ref[c, b0 + 1:b0 + 1 + ln, :] * w3[2:3] for b0, ln, r0 in segs], axis=0))
    t = bg_ref[...].astype(F32) * jnp.concatenate(conv_cols, axis=1)
    y = ((t * _rms_rows(t)) * cg_ref[layer:layer + 1, :]).astype(BF16)
    y_proj = jnp.dot(y, wo_ref[D_ATTN:, :].astype(BF16), preferred_element_type=F32)

    outs = []
    for n, ((q0, k0, nk), h) in enumerate(jobs):
        sl = slice(h * LANES, (h + 1) * LANES)
        st, sct = nxt
        if n + 1 < len(jobs):
            nxt = scores(jobs[n + 1])
        mx = jnp.max(st, axis=0, keepdims=True)
        if has_cache:
            mx = jnp.maximum(mx, jnp.max(sct, axis=0, keepdims=True))
        e = jnp.exp2(st - mx).astype(BF16)
        oe = jnp.dot(jnp.concatenate([vt_ref[sl, k0:k0 + nk], ones], axis=0), e,
                     preferred_element_type=F32)
        if has_cache:
            ec = jnp.exp2(sct - mx).astype(BF16)
            oe = oe + jnp.dot(jnp.concatenate([vct[sl, :], ones_c], axis=0), ec,
                              preferred_element_type=F32)
        den = oe[V_DIM:V_DIM + 1, :]
        c0 = 1.0 / den[:, :nq]
        c1 = lam / den[:, nq:]
        oh = oe[:V_DIM, :nq] * c0 - oe[:V_DIM, nq:] * c1
        r = lax.rsqrt(jnp.mean(oh * oh, axis=0, keepdims=True) + EPS)
        outs.append(((oh * r) * sgain) * (1.0 - lam_init))
    ot = jnp.concatenate(
        [jnp.concatenate(outs[g * N_HEADS:(g + 1) * N_HEADS], axis=0) for g in range(len(groups))],
        axis=1)
    o = ot.T.astype(BF16)
    mix = y_proj + jnp.dot(o, wo_ref[0:D_ATTN, :].astype(BF16), preferred_element_type=F32)
    g1 = m_ref[...][:, 2 * D_MODEL:3 * D_MODEL]
    o_ref[...] = x_ref[...] + g1 * mix


def _attn(x, mod, layer, row0, mod_span, seq_len, tq, q, k, vt, cache, u, bg, lam_ps, subln_g,
          conv_w, conv_g, w_out, lam_init, ffn_weights=None):
    n = x.shape[0]
    grp_rows = max(seq_len, tq)
    qt = grp_rows // tq
    hb = tq // BF16_ROWS
    n_hblocks = n // BF16_ROWS
    tokmap = lambda b, j: (b * qt + j, 0)
    in_specs = [
        pl.BlockSpec((tq, D_MODEL), tokmap),
        _mod_spec(layer, lambda b, j: row0 + (b * grp_rows) // mod_span),
        pl.BlockSpec((tq, D_ATTN), tokmap),
        pl.BlockSpec((grp_rows, D_ATTN), lambda b, j: (b, 0)),
        pl.BlockSpec((D_ATTN, grp_rows), lambda b, j: (0, b)),
    ]
    args = [x, mod, q, k, vt]
    if cache is not None:
        for c in cache:
            p = c.shape[2]
            in_specs.append(pl.BlockSpec((None, None, p, D_ATTN), lambda b, j: (b, layer, 0, 0)))
            args.append(c)
    in_specs += [
        pl.BlockSpec((tq, D_CONV), tokmap),
        pl.BlockSpec((BF16_ROWS, D_CONV), lambda b, j: (jnp.maximum((b * qt + j) * hb - 1, 0), 0)),
        pl.BlockSpec((BF16_ROWS, D_CONV),
                     lambda b, j: (jnp.minimum((b * qt + j + 1) * hb, n_hblocks - 1), 0)),
        pl.BlockSpec((tq, D_CONV), tokmap),
    ]
    in_specs += [_const_spec((DEPTH, HEAD_DIM))] * 4
    in_specs += [
        _layer_spec((V_DIM, LANES), layer),
        _layer_spec((3, D_CONV), layer),
        _const_spec((DEPTH, D_CONV)),
        _layer_spec((D_MODEL, D_MODEL), layer),
    ]
    args += [u, u, u, bg, *lam_ps, subln_g, conv_w, conv_g, w_out]
    out_specs = [pl.BlockSpec((tq, D_MODEL), tokmap)]
    out_shape = [jax.ShapeDtypeStruct(x.shape, F32)]
    if ffn_weights is not None:
        n_steps = n // tq
        for w in ffn_weights:
            rows, cols = w.shape[1:]
            slab = rows // n_steps
            assert slab * n_steps == rows and slab % BF16_ROWS == 0, (rows, n_steps)
            in_specs.append(pl.BlockSpec((None, slab, cols), lambda b, j: (layer, b * qt + j, 0)))
            out_specs.append(pl.BlockSpec((slab, cols), lambda b, j: (b * qt + j, 0)))
            out_shape.append(jax.ShapeDtypeStruct((rows, cols), BF16))
            args.append(w)
    if seq_len >= tq:
        rows_c = tq + 2 * F32_ROWS
    else:
        rows_c = _U_PAD + (tq // seq_len) * (seq_len + _U_PAD)
    outs = pl.pallas_call(
        functools.partial(_attn_kernel, has_cache=cache is not None,
                          cast_ffn=ffn_weights is not None, tq=tq, seq_len=seq_len, layer=layer,
                          lam_init=lam_init),
        grid=(n // grp_rows, qt),
        in_specs=in_specs,
        out_specs=out_specs,
        out_shape=out_shape,
        scratch_shapes=[pltpu.VMEM((D_CONV // LANES, rows_c, LANES), F32)],
        compiler_params=_cparams(2),
        name="attn",
    )(*args)
    return outs[0], tuple(outs[1:])


def _ffn_kernel(*refs, tile, seq_len, halo, layer):
    it = iter(refs)
    x_ref = next(it)
    if halo:
        xp_ref, xn_ref = next(it), next(it)
    m_ref, g2_ref, wup_ref, cw_ref, wdn_ref, o_ref, hs_ref, u_ref, g_ref = (
        next(it) for _ in range(9))

    m = m_ref[...]
    sh2 = m[:, 3 * D_MODEL:4 * D_MODEL]
    sc2 = m[:, 4 * D_MODEL:5 * D_MODEL]
    gate = m[:, 5 * D_MODEL:6 * D_MODEL]

    gain = g2_ref[layer:layer + 1, :] * (1.0 + sc2)

    def pre(xv):
        return (xv * _rms_rows(xv)) * gain + sh2

    x = x_ref[...]
    i = pl.program_id(0)
    base = HALO if halo else 0
    hs_ref[base:base + tile, :] = pre(x).astype(BF16)
    if halo:
        tiles_per_seq = seq_len // tile
        jt = i % tiles_per_seq
        hs_ref[0:HALO, :] = jnp.where(jt == 0, 0.0, pre(xp_ref[...])).astype(BF16)
        hs_ref[HALO + tile:, :] = jnp.where(jt == tiles_per_seq - 1, 0.0,
                                            pre(xn_ref[...])).astype(BF16)
    if halo:
        segs = [(HALO, tile, 0)]
    else:
        segs = [(_U_PAD + s * (seq_len + _U_PAD), seq_len, s * seq_len)
                for s in range(tile // seq_len)]

        @pl.when(i == 0)
        def _():
            u_ref[...] = jnp.zeros_like(u_ref)

    hs = hs_ref[...]
    acc, done = None, 0
    for c in range(N_FF_CHUNKS):
        slot = c % 2
        cols = (c * FF_CHUNK, D_FF + c * FF_CHUNK)
        for half, c0 in enumerate(cols):
            uu = jnp.dot(hs, wup_ref[:, c0:c0 + FF_CHUNK], preferred_element_type=F32)
            for j in range(_U_COLS):
                col = uu[:, j * LANES:(j + 1) * LANES]
                if halo:
                    u_ref[slot, half * _U_COLS + j] = col
                else:
                    for b0, ln, r0 in segs:
                        u_ref[slot, half * _U_COLS + j, b0:b0 + ln, :] = col[r0:r0 + ln]
        for b0, ln, r0 in segs:
            conv = []
            for half, c0 in enumerate(cols):
                w3 = cw_ref[:, c0:c0 + FF_CHUNK]
                taps = []
                for j in range(_U_COLS):
                    uc = u_ref.at[slot, half * _U_COLS + j]
                    w3j = w3[:, j * LANES:(j + 1) * LANES]
                    taps.append(uc[b0 - 1:b0 - 1 + ln, :] * w3j[0:1]
                                + uc[b0:b0 + ln, :] * w3j[1:2]
                                + uc[b0 + 1:b0 + 1 + ln, :] * w3j[2:3])
                conv.append(jnp.concatenate(taps, axis=1))
            ca, cb = conv
            g_ref[r0:r0 + ln, c * FF_CHUNK:(c + 1) * FF_CHUNK] = (
                (ca * jax.nn.sigmoid(ca)) * cb).astype(BF16)
        if c + 1 in _DOWN_PARTS:
            k0, k1 = done * FF_CHUNK, (c + 1) * FF_CHUNK
            part = [jnp.dot(g_ref[:, k0:k1], wdn_ref[k0:k1, n0:n0 + FF_CHUNK],
                            preferred_element_type=F32) for n0 in range(0, D_MODEL, FF_CHUNK)]
            acc = part if acc is None else [a + p for a, p in zip(acc, part)]
            done = c + 1

    for i0, n0 in enumerate(range(0, D_MODEL, FF_CHUNK)):
        o_ref[:, n0:n0 + FF_CHUNK] = x[:, n0:n0 + FF_CHUNK] + gate[:, n0:n0 + FF_CHUNK] * acc[i0]


def _ffn(x, mod, layer, row0, mod_span, seq_len, halo, norm2_g, w_up_b, conv_w, w_down_b):
    n = x.shape[0]
    tile = FFN_TILE
    hb = tile // BF16_ROWS
    n_hblocks = n // BF16_ROWS
    in_specs = [pl.BlockSpec((tile, D_MODEL), lambda i: (i, 0))]
    args = [x]
    if halo:
        in_specs += [
            pl.BlockSpec((HALO, D_MODEL), lambda i: (jnp.maximum(i * hb - 1, 0), 0)),
            pl.BlockSpec((HALO, D_MODEL), lambda i: (jnp.minimum((i + 1) * hb, n_hblocks - 1), 0)),
        ]
        args += [x, x]
    in_specs += [
        _mod_spec(layer, lambda i: row0 + (i * tile) // mod_span),
        _const_spec((DEPTH, D_MODEL)),
        _const_spec((D_MODEL, 2 * D_FF)),
        _layer_spec((3, 2 * D_FF), layer),
        _const_spec((D_FF, D_MODEL)),
    ]
    args += [mod, norm2_g, w_up_b, conv_w, w_down_b]
    if halo:
        rows_h = rows_u = tile + 2 * HALO
    else:
        rows_h = tile
        rows_u = _U_PAD + (tile // seq_len) * (seq_len + _U_PAD)
    return pl.pallas_call(
        functools.partial(_ffn_kernel, tile=tile, seq_len=seq_len, halo=halo, layer=layer),
        grid=(n // tile,),
        in_specs=in_specs,
        out_specs=pl.BlockSpec((tile, D_MODEL), lambda i: (i, 0)),
        out_shape=jax.ShapeDtypeStruct(x.shape, F32),
        scratch_shapes=[pltpu.VMEM((rows_h, D_MODEL), BF16),
                        pltpu.VMEM((2, 2 * _U_COLS, rows_u, LANES), F32),
                        pltpu.VMEM((tile, D_FF), BF16)],
        compiler_params=_cparams(1),
        name="ffn",
    )(*args)


def _rope_tables(n_tok):
    rows = n_tok // GRID_W
    row = np.repeat(np.arange(rows, dtype=np.float32), GRID_W)
    col = np.tile(np.arange(GRID_W, dtype=np.float32), rows)
    n_freq = HEAD_DIM // 4
    inv = (np.float32(ROPE_BASE) ** (-np.arange(n_freq, dtype=np.float32) / np.float32(n_freq)))
    inv = inv.astype(np.float32)
    ar, ac = row[:, None] * inv[None], col[:, None] * inv[None]
    cr, sr, cc, sc = np.cos(ar), np.sin(ar), np.cos(ac), np.sin(ac)
    z = np.zeros_like(sr)
    rep = lambda parts: jnp.asarray(
        np.tile(np.concatenate(parts, axis=-1), (1, LANES // HEAD_DIM)).astype(np.float32))
    return rep([cr, cr, cc, cc]), rep([-sr, z, -sc, z]), rep([z, sr, z, sc])


def kernel(x_prompt, x_sample, cache_k, cache_v, c, c_ctx, w_mod, b_mod, norm1_g, w_in, q_norm_g, k_norm_g, lambda_q1, lambda_k1, lambda_q2, lambda_k2, subln_g, conv_w, conv_norm_g, w_out, norm2_g, w_up, ffn_conv_w, w_down):
    batch, seq, _ = x_prompt.shape
    dec_batch, dec_seq, _ = x_sample.shape
    past = cache_k.shape[2]

    cvecs = jnp.concatenate(
        [c_ctx[None], c, jnp.zeros((MOD_ROWS - 1 - dec_batch, D_MODEL), F32)], axis=0)
    mod = _modulation(cvecs, w_mod, b_mod).reshape(DEPTH, MOD_ROWS, 1, N_MOD * D_MODEL)

    grp = np.arange(D_ATTN) // HEAD_DIM
    gmat = jnp.asarray(np.where(grp[:, None] == grp[None, :], 1.0 / HEAD_DIM, 0.0), dtype=BF16)
    qg = jnp.tile(q_norm_g, (1, D_ATTN // HEAD_DIM))
    kg = jnp.tile(k_norm_g, (1, D_ATTN // HEAD_DIM))
    n1, n2, cng = norm1_g, norm2_g, conv_norm_g
    sg = jnp.broadcast_to(subln_g[:, :, None], (DEPTH, V_DIM, LANES))
    lam_ps = (lambda_q1, lambda_k1, lambda_q2, lambda_k2)
    rope_tabs = _rope_tables(dec_seq)
    cache = (cache_k.reshape(dec_batch, DEPTH, past, D_ATTN),
             cache_v.reshape(dec_batch, DEPTH, past, D_ATTN))

    ffn_w = {}

    def run(x, row0, mod_span, seq_len, tq, ffn_halo, rope, cache, first_stream):
        kv = "new" if first_stream else None
        for l in range(DEPTH):
            lam_init = 0.8 - 0.6 * math.exp(-0.3 * l)
            outs = _proj(x, mod, l, row0, mod_span, seq_len, n1, w_in, qg, kg, gmat, rope, kv)
            q, k, v, u, bg = outs[:5]
            if first_stream:
                kv = tuple(outs[5:])
            x, cast = _attn(x, mod, l, row0, mod_span, seq_len, tq, q, k, v, cache, u, bg, lam_ps,
                            sg, conv_w, cng, w_out, lam_init,
                            ffn_weights=(w_up, w_down) if first_stream else None)
            if first_stream:
                ffn_w[l] = cast
            x = _ffn(x, mod, l, row0, mod_span, seq_len, ffn_halo, n2, ffn_w[l][0], ffn_conv_w,
                     ffn_w[l][1])
        return x, kv

    xp, (new_k, new_v) = run(x_prompt.reshape(batch * seq, D_MODEL), 0, batch * seq, seq,
                             CTX_Q_TILE, False, None, None, True)
    xs, _ = run(x_sample.reshape(dec_batch * dec_seq, D_MODEL), 1, dec_seq, dec_seq, Q_TILE, True,
                rope_tabs, cache, False)

    return (xp.reshape(batch, seq, D_MODEL), xs.reshape(dec_batch, dec_seq, D_MODEL),
            new_k.reshape(batch, DEPTH, seq, N_HEADS, 2, HEAD_DIM),
            new_v.reshape(batch, DEPTH, seq, N_HEADS, V_DIM))
```
